```python
import math
import jax, jax.numpy as jnp
from jax import lax
import numpy as np

D_MODEL = 2048
BATCH = 8
SEQ = 8192
DEPTH = 2

CHUNK = 64
Q_BLOCK = 128
EPS = 1e-6

MLA_HEADS = 16
MLA_Q_RANK = 512
MLA_KV_RANK = 512
MLA_NOPE = 128
MLA_ROPE = 64
MLA_V = 128
ROPE_THETA = 10000.0

HG_HEADS = 16
HG_DK = 128
HG_DV = 128
HG_WIDTH = HG_HEADS * HG_DK

SSM_EXPAND = 2
SSM_INNER = SSM_EXPAND * D_MODEL
SSM_HEADDIM = 64
SSM_HEADS = SSM_INNER // SSM_HEADDIM
SSM_GROUPS = 8
SSM_STATE = 128
SSM_CONV = 4
SSM_CONV_DIM = SSM_INNER + 2 * SSM_GROUPS * SSM_STATE

D_FF = 5632
N_BRANCH = 3

IN_SIZES = (MLA_Q_RANK, MLA_KV_RANK, MLA_ROPE,
            HG_WIDTH, HG_WIDTH, HG_WIDTH, HG_WIDTH,
            SSM_INNER, SSM_CONV_DIM, SSM_HEADS,
            N_BRANCH * D_MODEL)
IN_DIM = int(sum(IN_SIZES))
IN_SPLITS = tuple(int(v) for v in np.cumsum(IN_SIZES)[:-1])

kernel_name = "hybrid_mla_hgrn2_mamba2_macaron"


def rmsnorm(x, w):
    xf = x.astype(jnp.float32)
    y = xf * lax.rsqrt(jnp.mean(xf * xf, axis=-1, keepdims=True) + EPS)
    return (y * w.astype(jnp.float32)).astype(x.dtype)


def swiglu(x, w_gate_up, w_down):
    g, u = jnp.split(x @ w_gate_up, 2, axis=-1)
    return (jax.nn.silu(g) * u) @ w_down


def rope_tables(seq, dim):
    inv = 1.0 / (ROPE_THETA ** (jnp.arange(0, dim, 2, dtype=jnp.float32) / dim))
    ang = jnp.arange(seq, dtype=jnp.float32)[:, None] * inv[None, :]
    return jnp.cos(ang), jnp.sin(ang)


def apply_rope(x, cos, sin):
    x1, x2 = jnp.split(x.astype(jnp.float32), 2, axis=-1)
    return jnp.concatenate([x1 * cos - x2 * sin, x1 * sin + x2 * cos], axis=-1).astype(x.dtype)


def tril_mask():
    return jnp.tril(jnp.ones((CHUNK, CHUNK), dtype=bool))


def mla_branch(q_lat, kv_lat, k_pe, q_norm_w, w_uq, kv_norm_w, w_ukv, cos, sin):
    B, S, _ = q_lat.shape
    q = (rmsnorm(q_lat, q_norm_w) @ w_uq).reshape(B, S, MLA_HEADS, MLA_NOPE + MLA_ROPE)
    q_nope = q[..., :MLA_NOPE]
    q_pe = apply_rope(q[..., MLA_NOPE:], cos[:, None, :], sin[:, None, :])
    kv = (rmsnorm(kv_lat, kv_norm_w) @ w_ukv).reshape(B, S, MLA_HEADS, MLA_NOPE + MLA_V)
    k_nope, v = kv[..., :MLA_NOPE], kv[..., MLA_NOPE:]
    k_rot = apply_rope(k_pe, cos, sin)
    scale = (MLA_NOPE + MLA_ROPE) ** -0.5
    n_blk = S // Q_BLOCK
    qn_b = q_nope.reshape(B, n_blk, Q_BLOCK, MLA_HEADS, MLA_NOPE).transpose(1, 0, 2, 3, 4)
    qp_b = q_pe.reshape(B, n_blk, Q_BLOCK, MLA_HEADS, MLA_ROPE).transpose(1, 0, 2, 3, 4)
    key_chunk = jnp.arange(S) // CHUNK

    def block(args):
        i, qn, qp = args
        s = (jnp.einsum('bqhd,bkhd->bhqk', qn, k_nope, preferred_element_type=jnp.float32)
             + jnp.einsum('bqhd,bkd->bhqk', qp, k_rot, preferred_element_type=jnp.float32)) * scale
        q_chunk = (i * Q_BLOCK + jnp.arange(Q_BLOCK)) // CHUNK
        mask = key_chunk[None, :] <= q_chunk[:, None]
        p = jax.nn.softmax(jnp.where(mask, s, -jnp.inf), axis=-1).astype(v.dtype)
        return jnp.einsum('bhqk,bkhd->bqhd', p, v)

    o = lax.map(block, (jnp.arange(n_blk), qn_b, qp_b))
    return o.transpose(1, 0, 2, 3, 4).reshape(B, S, MLA_HEADS * MLA_V)


def hgrn2_branch(q_in, f_in, i_in, g_in, lb, norm_w):
    B, S, _ = q_in.shape
    nc = S // CHUNK
    f32 = jnp.float32
    tril = tril_mask()

    def chunks(t, d):
        return t.astype(f32).reshape(B, nc, CHUNK, HG_HEADS, d).transpose(1, 0, 3, 2, 4)

    f = lb + (1.0 - lb) * jax.nn.sigmoid(f_in.astype(f32))
    q = chunks(jax.nn.silu(q_in.astype(f32)) * HG_DK ** -0.5, HG_DK)
    k = chunks(1.0 - f, HG_DK)
    v = chunks(i_in, HG_DV)
    b = jnp.cumsum(chunks(jnp.log(f), HG_DK), axis=3)

    def step(state, inp):
        qc, kc, vc, bc = inp
        b_last = bc[:, :, -1:, :]
        o_inter = jnp.einsum('bhtk,bhkv->bhtv', qc * jnp.exp(bc), state)
        diff = jnp.where(tril[:, :, None], bc[:, :, :, None, :] - bc[:, :, None, :, :], -jnp.inf)
        att = jnp.einsum('bhtk,bhsk,bhtsk->bhts', qc, kc, jnp.exp(diff))
        o = o_inter + jnp.einsum('bhts,bhsv->bhtv', att, vc)
        state = (jnp.exp(b_last[:, :, 0, :, None]) * state
                 + jnp.einsum('bhsk,bhsv->bhkv', kc * jnp.exp(b_last - bc), vc))
        return state, o

    s0 = jnp.zeros((B, HG_HEADS, HG_DK, HG_DV), f32)
    _, o = lax.scan(step, s0, (q, k, v, b))
    o = o.transpose(1, 0, 3, 2, 4).reshape(B, S, HG_HEADS, HG_DV)
    o = o * lax.rsqrt(jnp.mean(o * o, axis=-1, keepdims=True) + EPS)
    o = o.reshape(B, S, HG_WIDTH) * norm_w.astype(f32) * jax.nn.silu(g_in.astype(f32))
    return o.astype(q_in.dtype)


def mamba2_branch(z, xbc, dt_raw, conv_w, conv_b, a_log, dt_bias, d_skip, norm_w):
    B, S, _ = xbc.shape
    nc = S // CHUNK
    hg = SSM_HEADS // SSM_GROUPS
    f32 = jnp.float32
    tril = tril_mask()
    xbc = lax.conv_general_dilated(xbc, conv_w[:, None, :].astype(xbc.dtype), window_strides=(1,),
                                   padding=[(SSM_CONV - 1, 0)], dimension_numbers=('NWC', 'WIO', 'NWC'),
                                   feature_group_count=SSM_CONV_DIM) + conv_b
    xbc = jax.nn.silu(xbc)
    xs, bm, cm = jnp.split(xbc, [SSM_INNER, SSM_INNER + SSM_GROUPS * SSM_STATE], axis=-1)
    xs = xs.astype(f32).reshape(B, nc, CHUNK, SSM_GROUPS, hg, SSM_HEADDIM)
    bm = bm.astype(f32).reshape(B, nc, CHUNK, SSM_GROUPS, SSM_STATE)
    cm = cm.astype(f32).reshape(B, nc, CHUNK, SSM_GROUPS, SSM_STATE)
    dt = jax.nn.softplus(dt_raw.astype(f32) + dt_bias.astype(f32)).reshape(B, nc, CHUNK, SSM_GROUPS, hg)
    a = -jnp.exp(a_log.astype(f32)).reshape(SSM_GROUPS, hg)
    a_cum = jnp.cumsum((dt * a).transpose(0, 3, 4, 1, 2), axis=-1)
    xdt = xs * dt[..., None]
    seg = a_cum[..., :, None] - a_cum[..., None, :]
    decay = jnp.exp(jnp.where(tril, seg, -jnp.inf))
    cb = jnp.einsum('bclgn,bcsgn->bgcls', cm, bm)
    y_diag = jnp.einsum('bgcls,bghcls,bcsghp->bclghp', cb, decay, xdt)
    decay_states = jnp.exp(a_cum[..., -1:] - a_cum)
    states = jnp.einsum('bcsgn,bghcs,bcsghp->cbghpn', bm, decay_states, xdt)
    chunk_decay = jnp.exp(a_cum[..., -1]).transpose(3, 0, 1, 2)

    def step(h, inp):
        st, dec = inp
        return dec[..., None, None] * h + st, h

    _, h_prev = lax.scan(step, jnp.zeros(states.shape[1:], f32), (states, chunk_decay))
    y_off = jnp.einsum('bclgn,cbghpn,bghcl->bclghp', cm, h_prev, jnp.exp(a_cum))
    y = y_diag + y_off + xs * d_skip.astype(f32).reshape(SSM_GROUPS, hg)[:, :, None]
    y = y.reshape(B, S, SSM_INNER) * jax.nn.silu(z.astype(f32))
    y = y.reshape(B, S, SSM_GROUPS, SSM_INNER // SSM_GROUPS)
    y = y * lax.rsqrt(jnp.mean(y * y, axis=-1, keepdims=True) + EPS)
    y = y.reshape(B, S, SSM_INNER) * norm_w.astype(f32)
    return y.astype(z.dtype)


def _fwd_setup_inputs(seed: int = 0) -> dict:
    key = jax.random.key(seed)
    ks = iter(jax.random.split(key, 32))
    L = DEPTH

    def nrm(shape, fan_in):
        return jax.random.normal(next(ks), shape, jnp.float32) * fan_in ** -0.5

    def gain(shape):
        return 1.0 + 0.01 * jax.random.normal(next(ks), shape, jnp.float32)

    x = jax.random.normal(next(ks), (BATCH, SEQ, D_MODEL), jnp.float32)
    ffn1_norm = gain((L, D_MODEL))
    ffn1_wi = nrm((L, D_MODEL, 2 * D_FF), D_MODEL)
    ffn1_wo = nrm((L, D_FF, D_MODEL), D_FF)
    mix_norm = gain((L, D_MODEL))
    w_in = nrm((L, D_MODEL, IN_DIM), D_MODEL)
    mla_q_norm = gain((L, MLA_Q_RANK))
    mla_w_uq = nrm((L, MLA_Q_RANK, MLA_HEADS * (MLA_NOPE + MLA_ROPE)), MLA_Q_RANK)
    mla_kv_norm = gain((L, MLA_KV_RANK))
    mla_w_ukv = nrm((L, MLA_KV_RANK, MLA_HEADS * (MLA_NOPE + MLA_V)), MLA_KV_RANK)
    hgrn_lb_logits = 0.5 * jax.random.normal(next(ks), (L, HG_WIDTH), jnp.float32)
    hgrn_norm = gain((L, HG_WIDTH))
    ssm_conv_w = nrm((L, SSM_CONV, SSM_CONV_DIM), SSM_CONV)
    ssm_conv_b = 0.01 * jax.random.normal(next(ks), (L, SSM_CONV_DIM), jnp.float32)
    ssm_a_log = jnp.log(jax.random.uniform(next(ks), (L, SSM_HEADS), jnp.float32, 1.0, 16.0))
    dt0 = jnp.exp(jax.random.uniform(next(ks), (L, SSM_HEADS), jnp.float32, math.log(1e-3), math.log(1e-1)))
    ssm_dt_bias = dt0 + jnp.log(-jnp.expm1(-dt0))
    ssm_d = gain((L, SSM_HEADS))
    ssm_norm = gain((L, SSM_INNER))
    w_o_mla = nrm((L, MLA_HEADS * MLA_V, D_MODEL), MLA_HEADS * MLA_V)
    w_o_hgrn = nrm((L, HG_WIDTH, D_MODEL), HG_WIDTH)
    w_o_ssm = nrm((L, SSM_INNER, D_MODEL), SSM_INNER)
    w_out = nrm((L, D_MODEL, D_MODEL), D_MODEL)
    ffn2_norm = gain((L, D_MODEL))
    ffn2_wi = nrm((L, D_MODEL, 2 * D_FF), D_MODEL)
    ffn2_wo = nrm((L, D_FF, D_MODEL), D_FF)
    final_norm = gain((D_MODEL,))
    return {"x": x, "ffn1_norm": ffn1_norm, "ffn1_wi": ffn1_wi, "ffn1_wo": ffn1_wo,
            "mix_norm": mix_norm, "w_in": w_in, "mla_q_norm": mla_q_norm, "mla_w_uq": mla_w_uq,
            "mla_kv_norm": mla_kv_norm, "mla_w_ukv": mla_w_ukv, "hgrn_lb_logits": hgrn_lb_logits,
            "hgrn_norm": hgrn_norm, "ssm_conv_w": ssm_conv_w, "ssm_conv_b": ssm_conv_b,
            "ssm_a_log": ssm_a_log, "ssm_dt_bias": ssm_dt_bias, "ssm_d": ssm_d, "ssm_norm": ssm_norm,
            "w_o_mla": w_o_mla, "w_o_hgrn": w_o_hgrn, "w_o_ssm": w_o_ssm, "w_out": w_out,
            "ffn2_norm": ffn2_norm, "ffn2_wi": ffn2_wi, "ffn2_wo": ffn2_wo, "final_norm": final_norm}


def _fwd_reference(x, ffn1_norm, ffn1_wi, ffn1_wo, mix_norm, w_in, mla_q_norm, mla_w_uq, mla_kv_norm,
              mla_w_ukv, hgrn_lb_logits, hgrn_norm, ssm_conv_w, ssm_conv_b, ssm_a_log, ssm_dt_bias,
              ssm_d, ssm_norm, w_o_mla, w_o_hgrn, w_o_ssm, w_out, ffn2_norm, ffn2_wi, ffn2_wo,
              final_norm):
    S = x.shape[1]
    cos, sin = rope_tables(S, MLA_ROPE)
    p = jax.nn.softmax(hgrn_lb_logits.astype(jnp.float32), axis=0)
    lower_bounds = jnp.cumsum(p, axis=0) - p[0:1]
    for l in range(DEPTH):
        x = x + 0.5 * swiglu(rmsnorm(x, ffn1_norm[l]), ffn1_wi[l], ffn1_wo[l])
        h = rmsnorm(x, mix_norm[l])
        (q_lat, kv_lat, k_pe, hq, hf, hi, hgate, z, xbc, dt_raw, gates) = jnp.split(h @ w_in[l], IN_SPLITS, axis=-1)
        y_a = mla_branch(q_lat, kv_lat, k_pe, mla_q_norm[l], mla_w_uq[l], mla_kv_norm[l], mla_w_ukv[l], cos, sin) @ w_o_mla[l]
        y_b = hgrn2_branch(hq, hf, hi, hgate, lower_bounds[l], hgrn_norm[l]) @ w_o_hgrn[l]
        y_c = mamba2_branch(z, xbc, dt_raw, ssm_conv_w[l], ssm_conv_b[l], ssm_a_log[l], ssm_dt_bias[l],
                            ssm_d[l], ssm_norm[l]) @ w_o_ssm[l]
        g_a, g_b, g_c = jnp.split(jax.nn.sigmoid(gates), N_BRANCH, axis=-1)
        x = x + (g_a * y_a + g_b * y_b + g_c * y_c) @ w_out[l]
        x = x + 0.5 * swiglu(rmsnorm(x, ffn2_norm[l]), ffn2_wi[l], ffn2_wo[l])
    return rmsnorm(x, final_norm)


import jax as _jax
import jax.numpy as _jnp

TWIN_FORMAT = 'train_step'
FWD_PARAMS = ['x', 'ffn1_norm', 'ffn1_wi', 'ffn1_wo', 'mix_norm', 'w_in', 'mla_q_norm', 'mla_w_uq', 'mla_kv_norm', 'mla_w_ukv', 'hgrn_lb_logits', 'hgrn_norm', 'ssm_conv_w', 'ssm_conv_b', 'ssm_a_log', 'ssm_dt_bias', 'ssm_d', 'ssm_norm', 'w_o_mla', 'w_o_hgrn', 'w_o_ssm', 'w_out', 'ffn2_norm', 'ffn2_wi', 'ffn2_wo', 'final_norm']
TWIN_WEIGHTS = ['ffn1_norm', 'ffn1_wi', 'ffn1_wo', 'mix_norm', 'w_in', 'mla_q_norm', 'mla_w_uq', 'mla_kv_norm', 'mla_w_ukv', 'hgrn_lb_logits', 'hgrn_norm', 'ssm_conv_w', 'ssm_conv_b', 'ssm_a_log', 'ssm_dt_bias', 'ssm_d', 'ssm_norm', 'w_o_mla', 'w_o_hgrn', 'w_o_ssm', 'w_out', 'ffn2_norm', 'ffn2_wi', 'ffn2_wo', 'final_norm']
TWIN_DIFF_INPUT = 'x'
TWIN_INPUTS = ['x', 'ffn1_norm', 'ffn1_wi', 'ffn1_wo', 'mix_norm', 'w_in', 'mla_q_norm', 'mla_w_uq', 'mla_kv_norm', 'mla_w_ukv', 'hgrn_lb_logits', 'hgrn_norm', 'ssm_conv_w', 'ssm_conv_b', 'ssm_a_log', 'ssm_dt_bias', 'ssm_d', 'ssm_norm', 'w_o_mla', 'w_o_hgrn', 'w_o_ssm', 'w_out', 'ffn2_norm', 'ffn2_wi', 'ffn2_wo', 'final_norm', 'loss_target', 'm_ffn1_norm', 'm_ffn1_wi', 'm_ffn1_wo', 'm_mix_norm', 'm_w_in', 'm_mla_q_norm', 'm_mla_w_uq', 'm_mla_kv_norm', 'm_mla_w_ukv', 'm_hgrn_lb_logits', 'm_hgrn_norm', 'm_ssm_conv_w', 'm_ssm_conv_b', 'm_ssm_a_log', 'm_ssm_dt_bias', 'm_ssm_d', 'm_ssm_norm', 'm_w_o_mla', 'm_w_o_hgrn', 'm_w_o_ssm', 'm_w_out', 'm_ffn2_norm', 'm_ffn2_wi', 'm_ffn2_wo', 'm_final_norm', 'v_ffn1_norm', 'v_ffn1_wi', 'v_ffn1_wo', 'v_mix_norm', 'v_w_in', 'v_mla_q_norm', 'v_mla_w_uq', 'v_mla_kv_norm', 'v_mla_w_ukv', 'v_hgrn_lb_logits', 'v_hgrn_norm', 'v_ssm_conv_w', 'v_ssm_conv_b', 'v_ssm_a_log', 'v_ssm_dt_bias', 'v_ssm_d', 'v_ssm_norm', 'v_w_o_mla', 'v_w_o_hgrn', 'v_w_o_ssm', 'v_w_out', 'v_ffn2_norm', 'v_ffn2_wi', 'v_ffn2_wo', 'v_final_norm']
TWIN_OUTPUTS = ['loss', 'grad_x', 'grad_ffn1_norm', 'grad_ffn1_wi', 'grad_ffn1_wo', 'grad_mix_norm', 'grad_w_in', 'grad_mla_q_norm', 'grad_mla_w_uq', 'grad_mla_kv_norm', 'grad_mla_w_ukv', 'grad_hgrn_lb_logits', 'grad_hgrn_norm', 'grad_ssm_conv_w', 'grad_ssm_conv_b', 'grad_ssm_a_log', 'grad_ssm_dt_bias', 'grad_ssm_d', 'grad_ssm_norm', 'grad_w_o_mla', 'grad_w_o_hgrn', 'grad_w_o_ssm', 'grad_w_out', 'grad_ffn2_norm', 'grad_ffn2_wi', 'grad_ffn2_wo', 'grad_final_norm', 'delta_ffn1_norm', 'delta_ffn1_wi', 'delta_ffn1_wo', 'delta_mix_norm', 'delta_w_in', 'delta_mla_q_norm', 'delta_mla_w_uq', 'delta_mla_kv_norm', 'delta_mla_w_ukv', 'delta_hgrn_lb_logits', 'delta_hgrn_norm', 'delta_ssm_conv_w', 'delta_ssm_conv_b', 'delta_ssm_a_log', 'delta_ssm_dt_bias', 'delta_ssm_d', 'delta_ssm_norm', 'delta_w_o_mla', 'delta_w_o_hgrn', 'delta_w_o_ssm', 'delta_w_out', 'delta_ffn2_norm', 'delta_ffn2_wi', 'delta_ffn2_wo', 'delta_final_norm', 'new_m_ffn1_norm', 'new_m_ffn1_wi', 'new_m_ffn1_wo', 'new_m_mix_norm', 'new_m_w_in', 'new_m_mla_q_norm', 'new_m_mla_w_uq', 'new_m_mla_kv_norm', 'new_m_mla_w_ukv', 'new_m_hgrn_lb_logits', 'new_m_hgrn_norm', 'new_m_ssm_conv_w', 'new_m_ssm_conv_b', 'new_m_ssm_a_log', 'new_m_ssm_dt_bias', 'new_m_ssm_d', 'new_m_ssm_norm', 'new_m_w_o_mla', 'new_m_w_o_hgrn', 'new_m_w_o_ssm', 'new_m_w_out', 'new_m_ffn2_norm', 'new_m_ffn2_wi', 'new_m_ffn2_wo', 'new_m_final_norm', 'new_v_ffn1_norm', 'new_v_ffn1_wi', 'new_v_ffn1_wo', 'new_v_mix_norm', 'new_v_w_in', 'new_v_mla_q_norm', 'new_v_mla_w_uq', 'new_v_mla_kv_norm', 'new_v_mla_w_ukv', 'new_v_hgrn_lb_logits', 'new_v_hgrn_norm', 'new_v_ssm_conv_w', 'new_v_ssm_conv_b', 'new_v_ssm_a_log', 'new_v_ssm_dt_bias', 'new_v_ssm_d', 'new_v_ssm_norm', 'new_v_w_o_mla', 'new_v_w_o_hgrn', 'new_v_w_o_ssm', 'new_v_w_out', 'new_v_ffn2_norm', 'new_v_ffn2_wi', 'new_v_ffn2_wo', 'new_v_final_norm']
TWIN_LEAF_KINDS = {'loss': 'loss', 'grad_x': 'grad_x', 'grad_ffn1_norm': 'grad_w', 'grad_ffn1_wi': 'grad_w', 'grad_ffn1_wo': 'grad_w', 'grad_mix_norm': 'grad_w', 'grad_w_in': 'grad_w', 'grad_mla_q_norm': 'grad_w', 'grad_mla_w_uq': 'grad_w', 'grad_mla_kv_norm': 'grad_w', 'grad_mla_w_ukv': 'grad_w', 'grad_hgrn_lb_logits': 'grad_w', 'grad_hgrn_norm': 'grad_w', 'grad_ssm_conv_w': 'grad_w', 'grad_ssm_conv_b': 'grad_w', 'grad_ssm_a_log': 'grad_w', 'grad_ssm_dt_bias': 'grad_w', 'grad_ssm_d': 'grad_w', 'grad_ssm_norm': 'grad_w', 'grad_w_o_mla': 'grad_w', 'grad_w_o_hgrn': 'grad_w', 'grad_w_o_ssm': 'grad_w', 'grad_w_out': 'grad_w', 'grad_ffn2_norm': 'grad_w', 'grad_ffn2_wi': 'grad_w', 'grad_ffn2_wo': 'grad_w', 'grad_final_norm': 'grad_w', 'delta_ffn1_norm': 'delta_w', 'delta_ffn1_wi': 'delta_w', 'delta_ffn1_wo': 'delta_w', 'delta_mix_norm': 'delta_w', 'delta_w_in': 'delta_w', 'delta_mla_q_norm': 'delta_w', 'delta_mla_w_uq': 'delta_w', 'delta_mla_kv_norm': 'delta_w', 'delta_mla_w_ukv': 'delta_w', 'delta_hgrn_lb_logits': 'delta_w', 'delta_hgrn_norm': 'delta_w', 'delta_ssm_conv_w': 'delta_w', 'delta_ssm_conv_b': 'delta_w', 'delta_ssm_a_log': 'delta_w', 'delta_ssm_dt_bias': 'delta_w', 'delta_ssm_d': 'delta_w', 'delta_ssm_norm': 'delta_w', 'delta_w_o_mla': 'delta_w', 'delta_w_o_hgrn': 'delta_w', 'delta_w_o_ssm': 'delta_w', 'delta_w_out': 'delta_w', 'delta_ffn2_norm': 'delta_w', 'delta_ffn2_wi': 'delta_w', 'delta_ffn2_wo': 'delta_w', 'delta_final_norm': 'delta_w', 'new_m_ffn1_norm': 'new_m', 'new_m_ffn1_wi': 'new_m', 'new_m_ffn1_wo': 'new_m', 'new_m_mix_norm': 'new_m', 'new_m_w_in': 'new_m', 'new_m_mla_q_norm': 'new_m', 'new_m_mla_w_uq': 'new_m', 'new_m_mla_kv_norm': 'new_m', 'new_m_mla_w_ukv': 'new_m', 'new_m_hgrn_lb_logits': 'new_m', 'new_m_hgrn_norm': 'new_m', 'new_m_ssm_conv_w': 'new_m', 'new_m_ssm_conv_b': 'new_m', 'new_m_ssm_a_log': 'new_m', 'new_m_ssm_dt_bias': 'new_m', 'new_m_ssm_d': 'new_m', 'new_m_ssm_norm': 'new_m', 'new_m_w_o_mla': 'new_m', 'new_m_w_o_hgrn': 'new_m', 'new_m_w_o_ssm': 'new_m', 'new_m_w_out': 'new_m', 'new_m_ffn2_norm': 'new_m', 'new_m_ffn2_wi': 'new_m', 'new_m_ffn2_wo': 'new_m', 'new_m_final_norm': 'new_m', 'new_v_ffn1_norm': 'new_v', 'new_v_ffn1_wi': 'new_v', 'new_v_ffn1_wo': 'new_v', 'new_v_mix_norm': 'new_v', 'new_v_w_in': 'new_v', 'new_v_mla_q_norm': 'new_v', 'new_v_mla_w_uq': 'new_v', 'new_v_mla_kv_norm': 'new_v', 'new_v_mla_w_ukv': 'new_v', 'new_v_hgrn_lb_logits': 'new_v', 'new_v_hgrn_norm': 'new_v', 'new_v_ssm_conv_w': 'new_v', 'new_v_ssm_conv_b': 'new_v', 'new_v_ssm_a_log': 'new_v', 'new_v_ssm_dt_bias': 'new_v', 'new_v_ssm_d': 'new_v', 'new_v_ssm_norm': 'new_v', 'new_v_w_o_mla': 'new_v', 'new_v_w_o_hgrn': 'new_v', 'new_v_w_o_ssm': 'new_v', 'new_v_w_out': 'new_v', 'new_v_ffn2_norm': 'new_v', 'new_v_ffn2_wi': 'new_v', 'new_v_ffn2_wo': 'new_v', 'new_v_final_norm': 'new_v'}


def _forward(args):
    return _fwd_reference(*[args[k] for k in FWD_PARAMS])


def _output_shape():
    def fwd():
        inp = _fwd_setup_inputs(0)
        return _fwd_reference(*[inp[k] for k in FWD_PARAMS])
    out = _jax.eval_shape(fwd)
    return out.shape, out.dtype

N_MICROBATCH = 1
ADAM_LR = 0.001
ADAM_B1 = 0.9
ADAM_B2 = 0.999
ADAM_EPS = 1e-08
ADAM_WD = 0.01
ADAM_STEP = 10
PER_EXAMPLE_BATCH_AXIS = {'x': 0, 'loss_target': 0}
SHARED_INPUTS = []
_WEIGHT_DTYPES = {'ffn1_norm': _jnp.float32, 'ffn1_wi': _jnp.float32, 'ffn1_wo': _jnp.float32, 'mix_norm': _jnp.float32, 'w_in': _jnp.float32, 'mla_q_norm': _jnp.float32, 'mla_w_uq': _jnp.float32, 'mla_kv_norm': _jnp.float32, 'mla_w_ukv': _jnp.float32, 'hgrn_lb_logits': _jnp.float32, 'hgrn_norm': _jnp.float32, 'ssm_conv_w': _jnp.float32, 'ssm_conv_b': _jnp.float32, 'ssm_a_log': _jnp.float32, 'ssm_dt_bias': _jnp.float32, 'ssm_d': _jnp.float32, 'ssm_norm': _jnp.float32, 'w_o_mla': _jnp.float32, 'w_o_hgrn': _jnp.float32, 'w_o_ssm': _jnp.float32, 'w_out': _jnp.float32, 'ffn2_norm': _jnp.float32, 'ffn2_wi': _jnp.float32, 'ffn2_wo': _jnp.float32, 'final_norm': _jnp.float32}
MOMENT_SCALE = {'ffn1_norm': 6.166511e-02, 'ffn1_wi': 2.642870e-02, 'ffn1_wo': 4.313175e-02, 'mix_norm': 1.048939e-01, 'w_in': 3.004932e-02, 'mla_q_norm': 1.658438e-02, 'mla_w_uq': 6.826199e-03, 'mla_kv_norm': 2.480205e-02, 'mla_w_ukv': 8.330353e-03, 'hgrn_lb_logits': 2.471908e-03, 'hgrn_norm': 3.488711e-02, 'ssm_conv_w': 3.624966e-02, 'ssm_conv_b': 4.898106e-02, 'ssm_a_log': 1.336602e-01, 'ssm_dt_bias': 8.749593e-02, 'ssm_d': 2.757526e-01, 'ssm_norm': 4.196481e-02, 'w_o_mla': 9.489562e-03, 'w_o_hgrn': 3.502928e-02, 'w_o_ssm': 5.883549e-02, 'w_out': 6.909042e-02, 'ffn2_norm': 4.385100e-02, 'ffn2_wi': 1.875676e-02, 'ffn2_wo': 3.059865e-02, 'final_norm': 3.197417e+01}


def _to_microbatches(a, axis):
    t = _jnp.moveaxis(a, axis, 0)
    t = t.reshape((N_MICROBATCH, t.shape[0] // N_MICROBATCH) + t.shape[1:])
    return _jnp.moveaxis(t, 1, axis + 1)


def setup_inputs(seed: int = 0) -> dict:
    inp = _fwd_setup_inputs(seed)
    key = _jax.random.fold_in(_jax.random.key(seed), 7919)
    shape, _ = _output_shape()
    out = dict(inp)
    out["loss_target"] = _jax.random.normal(_jax.random.fold_in(key, 0), shape, _jnp.float32)
    for i, name in enumerate(TWIN_WEIGHTS):
        w = inp[name].astype(_jnp.float32)
        if MOMENT_SCALE is None:
            s = _jnp.sqrt(_jnp.mean(_jnp.square(w)) + 1e-30)
        else:
            s = MOMENT_SCALE[name]
        km, kv = _jax.random.split(_jax.random.fold_in(key, i + 1))
        out[name] = w
        out["m_" + name] = s * _jax.random.normal(km, w.shape, _jnp.float32)
        out["v_" + name] = (s * s) * _jax.random.uniform(kv, w.shape, _jnp.float32, 0.5, 1.5)
    if N_MICROBATCH > 1:
        for name, axis in PER_EXAMPLE_BATCH_AXIS.items():
            out[name] = _to_microbatches(out[name], axis)
    return {'x': out['x'], 'ffn1_norm': out['ffn1_norm'], 'ffn1_wi': out['ffn1_wi'], 'ffn1_wo': out['ffn1_wo'], 'mix_norm': out['mix_norm'], 'w_in': out['w_in'], 'mla_q_norm': out['mla_q_norm'], 'mla_w_uq': out['mla_w_uq'], 'mla_kv_norm': out['mla_kv_norm'], 'mla_w_ukv': out['mla_w_ukv'], 'hgrn_lb_logits': out['hgrn_lb_logits'], 'hgrn_norm': out['hgrn_norm'], 'ssm_conv_w': out['ssm_conv_w'], 'ssm_conv_b': out['ssm_conv_b'], 'ssm_a_log': out['ssm_a_log'], 'ssm_dt_bias': out['ssm_dt_bias'], 'ssm_d': out['ssm_d'], 'ssm_norm': out['ssm_norm'], 'w_o_mla': out['w_o_mla'], 'w_o_hgrn': out['w_o_hgrn'], 'w_o_ssm': out['w_o_ssm'], 'w_out': out['w_out'], 'ffn2_norm': out['ffn2_norm'], 'ffn2_wi': out['ffn2_wi'], 'ffn2_wo': out['ffn2_wo'], 'final_norm': out['final_norm'], 'loss_target': out['loss_target'], 'm_ffn1_norm': out['m_ffn1_norm'], 'm_ffn1_wi': out['m_ffn1_wi'], 'm_ffn1_wo': out['m_ffn1_wo'], 'm_mix_norm': out['m_mix_norm'], 'm_w_in': out['m_w_in'], 'm_mla_q_norm': out['m_mla_q_norm'], 'm_mla_w_uq': out['m_mla_w_uq'], 'm_mla_kv_norm': out['m_mla_kv_norm'], 'm_mla_w_ukv': out['m_mla_w_ukv'], 'm_hgrn_lb_logits': out['m_hgrn_lb_logits'], 'm_hgrn_norm': out['m_hgrn_norm'], 'm_ssm_conv_w': out['m_ssm_conv_w'], 'm_ssm_conv_b': out['m_ssm_conv_b'], 'm_ssm_a_log': out['m_ssm_a_log'], 'm_ssm_dt_bias': out['m_ssm_dt_bias'], 'm_ssm_d': out['m_ssm_d'], 'm_ssm_norm': out['m_ssm_norm'], 'm_w_o_mla': out['m_w_o_mla'], 'm_w_o_hgrn': out['m_w_o_hgrn'], 'm_w_o_ssm': out['m_w_o_ssm'], 'm_w_out': out['m_w_out'], 'm_ffn2_norm': out['m_ffn2_norm'], 'm_ffn2_wi': out['m_ffn2_wi'], 'm_ffn2_wo': out['m_ffn2_wo'], 'm_final_norm': out['m_final_norm'], 'v_ffn1_norm': out['v_ffn1_norm'], 'v_ffn1_wi': out['v_ffn1_wi'], 'v_ffn1_wo': out['v_ffn1_wo'], 'v_mix_norm': out['v_mix_norm'], 'v_w_in': out['v_w_in'], 'v_mla_q_norm': out['v_mla_q_norm'], 'v_mla_w_uq': out['v_mla_w_uq'], 'v_mla_kv_norm': out['v_mla_kv_norm'], 'v_mla_w_ukv': out['v_mla_w_ukv'], 'v_hgrn_lb_logits': out['v_hgrn_lb_logits'], 'v_hgrn_norm': out['v_hgrn_norm'], 'v_ssm_conv_w': out['v_ssm_conv_w'], 'v_ssm_conv_b': out['v_ssm_conv_b'], 'v_ssm_a_log': out['v_ssm_a_log'], 'v_ssm_dt_bias': out['v_ssm_dt_bias'], 'v_ssm_d': out['v_ssm_d'], 'v_ssm_norm': out['v_ssm_norm'], 'v_w_o_mla': out['v_w_o_mla'], 'v_w_o_hgrn': out['v_w_o_hgrn'], 'v_w_o_ssm': out['v_w_o_ssm'], 'v_w_out': out['v_w_out'], 'v_ffn2_norm': out['v_ffn2_norm'], 'v_ffn2_wi': out['v_ffn2_wi'], 'v_ffn2_wo': out['v_ffn2_wo'], 'v_final_norm': out['v_final_norm']}


def _loss(weights, diff, rest, loss_target):
    with _jax.named_scope("forward"):
        args = {**rest, TWIN_DIFF_INPUT: diff, **{k: w.astype(_WEIGHT_DTYPES[k]) for k, w in weights.items()}}
        y = _forward(args)
    with _jax.named_scope("loss_head"):
        err = _jnp.square(y.astype(_jnp.float32) - loss_target)
        return 0.5 * _jnp.sum(_jnp.mean(err, axis=-1)) if err.ndim else 0.5 * err


def _adamw(w, g, m, v):
    m = ADAM_B1 * m + (1.0 - ADAM_B1) * g
    v = ADAM_B2 * v + (1.0 - ADAM_B2) * _jnp.square(g)
    m_hat = m / (1.0 - ADAM_B1 ** ADAM_STEP)
    v_hat = v / (1.0 - ADAM_B2 ** ADAM_STEP)
    delta = -ADAM_LR * (m_hat / (_jnp.sqrt(v_hat) + ADAM_EPS) + ADAM_WD * w)
    return delta, m, v


def reference(x, ffn1_norm, ffn1_wi, ffn1_wo, mix_norm, w_in, mla_q_norm, mla_w_uq, mla_kv_norm, mla_w_ukv, hgrn_lb_logits, hgrn_norm, ssm_conv_w, ssm_conv_b, ssm_a_log, ssm_dt_bias, ssm_d, ssm_norm, w_o_mla, w_o_hgrn, w_o_ssm, w_out, ffn2_norm, ffn2_wi, ffn2_wo, final_norm, loss_target, m_ffn1_norm, m_ffn1_wi, m_ffn1_wo, m_mix_norm, m_w_in, m_mla_q_norm, m_mla_w_uq, m_mla_kv_norm, m_mla_w_ukv, m_hgrn_lb_logits, m_hgrn_norm, m_ssm_conv_w, m_ssm_conv_b, m_ssm_a_log, m_ssm_dt_bias, m_ssm_d, m_ssm_norm, m_w_o_mla, m_w_o_hgrn, m_w_o_ssm, m_w_out, m_ffn2_norm, m_ffn2_wi, m_ffn2_wo, m_final_norm, v_ffn1_norm, v_ffn1_wi, v_ffn1_wo, v_mix_norm, v_w_in, v_mla_q_norm, v_mla_w_uq, v_mla_kv_norm, v_mla_w_ukv, v_hgrn_lb_logits, v_hgrn_norm, v_ssm_conv_w, v_ssm_conv_b, v_ssm_a_log, v_ssm_dt_bias, v_ssm_d, v_ssm_norm, v_w_o_mla, v_w_o_hgrn, v_w_o_ssm, v_w_out, v_ffn2_norm, v_ffn2_wi, v_ffn2_wo, v_final_norm):
    given = dict(x=x, ffn1_norm=ffn1_norm, ffn1_wi=ffn1_wi, ffn1_wo=ffn1_wo, mix_norm=mix_norm, w_in=w_in, mla_q_norm=mla_q_norm, mla_w_uq=mla_w_uq, mla_kv_norm=mla_kv_norm, mla_w_ukv=mla_w_ukv, hgrn_lb_logits=hgrn_lb_logits, hgrn_norm=hgrn_norm, ssm_conv_w=ssm_conv_w, ssm_conv_b=ssm_conv_b, ssm_a_log=ssm_a_log, ssm_dt_bias=ssm_dt_bias, ssm_d=ssm_d, ssm_norm=ssm_norm, w_o_mla=w_o_mla, w_o_hgrn=w_o_hgrn, w_o_ssm=w_o_ssm, w_out=w_out, ffn2_norm=ffn2_norm, ffn2_wi=ffn2_wi, ffn2_wo=ffn2_wo, final_norm=final_norm, loss_target=loss_target, m_ffn1_norm=m_ffn1_norm, m_ffn1_wi=m_ffn1_wi, m_ffn1_wo=m_ffn1_wo, m_mix_norm=m_mix_norm, m_w_in=m_w_in, m_mla_q_norm=m_mla_q_norm, m_mla_w_uq=m_mla_w_uq, m_mla_kv_norm=m_mla_kv_norm, m_mla_w_ukv=m_mla_w_ukv, m_hgrn_lb_logits=m_hgrn_lb_logits, m_hgrn_norm=m_hgrn_norm, m_ssm_conv_w=m_ssm_conv_w, m_ssm_conv_b=m_ssm_conv_b, m_ssm_a_log=m_ssm_a_log, m_ssm_dt_bias=m_ssm_dt_bias, m_ssm_d=m_ssm_d, m_ssm_norm=m_ssm_norm, m_w_o_mla=m_w_o_mla, m_w_o_hgrn=m_w_o_hgrn, m_w_o_ssm=m_w_o_ssm, m_w_out=m_w_out, m_ffn2_norm=m_ffn2_norm, m_ffn2_wi=m_ffn2_wi, m_ffn2_wo=m_ffn2_wo, m_final_norm=m_final_norm, v_ffn1_norm=v_ffn1_norm, v_ffn1_wi=v_ffn1_wi, v_ffn1_wo=v_ffn1_wo, v_mix_norm=v_mix_norm, v_w_in=v_w_in, v_mla_q_norm=v_mla_q_norm, v_mla_w_uq=v_mla_w_uq, v_mla_kv_norm=v_mla_kv_norm, v_mla_w_ukv=v_mla_w_ukv, v_hgrn_lb_logits=v_hgrn_lb_logits, v_hgrn_norm=v_hgrn_norm, v_ssm_conv_w=v_ssm_conv_w, v_ssm_conv_b=v_ssm_conv_b, v_ssm_a_log=v_ssm_a_log, v_ssm_dt_bias=v_ssm_dt_bias, v_ssm_d=v_ssm_d, v_ssm_norm=v_ssm_norm, v_w_o_mla=v_w_o_mla, v_w_o_hgrn=v_w_o_hgrn, v_w_o_ssm=v_w_o_ssm, v_w_out=v_w_out, v_ffn2_norm=v_ffn2_norm, v_ffn2_wi=v_ffn2_wi, v_ffn2_wo=v_ffn2_wo, v_final_norm=v_final_norm)
    weights = {n: given[n] for n in TWIN_WEIGHTS}
    shared = {n: given[n] for n in SHARED_INPUTS}
    per_example = {n: given[n] for n in ['x']}
    grad_fn = _jax.value_and_grad(_loss, argnums=(0, 1))

    def one_microbatch(ex, loss_target):
        ex = dict(ex)
        diff = ex.pop(TWIN_DIFF_INPUT)
        return grad_fn(weights, diff, {**shared, **ex}, loss_target)

    if N_MICROBATCH == 1:
        loss, (grad_w, grad_x) = one_microbatch(per_example, given["loss_target"])
    else:
        def body(carry, xs):
            loss_sum, grad_sum = carry
            l_k, (gw_k, gx_k) = one_microbatch(xs[0], xs[1])
            with _jax.named_scope("update"):
                return (loss_sum + l_k, _jax.tree.map(_jnp.add, grad_sum, gw_k)), gx_k

        init = (_jnp.zeros((), _jnp.float32), _jax.tree.map(_jnp.zeros_like, weights))
        (loss, grad_w), grad_x = _jax.lax.scan(body, init, (per_example, given["loss_target"]))
    with _jax.named_scope("update"):
        delta_w, new_m, new_v = {}, {}, {}
        for n in TWIN_WEIGHTS:
            delta_w[n], new_m[n], new_v[n] = _adamw(weights[n], grad_w[n], given["m_" + n], given["v_" + n])
    return (loss, grad_x, *[grad_w[n] for n in TWIN_WEIGHTS], *[delta_w[n] for n in TWIN_WEIGHTS],
            *[new_m[n] for n in TWIN_WEIGHTS], *[new_v[n] for n in TWIN_WEIGHTS])
```

```python
import functools

import jax
import jax.numpy as jnp
from jax import lax
from jax.experimental import pallas as pl
from jax.experimental.pallas import tpu as pltpu

F32 = jnp.float32
BF16 = jnp.bfloat16
HI = lax.Precision.HIGHEST
MESH = pl.DeviceIdType.MESH

EPS = 1e-6
CHUNK = 64
N_DEV = 8

MLA_HEADS = 16
MLA_Q_RANK = 512
MLA_KV_RANK = 512
MLA_NOPE = 128
MLA_ROPE = 64
MLA_V = 128
ROPE_THETA = 10000.0
HG_HEADS = 16
HG_DK = 128
HG_WIDTH = HG_HEADS * HG_DK
SSM_HEADDIM = 64
SSM_GROUPS = 8
SSM_STATE = 128
SSM_CONV = 4

ADAM_LR = 0.001
ADAM_B1 = 0.9
ADAM_B2 = 0.999
ADAM_EPS = 1e-08
ADAM_WD = 0.01
ADAM_STEP = 10

LANES = 128
VMEM_LIMIT = 48 * 1024 * 1024
ROW_BLOCK_ELEMS = 128 * 1024
COMM_COLS = 1024
COMM_ROWS = 4096


def _cparams(*sem):
    return pltpu.CompilerParams(dimension_semantics=sem, vmem_limit_bytes=VMEM_LIMIT)


def _tile(dim, pref):
    t = pref
    while t >= LANES:
        if dim % t == 0:
            return t
        t //= 2
    return dim


def _mm_call(a, b, mode, out_dtype, name):
    if mode == "nn":
        (m, k), (k2, n) = a.shape, b.shape
    elif mode == "nt":
        (m, k), (n, k2) = a.shape, b.shape
    else:
        (k, m), (k2, n) = a.shape, b.shape
    assert k == k2, (a.shape, b.shape, mode)
    tm, tn, tk = _tile(m, 1024), _tile(n, 1024), _tile(k, 512)
    nk = k // tk
    if mode == "nn":
        a_spec = pl.BlockSpec((tm, tk), lambda i, j, kk: (i, kk))
        b_spec = pl.BlockSpec((tk, tn), lambda i, j, kk: (kk, j))
        dims = (((1,), (0,)), ((), ()))
    elif mode == "nt":
        a_spec = pl.BlockSpec((tm, tk), lambda i, j, kk: (i, kk))
        b_spec = pl.BlockSpec((tn, tk), lambda i, j, kk: (j, kk))
        dims = (((1,), (1,)), ((), ()))
    else:
        a_spec = pl.BlockSpec((tk, tm), lambda i, j, kk: (kk, i))
        b_spec = pl.BlockSpec((tk, tn), lambda i, j, kk: (kk, j))
        dims = (((0,), (0,)), ((), ()))

    def body(a_ref, b_ref, o_ref, acc_ref):
        kk = pl.program_id(2)

        @pl.when(kk == 0)
        def _():
            acc_ref[...] = jnp.zeros_like(acc_ref)

        acc_ref[...] += lax.dot_general(a_ref[...].astype(BF16), b_ref[...].astype(BF16), dims,
                                        preferred_element_type=F32)

        @pl.when(kk == nk - 1)
        def _():
            o_ref[...] = acc_ref[...].astype(o_ref.dtype)

    return pl.pallas_call(
        body, grid=(m // tm, n // tn, nk), in_specs=[a_spec, b_spec],
        out_specs=pl.BlockSpec((tm, tn), lambda i, j, kk: (i, j)),
        out_shape=jax.ShapeDtypeStruct((m, n), out_dtype),
        scratch_shapes=[pltpu.VMEM((tm, tn), F32)],
        compiler_params=_cparams("parallel", "parallel", "arbitrary"), name=name)(a, b)


@jax.custom_vjp
def mm(a, w):
    return _mm_call(a, w, "nn", F32, "mm_fwd")


def _mm_fwd(a, w):
    return _mm_call(a, w, "nn", F32, "mm_fwd"), (a, w)


def _mm_bwd(res, g):
    a, w = res
    return _mm_call(g, w, "nt", a.dtype, "mm_da"), _mm_call(a, g, "tn", w.dtype, "mm_dw")


mm.defvjp(_mm_fwd, _mm_bwd)


def _row_block(t, widths):
    bt = max(8, ROW_BLOCK_ELEMS // max(widths))
    while t % bt:
        bt //= 2
    return bt


def make_rowwise(fn, name, n_par, group_width=None, shared=(), nondiff=()):
    def specs(args):
        t = max(a.shape[0] for a in args)
        cut = next(a for i, a in enumerate(args) if i >= n_par and i not in shared)
        gw = group_width(cut.shape[1]) if callable(group_width) else group_width
        groups = cut.shape[1] // gw if gw else 1
        ws = [a.shape[1] if i in shared else a.shape[1] // groups for i, a in enumerate(args)]
        ows = out_widths(ws)
        bt = _row_block(t, ws + ows)
        sp = []
        for i, a in enumerate(args):
            col = (lambda g: 0) if i in shared else (lambda g: g)
            if i < n_par:
                sp.append(pl.BlockSpec((1, ws[i]), lambda g, r, col=col: (0, col(g))))
            else:
                sp.append(pl.BlockSpec((bt, ws[i]), lambda g, r, col=col: (r, col(g))))
        return t, bt, groups, ows, sp

    def out_widths(ws):
        blocks = [jax.ShapeDtypeStruct((1 if i < n_par else 8, w), F32) for i, w in enumerate(ws)]
        return [o.shape[1] for o in jax.eval_shape(fn, *blocks)]

    def fwd_call(*args):
        t, bt, groups, ows, in_specs = specs(args)
        n_in = len(args)

        def body(*refs):
            outs = fn(*[r[...] for r in refs[:n_in]])
            for r, o in zip(refs[n_in:], outs):
                r[...] = o

        return pl.pallas_call(
            body, grid=(groups, t // bt), in_specs=in_specs,
            out_specs=[pl.BlockSpec((bt, w), lambda g, r: (r, g)) for w in ows],
            out_shape=[jax.ShapeDtypeStruct((t, w * groups), F32) for w in ows],
            compiler_params=_cparams("parallel", "parallel"), name=name + "_fwd")(*args)

    def bwd_call(args, gs):
        t, bt, groups, ows, in_specs = specs(args)
        n_in, n_out = len(args), len(gs)
        diff = [i for i in range(n_in) if i not in nondiff]
        g_specs = [pl.BlockSpec((bt, w), lambda g, r: (r, g)) for w in ows]
        o_specs, o_shapes = [], []
        for i in diff:
            o_specs.append(in_specs[i])
            o_shapes.append(jax.ShapeDtypeStruct(args[i].shape, F32))

        def body(*refs):
            r_idx = pl.program_id(1)
            vals = [r[...] for r in refs[:n_in]]
            cts = tuple(r[...] for r in refs[n_in:n_in + n_out])

            def f_diff(*dv):
                full = list(vals)
                for i, v in zip(diff, dv):
                    full[i] = v
                return tuple(fn(*full))

            _, vjp = jax.vjp(f_diff, *[vals[i] for i in diff])
            grads = vjp(cts)
            for i, g_val, ref in zip(diff, grads, refs[n_in + n_out:]):
                if i < n_par:
                    @pl.when(r_idx == 0)
                    def _(ref=ref, g_val=g_val):
                        ref[...] = g_val

                    @pl.when(r_idx != 0)
                    def _(ref=ref, g_val=g_val):
                        ref[...] += g_val
                else:
                    ref[...] = g_val

        outs = pl.pallas_call(
            body, grid=(groups, t // bt), in_specs=in_specs + g_specs, out_specs=o_specs, out_shape=o_shapes,
            compiler_params=_cparams("parallel", "arbitrary"), name=name + "_bwd")(*args, *gs)
        full = [jnp.zeros_like(a) for a in args]
        for i, o in zip(diff, outs):
            full[i] = o
        return tuple(full)

    @jax.custom_vjp
    def op(*args):
        return tuple(fwd_call(*args))

    def op_fwd(*args):
        return tuple(fwd_call(*args)), args

    def op_bwd(args, gs):
        return bwd_call(args, gs)

    op.defvjp(op_fwd, op_bwd)
    return op


def _silu(x):
    return x * jax.nn.sigmoid(x)


def _rmsnorm_fn(w, x):
    return (x * lax.rsqrt(jnp.mean(x * x, axis=-1, keepdims=True) + EPS) * w,)


def _swiglu_fn(g, u):
    return (_silu(g) * u,)


def _silu_fn(x):
    return (_silu(x),)


def _rope_fn(x, cos, sin):
    i = lax.broadcasted_iota(jnp.int32, (LANES, LANES), 0)
    j = lax.broadcasted_iota(jnp.int32, (LANES, LANES), 1)
    half = MLA_ROPE // 2
    first = (j % MLA_ROPE) < half
    p = jnp.where(first & (i == j + half), -1.0, 0.0) + jnp.where((~first) & (i == j - half), 1.0, 0.0)
    return (x * cos + jnp.dot(x, p.astype(F32), precision=HI) * sin,)


def _hgrn_out_fn(w, o, g):
    return (o * lax.rsqrt(jnp.mean(o * o, axis=-1, keepdims=True) + EPS) * w * _silu(g),)


def _softplus(x):
    return jnp.maximum(x, 0.0) + jnp.log(1.0 + jnp.exp(-jnp.abs(x)))


def _dt_expand_fn(bias, dt_raw):
    nh = dt_raw.shape[1]
    h = lax.broadcasted_iota(jnp.int32, (nh, nh * SSM_HEADDIM), 0)
    c = lax.broadcasted_iota(jnp.int32, (nh, nh * SSM_HEADDIM), 1)
    e = (c // SSM_HEADDIM == h).astype(F32)
    return (jnp.dot(_softplus(dt_raw + bias), e, precision=HI),)


def _ssm_norm_fn(w, y, z):
    y = y * _silu(z)
    return (y * lax.rsqrt(jnp.mean(y * y, axis=-1, keepdims=True) + EPS) * w,)


def _merge_fn(ya, yb, yc, ga, gb, gc):
    return (jax.nn.sigmoid(ga) * ya + jax.nn.sigmoid(gb) * yb + jax.nn.sigmoid(gc) * yc,)


def _loss_fn(w, x, tgt):
    y = x * lax.rsqrt(jnp.mean(x * x, axis=-1, keepdims=True) + EPS) * w
    err = y - tgt
    return (0.5 * jnp.mean(err * err, axis=-1, keepdims=True),)


rmsnorm = make_rowwise(_rmsnorm_fn, "rmsnorm", 1)
swiglu = make_rowwise(_swiglu_fn, "swiglu", 0, group_width=512)
silu_op = make_rowwise(_silu_fn, "silu", 0, group_width=512)
rope = make_rowwise(_rope_fn, "rope", 0, group_width=LANES, shared=(1, 2), nondiff=(1, 2))
hgrn_out = make_rowwise(_hgrn_out_fn, "hgrn_out", 1, group_width=HG_DK)
dt_expand = make_rowwise(_dt_expand_fn, "dt_expand", 1)
ssm_norm = make_rowwise(_ssm_norm_fn, "ssm_norm", 1, group_width=lambda w: w // SSM_GROUPS)
merge = make_rowwise(_merge_fn, "merge", 0, group_width=512)
loss_rows = make_rowwise(_loss_fn, "loss", 1, nondiff=(2,))


HALO = 8


def _conv_blocks(t, c):
    cw = _tile(c, 512)
    bt = _row_block(t, [cw])
    return cw, bt


def _conv_fwd_call(w, b, x):
    t, c = x.shape
    cw, bt = _conv_blocks(t, c)
    hb = bt // HALO

    def body(w_ref, b_ref, x_ref, prev_ref, y_ref):
        r = pl.program_id(1)
        prev = jnp.where(r == 0, 0.0, prev_ref[...])
        xx = jnp.concatenate([prev, x_ref[...]], axis=0)
        acc = jnp.zeros((bt, cw), F32) + b_ref[...]
        for k in range(SSM_CONV):
            sh = SSM_CONV - 1 - k
            xs = xx if sh == 0 else pltpu.roll(xx, sh, axis=0)
            acc = acc + w_ref[k:k + 1, :] * xs[HALO:, :]
        y_ref[...] = acc

    return pl.pallas_call(
        body, grid=(c // cw, t // bt),
        in_specs=[pl.BlockSpec((SSM_CONV, cw), lambda g, r: (0, g)), pl.BlockSpec((1, cw), lambda g, r: (0, g)),
                  pl.BlockSpec((bt, cw), lambda g, r: (r, g)),
                  pl.BlockSpec((HALO, cw), lambda g, r: (jnp.maximum(r * hb - 1, 0), g))],
        out_specs=pl.BlockSpec((bt, cw), lambda g, r: (r, g)),
        out_shape=jax.ShapeDtypeStruct((t, c), F32),
        compiler_params=_cparams("parallel", "parallel"), name="conv_fwd")(w, b, x, x)


def _conv_bwd_call(w, x, dy):
    t, c = x.shape
    cw, bt = _conv_blocks(t, c)
    hb = bt // HALO
    nr = t // bt

    def body(w_ref, x_ref, prev_ref, dy_ref, next_ref, dx_ref, dw_ref, db_ref):
        r = pl.program_id(1)
        prev = jnp.where(r == 0, 0.0, prev_ref[...])
        nxt = jnp.where(r == nr - 1, 0.0, next_ref[...])
        xx = jnp.concatenate([prev, x_ref[...]], axis=0)
        dd = jnp.concatenate([dy_ref[...], nxt], axis=0)
        dy_val = dy_ref[...]
        dx = jnp.zeros((bt, cw), F32)
        dws = []
        for k in range(SSM_CONV):
            sh = SSM_CONV - 1 - k
            xs = xx if sh == 0 else pltpu.roll(xx, sh, axis=0)
            ds = dd if sh == 0 else pltpu.roll(dd, bt + HALO - sh, axis=0)
            dx = dx + w_ref[k:k + 1, :] * ds[:bt, :]
            dws.append(jnp.sum(dy_val * xs[HALO:, :], axis=0, keepdims=True))
        dx_ref[...] = dx
        dw = jnp.concatenate(dws, axis=0)
        db = jnp.sum(dy_val, axis=0, keepdims=True)

        @pl.when(r == 0)
        def _():
            dw_ref[...] = dw
            db_ref[...] = db

        @pl.when(r != 0)
        def _():
            dw_ref[...] += dw
            db_ref[...] += db

    return pl.pallas_call(
        body, grid=(c // cw, nr),
        in_specs=[pl.BlockSpec((SSM_CONV, cw), lambda g, r: (0, g)),
                  pl.BlockSpec((bt, cw), lambda g, r: (r, g)),
                  pl.BlockSpec((HALO, cw), lambda g, r: (jnp.maximum(r * hb - 1, 0), g)),
                  pl.BlockSpec((bt, cw), lambda g, r: (r, g)),
                  pl.BlockSpec((HALO, cw), lambda g, r: (jnp.minimum((r + 1) * hb, nr * hb - 1), g))],
        out_specs=[pl.BlockSpec((bt, cw), lambda g, r: (r, g)),
                   pl.BlockSpec((SSM_CONV, cw), lambda g, r: (0, g)), pl.BlockSpec((1, cw), lambda g, r: (0, g))],
        out_shape=[jax.ShapeDtypeStruct((t, c), F32), jax.ShapeDtypeStruct((SSM_CONV, c), F32),
                   jax.ShapeDtypeStruct((1, c), F32)],
        compiler_params=_cparams("parallel", "arbitrary"), name="conv_bwd")(w, x, x, dy, dy)


@jax.custom_vjp
def conv(w, b, x):
    return _conv_fwd_call(w, b, x)


def _conv_vjp_fwd(w, b, x):
    return _conv_fwd_call(w, b, x), (w, x)


def _conv_vjp_bwd(res, dy):
    w, x = res
    dx, dw, db = _conv_bwd_call(w, x, dy)
    return dw, db, dx


conv.defvjp(_conv_vjp_fwd, _conv_vjp_bwd)


ATT_DQK = 2 * LANES
ATT_SCALE = (MLA_NOPE + MLA_ROPE) ** -0.5
NEG = -1e30


def _att_tile(t):
    return min(512, max(CHUNK, t // 4))


def _att_mask(s, tq):
    row = lax.broadcasted_iota(jnp.int32, (tq, tq), 0) // CHUNK
    col = lax.broadcasted_iota(jnp.int32, (tq, tq), 1) // CHUNK
    return jnp.where(col <= row, s, NEG)


def _att_fwd_call(q, k, v):
    t = q.shape[0]
    h = q.shape[1] // ATT_DQK
    dv = v.shape[1] // h
    tq = _att_tile(t)
    nq = t // tq
    dims_nt = (((1,), (1,)), ((), ()))

    def body(q_ref, k_ref, v_ref, o_ref, lse_ref, m_scr, l_scr, acc_scr):
        i, j = pl.program_id(1), pl.program_id(2)

        @pl.when(j == 0)
        def _():
            m_scr[...] = jnp.full_like(m_scr, NEG)
            l_scr[...] = jnp.zeros_like(l_scr)
            acc_scr[...] = jnp.zeros_like(acc_scr)

        @pl.when(j <= i)
        def _():
            s = lax.dot_general(q_ref[...], k_ref[...], dims_nt, preferred_element_type=F32) * ATT_SCALE
            s = jnp.where(j == i, _att_mask(s, tq), s)
            m_new = jnp.maximum(m_scr[...], jnp.max(s, axis=-1, keepdims=True))
            alpha = jnp.exp(m_scr[...] - m_new)
            p = jnp.exp(s - m_new)
            l_scr[...] = alpha * l_scr[...] + jnp.sum(p, axis=-1, keepdims=True)
            acc_scr[...] = alpha * acc_scr[...] + jnp.dot(p.astype(BF16), v_ref[...], preferred_element_type=F32)
            m_scr[...] = m_new

        @pl.when(j == i)
        def _():
            o_ref[...] = acc_scr[...] / l_scr[...]
            lse_ref[0] = m_scr[...] + jnp.log(l_scr[...])

    return pl.pallas_call(
        body, grid=(h, nq, nq),
        in_specs=[pl.BlockSpec((tq, ATT_DQK), lambda hh, i, j: (i, hh)),
                  pl.BlockSpec((tq, ATT_DQK), lambda hh, i, j: (jnp.minimum(j, i), hh)),
                  pl.BlockSpec((tq, dv), lambda hh, i, j: (jnp.minimum(j, i), hh))],
        out_specs=[pl.BlockSpec((tq, dv), lambda hh, i, j: (i, hh)),
                   pl.BlockSpec((1, tq, 1), lambda hh, i, j: (hh, i, 0))],
        out_shape=[jax.ShapeDtypeStruct((t, h * dv), F32), jax.ShapeDtypeStruct((h, t, 1), F32)],
        scratch_shapes=[pltpu.VMEM((tq, 1), F32), pltpu.VMEM((tq, 1), F32), pltpu.VMEM((tq, dv), F32)],
        compiler_params=_cparams("parallel", "parallel", "arbitrary"), name="att_fwd")(q, k, v)


def _att_dq_call(q, k, v, o, lse, do):
    t = q.shape[0]
    h = q.shape[1] // ATT_DQK
    dv = v.shape[1] // h
    tq = _att_tile(t)
    nq = t // tq
    dims_nt = (((1,), (1,)), ((), ()))

    def body(q_ref, k_ref, v_ref, o_ref, lse_ref, do_ref, dq_ref, delta_ref, acc_scr, d_scr):
        i, j = pl.program_id(1), pl.program_id(2)

        @pl.when(j == 0)
        def _():
            acc_scr[...] = jnp.zeros_like(acc_scr)
            d_scr[...] = jnp.sum(do_ref[...] * o_ref[...], axis=-1, keepdims=True)

        @pl.when(j <= i)
        def _():
            s = lax.dot_general(q_ref[...], k_ref[...], dims_nt, preferred_element_type=F32) * ATT_SCALE
            s = jnp.where(j == i, _att_mask(s, tq), s)
            p = jnp.exp(s - lse_ref[0])
            dp = lax.dot_general(do_ref[...].astype(BF16), v_ref[...], dims_nt, preferred_element_type=F32)
            ds = p * (dp - d_scr[...]) * ATT_SCALE
            acc_scr[...] += jnp.dot(ds.astype(BF16), k_ref[...], preferred_element_type=F32)

        @pl.when(j == i)
        def _():
            dq_ref[...] = acc_scr[...].astype(dq_ref.dtype)
            delta_ref[0] = d_scr[...]

    return pl.pallas_call(
        body, grid=(h, nq, nq),
        in_specs=[pl.BlockSpec((tq, ATT_DQK), lambda hh, i, j: (i, hh)),
                  pl.BlockSpec((tq, ATT_DQK), lambda hh, i, j: (jnp.minimum(j, i), hh)),
                  pl.BlockSpec((tq, dv), lambda hh, i, j: (jnp.minimum(j, i), hh)),
                  pl.BlockSpec((tq, dv), lambda hh, i, j: (i, hh)),
                  pl.BlockSpec((1, tq, 1), lambda hh, i, j: (hh, i, 0)),
                  pl.BlockSpec((tq, dv), lambda hh, i, j: (i, hh))],
        out_specs=[pl.BlockSpec((tq, ATT_DQK), lambda hh, i, j: (i, hh)),
                   pl.BlockSpec((1, tq, 1), lambda hh, i, j: (hh, i, 0))],
        out_shape=[jax.ShapeDtypeStruct(q.shape, q.dtype), jax.ShapeDtypeStruct((h, t, 1), F32)],
        scratch_shapes=[pltpu.VMEM((tq, ATT_DQK), F32), pltpu.VMEM((tq, 1), F32)],
        compiler_params=_cparams("parallel", "parallel", "arbitrary"), name="att_dq")(q, k, v, o, lse, do)


def _att_dkv_call(q, k, v, lse, delta, do):
    t = q.shape[0]
    h = q.shape[1] // ATT_DQK
    dv = v.shape[1] // h
    tq = _att_tile(t)
    nq = t // tq
    dims_nt = (((1,), (1,)), ((), ()))
    dims_tn = (((0,), (0,)), ((), ()))

    def body(q_ref, k_ref, v_ref, lse_ref, delta_ref, do_ref, dk_ref, dv_ref, dk_scr, dv_scr):
        j, i = pl.program_id(1), pl.program_id(2)

        @pl.when(i == 0)
        def _():
            dk_scr[...] = jnp.zeros_like(dk_scr)
            dv_scr[...] = jnp.zeros_like(dv_scr)

        @pl.when(i >= j)
        def _():
            s = lax.dot_general(q_ref[...], k_ref[...], dims_nt, preferred_element_type=F32) * ATT_SCALE
            s = jnp.where(j == i, _att_mask(s, tq), s)
            p = jnp.exp(s - lse_ref[0])
            do_b = do_ref[...].astype(BF16)
            dv_scr[...] += lax.dot_general(p.astype(BF16), do_b, dims_tn, preferred_element_type=F32)
            dp = lax.dot_general(do_b, v_ref[...], dims_nt, preferred_element_type=F32)
            ds = p * (dp - delta_ref[0]) * ATT_SCALE
            dk_scr[...] += lax.dot_general(ds.astype(BF16), q_ref[...], dims_tn, preferred_element_type=F32)

        @pl.when(i == nq - 1)
        def _():
            dk_ref[...] = dk_scr[...].astype(dk_ref.dtype)
            dv_ref[...] = dv_scr[...].astype(dv_ref.dtype)

    return pl.pallas_call(
        body, grid=(h, nq, nq),
        in_specs=[pl.BlockSpec((tq, ATT_DQK), lambda hh, j, i: (jnp.maximum(i, j), hh)),
                  pl.BlockSpec((tq, ATT_DQK), lambda hh, j, i: (j, hh)),
                  pl.BlockSpec((tq, dv), lambda hh, j, i: (j, hh)),
                  pl.BlockSpec((1, tq, 1), lambda hh, j, i: (hh, jnp.maximum(i, j), 0)),
                  pl.BlockSpec((1, tq, 1), lambda hh, j, i: (hh, jnp.maximum(i, j), 0)),
                  pl.BlockSpec((tq, dv), lambda hh, j, i: (jnp.maximum(i, j), hh))],
        out_specs=[pl.BlockSpec((tq, ATT_DQK), lambda hh, j, i: (j, hh)),
                   pl.BlockSpec((tq, dv), lambda hh, j, i: (j, hh))],
        out_shape=[jax.ShapeDtypeStruct(k.shape, k.dtype), jax.ShapeDtypeStruct(v.shape, v.dtype)],
        scratch_shapes=[pltpu.VMEM((tq, ATT_DQK), F32), pltpu.VMEM((tq, dv), F32)],
        compiler_params=_cparams("parallel", "parallel", "arbitrary"), name="att_dkv")(q, k, v, lse, delta, do)


@jax.custom_vjp
def attention(q, k, v):
    return _att_fwd_call(q, k, v)[0]


def _att_vjp_fwd(q, k, v):
    o, lse = _att_fwd_call(q, k, v)
    return o, (q, k, v, o, lse)


def _att_vjp_bwd(res, do):
    q, k, v, o, lse = res
    dq, delta = _att_dq_call(q, k, v, o, lse, do)
    dk, dv = _att_dkv_call(q, k, v, lse, delta, do)
    return dq, dk, dv


attention.defvjp(_att_vjp_fwd, _att_vjp_bwd)


MID = CHUNK // 2 - 1


def _tril(n):
    r = lax.broadcasted_iota(jnp.int32, (n, n), 0)
    c = lax.broadcasted_iota(jnp.int32, (n, n), 1)
    return c <= r


def _bdot(a, b, dims):
    return lax.dot_general(a.astype(BF16), b.astype(BF16), dims, preferred_element_type=F32)


NN = (((1,), (0,)), ((), ()))
NT = (((1,), (1,)), ((), ()))
TN = (((0,), (0,)), ((), ()))


def _hgrn_chunk(state, q_in, f_in, i_in, lb):
    tril = _tril(CHUNK)
    f = lb + (1.0 - lb) * jax.nn.sigmoid(f_in)
    logf = jnp.log(f)
    b = jnp.dot(tril.astype(F32), logf, precision=HI)
    q = _silu(q_in) * HG_DK ** -0.5
    k = 1.0 - f
    b_mid = b[MID:MID + 1, :]
    att = _bdot(q * jnp.exp(b - b_mid), k * jnp.exp(b_mid - b), NT)
    att = jnp.where(tril, att, 0.0)
    o = _bdot(q * jnp.exp(b), state, NN) + _bdot(att, i_in, NN)
    b_last = b[CHUNK - 1:CHUNK, :]
    decay = jnp.exp(lax.dot_general(logf, jnp.ones((CHUNK, i_in.shape[1]), F32), TN, precision=HI))
    new_state = decay * state + _bdot(k * jnp.exp(b_last - b), i_in, TN)
    return o, new_state


def _hgrn_fwd_call(q, f, i, lb):
    t = q.shape[0]
    nc = t // CHUNK
    blk = pl.BlockSpec((CHUNK, HG_DK), lambda h, c: (c, h))

    def body(q_ref, f_ref, i_ref, lb_ref, o_ref, s_ref, s_scr):
        @pl.when(pl.program_id(1) == 0)
        def _():
            s_scr[...] = jnp.zeros_like(s_scr)

        s_ref[0, 0] = s_scr[...]
        o, ns = _hgrn_chunk(s_scr[...], q_ref[...], f_ref[...], i_ref[...], lb_ref[...])
        o_ref[...] = o
        s_scr[...] = ns

    return pl.pallas_call(
        body, grid=(HG_HEADS, nc),
        in_specs=[blk, blk, blk, pl.BlockSpec((1, HG_DK), lambda h, c: (0, h))],
        out_specs=[blk, pl.BlockSpec((1, 1, HG_DK, HG_DK), lambda h, c: (h, c, 0, 0))],
        out_shape=[jax.ShapeDtypeStruct((t, HG_WIDTH), F32), jax.ShapeDtypeStruct((HG_HEADS, nc, HG_DK, HG_DK), F32)],
        scratch_shapes=[pltpu.VMEM((HG_DK, HG_DK), F32)],
        compiler_params=_cparams("parallel", "arbitrary"), name="hgrn_fwd")(q, f, i, lb)


def _hgrn_bwd_call(q, f, i, lb, states, do):
    t = q.shape[0]
    nc = t // CHUNK
    blk = pl.BlockSpec((CHUNK, HG_DK), lambda h, c: (nc - 1 - c, h))
    row = pl.BlockSpec((1, HG_DK), lambda h, c: (0, h))

    def body(q_ref, f_ref, i_ref, lb_ref, s_ref, do_ref, dq_ref, df_ref, di_ref, dlb_ref, ds_scr):
        c = pl.program_id(1)

        @pl.when(c == 0)
        def _():
            ds_scr[...] = jnp.zeros_like(ds_scr)

        _, vjp = jax.vjp(_hgrn_chunk, s_ref[0, 0], q_ref[...], f_ref[...], i_ref[...], lb_ref[...])
        ds, dq, df, di, dlb = vjp((do_ref[...], ds_scr[...]))
        ds_scr[...] = ds
        dq_ref[...] = dq
        df_ref[...] = df
        di_ref[...] = di

        @pl.when(c == 0)
        def _():
            dlb_ref[...] = dlb

        @pl.when(c != 0)
        def _():
            dlb_ref[...] += dlb

    return pl.pallas_call(
        body, grid=(HG_HEADS, nc),
        in_specs=[blk, blk, blk, row, pl.BlockSpec((1, 1, HG_DK, HG_DK), lambda h, c: (h, nc - 1 - c, 0, 0)), blk],
        out_specs=[blk, blk, blk, row],
        out_shape=[jax.ShapeDtypeStruct((t, HG_WIDTH), F32)] * 3 + [jax.ShapeDtypeStruct((1, HG_WIDTH), F32)],
        scratch_shapes=[pltpu.VMEM((HG_DK, HG_DK), F32)],
        compiler_params=_cparams("parallel", "arbitrary"), name="hgrn_bwd")(q, f, i, lb, states, do)


@jax.custom_vjp
def hgrn_scan(q, f, i, lb):
    return _hgrn_fwd_call(q, f, i, lb)[0]


def _hgrn_vjp_fwd(q, f, i, lb):
    o, states = _hgrn_fwd_call(q, f, i, lb)
    return o, (q, f, i, lb, states)


def _hgrn_vjp_bwd(res, do):
    return tuple(_hgrn_bwd_call(*res, do))


hgrn_scan.defvjp(_hgrn_vjp_fwd, _hgrn_vjp_bwd)


def _ssd_chunk(state, xs, bm, cm, dtx, alog, dskip):
    assert CHUNK == SSM_HEADDIM
    gw = xs.shape[1]
    tril = _tril(CHUNK)
    trilf = tril.astype(F32)
    da = dtx * (-jnp.exp(alog))
    a = jnp.dot(trilf, da, precision=HI)
    a_rows = lax.dot_general(da, trilf, (((0,), (1,)), ((), ())), precision=HI)
    xdt = xs * dtx
    cb = _bdot(cm, bm, NT)
    lane_head = lax.broadcasted_iota(jnp.int32, (1, gw), 1) // SSM_HEADDIM
    pick_row = lax.broadcasted_iota(jnp.int32, (gw, CHUNK), 0)
    y_diag = jnp.zeros_like(xs)
    for h in range(gw // SSM_HEADDIM):
        a_col = jnp.dot(a, (pick_row == h * SSM_HEADDIM).astype(F32), precision=HI)
        seg = a_col - a_rows[h * SSM_HEADDIM:(h + 1) * SSM_HEADDIM, :]
        decay_ls = jnp.exp(jnp.where(tril, seg, NEG))
        y_diag = y_diag + _bdot(cb * decay_ls, jnp.where(lane_head == h, xdt, 0.0), NN)
    y_off = jnp.exp(a) * _bdot(cm, state, NT)
    y = y_diag + y_off + xs * dskip
    a_last = a[CHUNK - 1:CHUNK, :]
    decay = jnp.exp(lax.dot_general(da, jnp.ones((CHUNK, bm.shape[1]), F32), TN, precision=HI))
    new_state = decay * state + _bdot(jnp.exp(a_last - a) * xdt, bm, TN)
    return y, new_state


def _ssd_specs(inner, nc, rev):
    gw = inner // SSM_GROUPS
    nb = inner // SSM_STATE
    ci = (lambda c: nc - 1 - c) if rev else (lambda c: c)
    xs = pl.BlockSpec((CHUNK, gw), lambda g, c: (ci(c), g))
    bm = pl.BlockSpec((CHUNK, SSM_STATE), lambda g, c: (ci(c), nb + g))
    cm = pl.BlockSpec((CHUNK, SSM_STATE), lambda g, c: (ci(c), nb + SSM_GROUPS + g))
    row = pl.BlockSpec((1, gw), lambda g, c: (0, g))
    st = pl.BlockSpec((1, 1, gw, SSM_STATE), lambda g, c: (g, ci(c), 0, 0))
    return gw, xs, bm, cm, row, st


def _ssd_fwd_call(xbc, dtx, alog, dskip):
    t, inner = dtx.shape
    nc = t // CHUNK
    gw, xs_s, bm_s, cm_s, row, st = _ssd_specs(inner, nc, False)

    def body(xs_ref, bm_ref, cm_ref, dt_ref, a_ref, d_ref, y_ref, s_ref, s_scr):
        @pl.when(pl.program_id(1) == 0)
        def _():
            s_scr[...] = jnp.zeros_like(s_scr)

        s_ref[0, 0] = s_scr[...]
        y, ns = _ssd_chunk(s_scr[...], xs_ref[...], bm_ref[...], cm_ref[...], dt_ref[...], a_ref[...], d_ref[...])
        y_ref[...] = y
        s_scr[...] = ns

    return pl.pallas_call(
        body, grid=(SSM_GROUPS, nc), in_specs=[xs_s, bm_s, cm_s, xs_s, row, row],
        out_specs=[xs_s, st],
        out_shape=[jax.ShapeDtypeStruct((t, inner), F32), jax.ShapeDtypeStruct((SSM_GROUPS, nc, gw, SSM_STATE), F32)],
        scratch_shapes=[pltpu.VMEM((gw, SSM_STATE), F32)],
        compiler_params=_cparams("parallel", "arbitrary"), name="ssd_fwd")(xbc, xbc, xbc, dtx, alog, dskip)


def _ssd_bwd_call(xbc, dtx, alog, dskip, states, dy):
    t, inner = dtx.shape
    nc = t // CHUNK
    gw, xs_s, bm_s, cm_s, row, st = _ssd_specs(inner, nc, True)
    gn = pl.BlockSpec((CHUNK, SSM_STATE), lambda g, c: (nc - 1 - c, g))
    gn_shape = jax.ShapeDtypeStruct((t, SSM_GROUPS * SSM_STATE), F32)

    def body(xs_ref, bm_ref, cm_ref, dt_ref, a_ref, d_ref, s_ref, dy_ref,
             dxs_ref, dbm_ref, dcm_ref, ddt_ref, da_ref, dd_ref, ds_scr):
        c = pl.program_id(1)

        @pl.when(c == 0)
        def _():
            ds_scr[...] = jnp.zeros_like(ds_scr)

        _, vjp = jax.vjp(_ssd_chunk, s_ref[0, 0], xs_ref[...], bm_ref[...], cm_ref[...], dt_ref[...],
                         a_ref[...], d_ref[...])
        ds, dxs, dbm, dcm, ddt, da, dd = vjp((dy_ref[...], ds_scr[...]))
        ds_scr[...] = ds
        dxs_ref[...] = dxs
        dbm_ref[...] = dbm
        dcm_ref[...] = dcm
        ddt_ref[...] = ddt

        @pl.when(c == 0)
        def _():
            da_ref[...] = da
            dd_ref[...] = dd

        @pl.when(c != 0)
        def _():
            da_ref[...] += da
            dd_ref[...] += dd

    big = jax.ShapeDtypeStruct((t, inner), F32)
    small = jax.ShapeDtypeStruct((1, inner), F32)
    return pl.pallas_call(
        body, grid=(SSM_GROUPS, nc), in_specs=[xs_s, bm_s, cm_s, xs_s, row, row, st, xs_s],
        out_specs=[xs_s, gn, gn, xs_s, row, row],
        out_shape=[big, gn_shape, gn_shape, big, small, small],
        scratch_shapes=[pltpu.VMEM((gw, SSM_STATE), F32)],
        compiler_params=_cparams("parallel", "arbitrary"), name="ssd_bwd")(xbc, xbc, xbc, dtx, alog, dskip, states, dy)


@jax.custom_vjp
def ssd_scan(xbc, dtx, alog, dskip):
    return _ssd_fwd_call(xbc, dtx, alog, dskip)[0]


def _ssd_vjp_fwd(xbc, dtx, alog, dskip):
    y, states = _ssd_fwd_call(xbc, dtx, alog, dskip)
    return y, (xbc, dtx, alog, dskip, states)


def _ssd_vjp_bwd(res, dy):
    dxs, dbm, dcm, ddt, da, dd = _ssd_bwd_call(*res, dy)
    return jnp.concatenate([dxs, dbm, dcm], axis=1), ddt, da, dd


ssd_scan.defvjp(_ssd_vjp_fwd, _ssd_vjp_bwd)


ANY = pl.BlockSpec(memory_space=pl.ANY)


def _my_pos():
    return lax.axis_index("x"), lax.axis_index("y"), lax.axis_index("c")


def all_gather(x_shard):
    r, c_ = x_shard.shape

    def body(x_ref, out_ref, send_sems, recv_sems, local_sem):
        x, y, c = _my_pos()
        me, sibling = (x, y, c), (x, y, 1 - c)
        chips = [(1 - x, y), (x, 1 - y), (1 - x, 1 - y)]

        def slot(px, py, pc):
            return out_ref.at[4 * px + 2 * py + pc]

        def copy(k, block, to, src=None):
            return pltpu.make_async_remote_copy(
                src_ref=slot(*block) if src is None else src, dst_ref=slot(*block),
                send_sem=send_sems.at[k], recv_sem=recv_sems.at[k], device_id=to, device_id_type=MESH)

        mine = pltpu.make_async_copy(x_ref, slot(*me), local_sem)
        mine.start()
        first = [copy(0, me, sibling, src=x_ref)]
        first += [copy(1 + j, me, (*chip, c), src=x_ref) for j, chip in enumerate(chips)]
        for cp in first:
            cp.start()
        passed = [copy(4 + j, (*chip, c), sibling) for j, chip in enumerate(chips)]
        for j, chip in enumerate(chips):
            copy(1 + j, (*chip, c), me).wait_recv()
            passed[j].start()
        copy(0, sibling, me).wait_recv()
        for j, chip in enumerate(chips):
            copy(4 + j, (*chip, 1 - c), me).wait_recv()
        for cp in first + passed:
            cp.wait_send()
        mine.wait()

    return pl.pallas_call(
        body, out_shape=jax.ShapeDtypeStruct((N_DEV, r, c_), x_shard.dtype), in_specs=[ANY], out_specs=ANY,
        scratch_shapes=[pltpu.SemaphoreType.DMA((N_DEV - 1,)), pltpu.SemaphoreType.DMA((N_DEV - 1,)),
                        pltpu.SemaphoreType.DMA],
        name="all_gather")(x_shard)


def exchange(x):
    def body(x_ref, out_ref, send_sems, recv_sems, local_sem):
        x, y, c = _my_pos()
        me = 4 * x + 2 * y + c
        local = pltpu.make_async_copy(x_ref.at[me], out_ref.at[me], local_sem)
        local.start()
        copies = []
        for k in range(1, N_DEV):
            px = 1 - x if k & 4 else x
            py = 1 - y if k & 2 else y
            pc = 1 - c if k & 1 else c
            cp = pltpu.make_async_remote_copy(
                src_ref=x_ref.at[4 * px + 2 * py + pc], dst_ref=out_ref.at[me],
                send_sem=send_sems.at[k - 1], recv_sem=recv_sems.at[k - 1],
                device_id=(px, py, pc), device_id_type=MESH)
            cp.start()
            copies.append(cp)
        for cp in copies:
            cp.wait_recv()
        for cp in copies:
            cp.wait_send()
        local.wait()

    return pl.pallas_call(
        body, out_shape=jax.ShapeDtypeStruct(x.shape, x.dtype), in_specs=[ANY], out_specs=ANY,
        scratch_shapes=[pltpu.SemaphoreType.DMA((N_DEV - 1,)), pltpu.SemaphoreType.DMA((N_DEV - 1,)),
                        pltpu.SemaphoreType.DMA],
        name="exchange")(x)


def _in_chunks(fn, flat, axis):
    rows = flat.shape[axis]
    step = min(COMM_ROWS, rows)
    assert rows % step == 0
    outs = [fn(lax.slice_in_dim(flat, s, s + step, axis=axis)) for s in range(0, rows, step)]
    return outs[0] if len(outs) == 1 else jnp.concatenate(outs, axis=1)


def reduce_adamw(parts, w, m, v):
    p, r, c_ = parts.shape
    br = _tile(r, 128) if r % 8 == 0 else r
    if r % br:
        br = r
    c1 = 1.0 - ADAM_B1 ** ADAM_STEP
    c2 = 1.0 - ADAM_B2 ** ADAM_STEP

    def body(p_ref, w_ref, m_ref, v_ref, g_ref, d_ref, m2_ref, v2_ref):
        g = p_ref[0].astype(F32)
        for i in range(1, p):
            g = g + p_ref[i].astype(F32)
        m2 = ADAM_B1 * m_ref[...] + (1.0 - ADAM_B1) * g
        v2 = ADAM_B2 * v_ref[...] + (1.0 - ADAM_B2) * (g * g)
        g_ref[...] = g
        m2_ref[...] = m2
        v2_ref[...] = v2
        d_ref[...] = -ADAM_LR * ((m2 / c1) / (jnp.sqrt(v2 / c2) + ADAM_EPS) + ADAM_WD * w_ref[...])

    blk = pl.BlockSpec((br, c_), lambda i: (i, 0))
    return pl.pallas_call(
        body, grid=(r // br,), in_specs=[pl.BlockSpec((p, br, c_), lambda i: (0, i, 0)), blk, blk, blk],
        out_specs=[blk] * 4, out_shape=[jax.ShapeDtypeStruct((r, c_), F32)] * 4,
        compiler_params=_cparams("parallel"), name="reduce_adamw")(parts, w, m, v)


WEIGHTS = ['ffn1_norm', 'ffn1_wi', 'ffn1_wo', 'mix_norm', 'w_in', 'mla_q_norm', 'mla_w_uq', 'mla_kv_norm',
           'mla_w_ukv', 'hgrn_lb_logits', 'hgrn_norm', 'ssm_conv_w', 'ssm_conv_b', 'ssm_a_log', 'ssm_dt_bias',
           'ssm_d', 'ssm_norm', 'w_o_mla', 'w_o_hgrn', 'w_o_ssm', 'w_out', 'ffn2_norm', 'ffn2_wi', 'ffn2_wo',
           'final_norm']
COL_SHARDED = ('ffn1_wi', 'w_in', 'mla_w_uq', 'mla_w_ukv', 'ffn2_wi')
ROW_SHARDED = ('ffn1_wo', 'w_o_mla', 'w_o_hgrn', 'w_o_ssm', 'w_out', 'ffn2_wo')
BIG = tuple(n for n in WEIGHTS if n in COL_SHARDED + ROW_SHARDED)
CONV_W = 'ssm_conv_w'
REPLICATED = tuple(n for n in WEIGHTS if n not in BIG and n != CONV_W)


@functools.partial(jax.custom_vjp, nondiff_argnums=(1,))
def split_cols(w, sizes):
    offs = [0]
    for s in sizes:
        offs.append(offs[-1] + s)
    return tuple(w[:, a:b] for a, b in zip(offs[:-1], offs[1:]))


def _split_cols_fwd(w, sizes):
    return split_cols(w, sizes), None


def _split_cols_bwd(sizes, _, gs):
    return (jnp.concatenate(gs, axis=1),)


split_cols.defvjp(_split_cols_fwd, _split_cols_bwd)


def _full_weight(name, g_layout, layer):
    blocks = g_layout[:, layer]
    if name in ROW_SHARDED:
        return blocks.reshape(-1, blocks.shape[-1])
    k, n = blocks.shape[1:]
    return jnp.transpose(blocks, (1, 0, 2)).reshape(k, N_DEV * n)


def _rope_tables(t):
    half = MLA_ROPE // 2
    inv = 1.0 / (ROPE_THETA ** (jnp.arange(0, MLA_ROPE, 2, dtype=F32) / MLA_ROPE))
    ang = jnp.arange(t, dtype=F32)[:, None] * inv[None, :]
    reps = LANES // half
    return jnp.tile(jnp.cos(ang), (1, reps)), jnp.tile(jnp.sin(ang), (1, reps))


def _ffn(x, norm, wi, wo):
    dff = wo.shape[0]
    h = rmsnorm(norm[None], x)[0]
    wg, wu = split_cols(wi, (dff, dff))
    return x + 0.5 * mm(swiglu(mm(h, wg), mm(h, wu))[0], wo)


def _per_head(w, widths, pad_to):
    k = w.shape[0]
    w3 = w.reshape(k, -1, sum(widths))
    outs, off = [], 0
    for wd in widths:
        part = w3[:, :, off:off + wd]
        if wd < pad_to:
            part = jnp.pad(part, ((0, 0), (0, 0), (0, pad_to - wd)))
        outs.append(part.reshape(k, -1))
        off += wd
    return outs


def _layer(x, p, lb, cos, sin):
    t, d = x.shape
    inner = 2 * d
    conv_dim = inner + 2 * SSM_GROUPS * SSM_STATE
    n_ssm_heads = inner // SSM_HEADDIM
    x = _ffn(x, p['ffn1_norm'], p['ffn1_wi'], p['ffn1_wo'])

    h = rmsnorm(p['mix_norm'][None], x)[0]
    sizes = (MLA_Q_RANK, MLA_KV_RANK, MLA_ROPE, HG_WIDTH, HG_WIDTH, HG_WIDTH, HG_WIDTH,
             inner, conv_dim, n_ssm_heads, d, d, d)
    (w_q, w_kv, w_kpe, w_hq, w_hf, w_hi, w_hg, w_z, w_xbc, w_dt, w_ga, w_gb, w_gc) = split_cols(p['w_in'], sizes)

    qn = rmsnorm(p['mla_q_norm'][None], mm(h, w_q))[0]
    kvn = rmsnorm(p['mla_kv_norm'][None], mm(h, w_kv))[0]
    wq_nope, wq_pe = _per_head(p['mla_w_uq'], (MLA_NOPE, MLA_ROPE), LANES)
    wk_nope, wv = _per_head(p['mla_w_ukv'], (MLA_NOPE, MLA_V), LANES)
    q_nope = mm(qn, wq_nope)
    q_pe = rope(mm(qn, wq_pe), cos, sin)[0]
    k_nope = mm(kvn, wk_nope)
    v = mm(kvn, wv)
    k_rot = rope(mm(h, jnp.pad(w_kpe, ((0, 0), (0, LANES - MLA_ROPE)))), cos, sin)[0]
    q = jnp.concatenate([q_nope.reshape(t, MLA_HEADS, LANES), q_pe.reshape(t, MLA_HEADS, LANES)], axis=2)
    k = jnp.concatenate([k_nope.reshape(t, MLA_HEADS, LANES),
                         jnp.broadcast_to(k_rot[:, None, :], (t, MLA_HEADS, LANES))], axis=2)
    o = attention(q.reshape(t, -1).astype(BF16), k.reshape(t, -1).astype(BF16), v.astype(BF16))
    y_a = mm(o, p['w_o_mla'])

    o = hgrn_scan(mm(h, w_hq), mm(h, w_hf), mm(h, w_hi), lb[None])
    o = hgrn_out(p['hgrn_norm'][None], o, mm(h, w_hg))[0]
    y_b = mm(o, p['w_o_hgrn'])

    xbc = silu_op(conv(p['ssm_conv_w'], p['ssm_conv_b'][None], mm(h, w_xbc)))[0]
    dtx = dt_expand(p['ssm_dt_bias'][None], mm(h, w_dt))[0]
    y = ssd_scan(xbc, dtx, jnp.repeat(p['ssm_a_log'], SSM_HEADDIM)[None], jnp.repeat(p['ssm_d'], SSM_HEADDIM)[None])
    y = ssm_norm(p['ssm_norm'][None], y, mm(h, w_z))[0]
    y_c = mm(y, p['w_o_ssm'])

    merged = merge(y_a, y_b, y_c, mm(h, w_ga), mm(h, w_gb), mm(h, w_gc))[0]
    x = x + mm(merged, p['w_out'])
    return _ffn(x, p['ffn2_norm'], p['ffn2_wi'], p['ffn2_wo'])


def _local_loss(params, x, target):
    depth = params['ffn1_norm'].shape[0]
    cos, sin = _rope_tables(x.shape[0])
    prob = jax.nn.softmax(params['hgrn_lb_logits'], axis=0)
    lower = jnp.cumsum(prob, axis=0) - prob[0:1]
    for layer in range(depth):
        p = {}
        for n in WEIGHTS:
            if n in BIG:
                p[n] = _full_weight(n, params[n], layer)
            elif n != 'final_norm':
                p[n] = params[n][layer]
        x = _layer(x, p, lower[layer], cos, sin)
    return jnp.sum(loss_rows(params['final_norm'][None], x, target)[0])


def _flat_rows(a):
    return a.reshape(-1, COMM_COLS)


def _pack(arrays, rows):
    flat = jnp.concatenate([_flat_rows(a) for a in arrays], axis=0)
    return jnp.pad(flat, ((0, rows - flat.shape[0]), (0, 0)))


def _pack_vec(arrays, rows):
    flat = jnp.concatenate([a.reshape(-1) for a in arrays])
    return jnp.pad(flat, (0, rows * COMM_COLS - flat.shape[0])).reshape(rows, COMM_COLS)


def _unpack(flat, shapes, lead=()):
    flat = flat.reshape(lead + (-1,))
    outs, off = [], 0
    for s in shapes:
        n = 1
        for dim in s:
            n *= dim
        outs.append(flat[..., off:off + n].reshape(lead + tuple(s)))
        off += n
    return outs


def _round_up(n, m):
    return -(-n // m) * m


def _step(a):
    x = a['x'][0]
    target = a['loss_target'][0]
    me = 4 * lax.axis_index("x") + 2 * lax.axis_index("y") + lax.axis_index("c")

    big_shapes = [a[n].shape for n in BIG]
    big_rows = sum(a[n].size for n in BIG) // COMM_COLS
    big_rows = _round_up(big_rows, min(COMM_ROWS, _round_up(big_rows, 8)))
    conv_shape = a[CONV_W].shape
    conv_full_shape = conv_shape[:-1] + (conv_shape[-1] * N_DEV,)
    rep_shapes = [a[n].shape for n in REPLICATED]
    n_small = 1 + sum(a[n].size for n in REPLICATED) + a[CONV_W].size * N_DEV
    small_rows = _round_up(-(-n_small // COMM_COLS), 8)

    gathered = _in_chunks(all_gather, _pack([a[n].astype(BF16) for n in BIG], big_rows), 0)
    params = dict(zip(BIG, _unpack(gathered, big_shapes, lead=(N_DEV,))))
    conv_blocks = _unpack(all_gather(_pack_vec([a[CONV_W]], small_rows)), [conv_shape], lead=(N_DEV,))[0]
    params[CONV_W] = jnp.moveaxis(conv_blocks, 0, -2).reshape(conv_full_shape)
    for n in REPLICATED:
        params[n] = a[n]

    loss, (gp, gx) = jax.value_and_grad(_local_loss, argnums=(0, 1))(params, x, target)

    send = jnp.concatenate([gp[n].reshape(N_DEV, -1, COMM_COLS) for n in BIG], axis=1)
    send = jnp.pad(send, ((0, 0), (0, big_rows - send.shape[1]), (0, 0)))
    parts = _in_chunks(exchange, send, 1)
    res = reduce_adamw(parts, *[_pack([a[pre + n] for n in BIG], big_rows) for pre in ('', 'm_', 'v_')])
    big_out = [dict(zip(BIG, _unpack(r, big_shapes))) for r in res]

    small = _pack_vec([loss.reshape(1)] + [gp[n] for n in REPLICATED] + [gp[CONV_W]], small_rows)
    zero1, one1 = jnp.zeros((1,), F32), jnp.ones((1,), F32)
    zero_c, one_c = jnp.zeros(conv_full_shape, F32), jnp.ones(conv_full_shape, F32)
    small_w = _pack_vec([zero1] + [a[n] for n in REPLICATED] + [zero_c], small_rows)
    small_m = _pack_vec([zero1] + [a['m_' + n] for n in REPLICATED] + [zero_c], small_rows)
    small_v = _pack_vec([one1] + [a['v_' + n] for n in REPLICATED] + [one_c], small_rows)
    res = reduce_adamw(all_gather(small), small_w, small_m, small_v)
    small_out = []
    for r in res:
        pieces = _unpack(r, [(1,)] + rep_shapes + [conv_full_shape])
        small_out.append((pieces[0], dict(zip(REPLICATED, pieces[1:-1])), pieces[-1]))
    total_loss = small_out[0][0][0]

    width = conv_shape[-1]
    g_conv = lax.dynamic_slice_in_dim(small_out[0][2], me * width, width, axis=len(conv_shape) - 1)
    conv_rows = -(-a[CONV_W].size // COMM_COLS)
    conv_res = reduce_adamw(_pack_vec([g_conv], conv_rows)[None],
                            *[_pack_vec([a[pre + CONV_W]], conv_rows) for pre in ('', 'm_', 'v_')])
    conv_out = [_unpack(r, [conv_shape])[0] for r in conv_res]

    outs = [total_loss, gx[None]]
    for kind in range(4):
        for n in WEIGHTS:
            if n in BIG:
                outs.append(big_out[kind][n])
            elif n == CONV_W:
                outs.append(conv_out[kind])
            else:
                outs.append(small_out[kind][1][n])
    return tuple(outs)


def kernel(x, ffn1_norm, ffn1_wi, ffn1_wo, mix_norm, w_in, mla_q_norm, mla_w_uq, mla_kv_norm, mla_w_ukv, hgrn_lb_logits, hgrn_norm, ssm_conv_w, ssm_conv_b, ssm_a_log, ssm_dt_bias, ssm_d, ssm_norm, w_o_mla, w_o_hgrn, w_o_ssm, w_out, ffn2_norm, ffn2_wi, ffn2_wo, final_norm, loss_target, m_ffn1_norm, m_ffn1_wi, m_ffn1_wo, m_mix_norm, m_w_in, m_mla_q_norm, m_mla_w_uq, m_mla_kv_norm, m_mla_w_ukv, m_hgrn_lb_logits, m_hgrn_norm, m_ssm_conv_w, m_ssm_conv_b, m_ssm_a_log, m_ssm_dt_bias, m_ssm_d, m_ssm_norm, m_w_o_mla, m_w_o_hgrn, m_w_o_ssm, m_w_out, m_ffn2_norm, m_ffn2_wi, m_ffn2_wo, m_final_norm, v_ffn1_norm, v_ffn1_wi, v_ffn1_wo, v_mix_norm, v_w_in, v_mla_q_norm, v_mla_w_uq, v_mla_kv_norm, v_mla_w_ukv, v_hgrn_lb_logits, v_hgrn_norm, v_ssm_conv_w, v_ssm_conv_b, v_ssm_a_log, v_ssm_dt_bias, v_ssm_d, v_ssm_norm, v_w_o_mla, v_w_o_hgrn, v_w_o_ssm, v_w_out, v_ffn2_norm, v_ffn2_wi, v_ffn2_wo, v_final_norm):
    return _step(dict(locals()))
```

```python
import functools

import jax
import jax.numpy as jnp
from jax import lax
from jax.experimental import pallas as pl
from jax.experimental.pallas import tpu as pltpu

F32 = jnp.float32
BF16 = jnp.bfloat16
HI = lax.Precision.HIGHEST
MESH = pl.DeviceIdType.MESH

EPS = 1e-6
CHUNK = 64
N_DEV = 8

MLA_HEADS = 16
MLA_Q_RANK = 512
MLA_KV_RANK = 512
MLA_NOPE = 128
MLA_ROPE = 64
MLA_V = 128
ROPE_THETA = 10000.0
HG_HEADS = 16
HG_DK = 128
HG_WIDTH = HG_HEADS * HG_DK
SSM_HEADDIM = 64
SSM_GROUPS = 8
SSM_STATE = 128
SSM_CONV = 4

ADAM_LR = 0.001
ADAM_B1 = 0.9
ADAM_B2 = 0.999
ADAM_EPS = 1e-08
ADAM_WD = 0.01
ADAM_STEP = 10

LANES = 128
VMEM_LIMIT = 48 * 1024 * 1024
ROW_BLOCK_ELEMS = 128 * 1024
COMM_COLS = 1024
COMM_BYTES = 128 * 1024 * 1024


def _cparams(*sem):
    return pltpu.CompilerParams(dimension_semantics=sem, vmem_limit_bytes=VMEM_LIMIT)


def _tile(dim, pref):
    t = pref
    while t >= LANES:
        if dim % t == 0:
            return t
        t //= 2
    return dim


def _mm_call(a, b, mode, out_dtype, name):
    if mode == "nn":
        (m, k), (k2, n) = a.shape, b.shape
    elif mode == "nt":
        (m, k), (n, k2) = a.shape, b.shape
    else:
        (k, m), (k2, n) = a.shape, b.shape
    assert k == k2, (a.shape, b.shape, mode)
    tm, tn, tk = _tile(m, 1024), _tile(n, 1024), _tile(k, 1024)
    nk = k // tk
    if mode == "nn":
        a_spec = pl.BlockSpec((tm, tk), lambda i, j, kk: (i, kk))
        b_spec = pl.BlockSpec((tk, tn), lambda i, j, kk: (kk, j))
        dims = (((1,), (0,)), ((), ()))
    elif mode == "nt":
        a_spec = pl.BlockSpec((tm, tk), lambda i, j, kk: (i, kk))
        b_spec = pl.BlockSpec((tn, tk), lambda i, j, kk: (j, kk))
        dims = (((1,), (1,)), ((), ()))
    else:
        a_spec = pl.BlockSpec((tk, tm), lambda i, j, kk: (kk, i))
        b_spec = pl.BlockSpec((tk, tn), lambda i, j, kk: (kk, j))
        dims = (((0,), (0,)), ((), ()))

    def body(a_ref, b_ref, o_ref, acc_ref):
        kk = pl.program_id(2)

        @pl.when(kk == 0)
        def _():
            acc_ref[...] = jnp.zeros_like(acc_ref)

        acc_ref[...] += lax.dot_general(a_ref[...].astype(BF16), b_ref[...].astype(BF16), dims,
                                        preferred_element_type=F32)

        @pl.when(kk == nk - 1)
        def _():
            o_ref[...] = acc_ref[...].astype(o_ref.dtype)

    return pl.pallas_call(
        body, grid=(m // tm, n // tn, nk), in_specs=[a_spec, b_spec],
        out_specs=pl.BlockSpec((tm, tn), lambda i, j, kk: (i, j)),
        out_shape=jax.ShapeDtypeStruct((m, n), out_dtype),
        scratch_shapes=[pltpu.VMEM((tm, tn), F32)],
        compiler_params=_cparams("parallel", "parallel", "arbitrary"), name=name)(a, b)


@jax.custom_vjp
def mm(a, w):
    return _mm_call(a.astype(BF16), w, "nn", F32, "mm_fwd")


def _mm_fwd(a, w):
    a16 = a.astype(BF16)
    return _mm_call(a16, w, "nn", F32, "mm_fwd"), (a16, w)


def _mm_bwd(res, g):
    a16, w = res
    g16 = g.astype(BF16)
    return _mm_call(g16, w, "nt", F32, "mm_da"), _mm_call(a16, g16, "tn", w.dtype, "mm_dw")


mm.defvjp(_mm_fwd, _mm_bwd)


def _row_block(t, widths):
    bt = 8
    while 2 * bt * max(widths) <= ROW_BLOCK_ELEMS:
        bt *= 2
    while t % bt:
        bt //= 2
    return bt


def make_rowwise(fn, name, n_par, group_width=None, shared=(), nondiff=()):
    def specs(args):
        t = max(a.shape[0] for a in args)
        cut = next(a for i, a in enumerate(args) if i >= n_par and i not in shared)
        gw = group_width(cut.shape[1]) if callable(group_width) else group_width
        groups = cut.shape[1] // gw if gw else 1
        ws = [a.shape[1] if i in shared else a.shape[1] // groups for i, a in enumerate(args)]
        ows = out_widths(ws)
        bt = _row_block(t, ws + ows)
        sp = []
        for i, a in enumerate(args):
            col = (lambda g: 0) if i in shared else (lambda g: g)
            if i < n_par:
                sp.append(pl.BlockSpec((1, ws[i]), lambda g, r, col=col: (0, col(g))))
            else:
                sp.append(pl.BlockSpec((bt, ws[i]), lambda g, r, col=col: (r, col(g))))
        return t, bt, groups, ows, sp

    def out_widths(ws):
        blocks = [jax.ShapeDtypeStruct((1 if i < n_par else 8, w), F32) for i, w in enumerate(ws)]
        return [o.shape[1] for o in jax.eval_shape(fn, *blocks)]

    def fwd_call(*args):
        t, bt, groups, ows, in_specs = specs(args)
        n_in = len(args)

        def body(*refs):
            outs = fn(*[r[...] for r in refs[:n_in]])
            for r, o in zip(refs[n_in:], outs):
                r[...] = o

        return pl.pallas_call(
            body, grid=(groups, t // bt), in_specs=in_specs,
            out_specs=[pl.BlockSpec((bt, w), lambda g, r: (r, g)) for w in ows],
            out_shape=[jax.ShapeDtypeStruct((t, w * groups), F32) for w in ows],
            compiler_params=_cparams("parallel", "parallel"), name=name + "_fwd")(*args)

    def bwd_call(args, gs):
        t, bt, groups, ows, in_specs = specs(args)
        n_in, n_out = len(args), len(gs)
        diff = [i for i in range(n_in) if i not in nondiff]
        g_specs = [pl.BlockSpec((bt, w), lambda g, r: (r, g)) for w in ows]
        o_specs, o_shapes = [], []
        for i in diff:
            o_specs.append(in_specs[i])
            o_shapes.append(jax.ShapeDtypeStruct(args[i].shape, F32))

        def body(*refs):
            r_idx = pl.program_id(1)
            vals = [r[...] for r in refs[:n_in]]
            cts = tuple(r[...] for r in refs[n_in:n_in + n_out])

            def f_diff(*dv):
                full = list(vals)
                for i, v in zip(diff, dv):
                    full[i] = v
                return tuple(fn(*full))

            _, vjp = jax.vjp(f_diff, *[vals[i] for i in diff])
            grads = vjp(cts)
            for i, g_val, ref in zip(diff, grads, refs[n_in + n_out:]):
                if i < n_par:
                    @pl.when(r_idx == 0)
                    def _(ref=ref, g_val=g_val):
                        ref[...] = g_val

                    @pl.when(r_idx != 0)
                    def _(ref=ref, g_val=g_val):
                        ref[...] += g_val
                else:
                    ref[...] = g_val

        outs = pl.pallas_call(
            body, grid=(groups, t // bt), in_specs=in_specs + g_specs, out_specs=o_specs, out_shape=o_shapes,
            compiler_params=_cparams("parallel", "arbitrary"), name=name + "_bwd")(*args, *gs)
        full = [jnp.zeros_like(a) for a in args]
        for i, o in zip(diff, outs):
            full[i] = o
        return tuple(full)

    @jax.custom_vjp
    def op(*args):
        return tuple(fwd_call(*args))

    def op_fwd(*args):
        return tuple(fwd_call(*args)), args

    def op_bwd(args, gs):
        return bwd_call(args, gs)

    op.defvjp(op_fwd, op_bwd)
    return op


def _silu(x):
    return x * jax.nn.sigmoid(x)


def _rmsnorm_fn(w, x):
    return (x * lax.rsqrt(jnp.mean(x * x, axis=-1, keepdims=True) + EPS) * w,)


def _swiglu_fn(g, u):
    return (_silu(g) * u,)


def _silu_fn(x):
    return (_silu(x),)


def _rope_fn(x, cos, sin):
    i = lax.broadcasted_iota(jnp.int32, (LANES, LANES), 0)
    j = lax.broadcasted_iota(jnp.int32, (LANES, LANES), 1)
    half = MLA_ROPE // 2
    first = (j % MLA_ROPE) < half
    p = jnp.where(first & (i == j + half), -1.0, 0.0) + jnp.where((~first) & (i == j - half), 1.0, 0.0)
    return (x * cos + jnp.dot(x, p.astype(F32), precision=HI) * sin,)


def _hgrn_out_fn(w, o, g):
    return (o * lax.rsqrt(jnp.mean(o * o, axis=-1, keepdims=True) + EPS) * w * _silu(g),)


def _softplus(x):
    return jnp.maximum(x, 0.0) + jnp.log(1.0 + jnp.exp(-jnp.abs(x)))


def _dt_expand_fn(bias, dt_raw):
    nh = dt_raw.shape[1]
    h = lax.broadcasted_iota(jnp.int32, (nh, nh * SSM_HEADDIM), 0)
    c = lax.broadcasted_iota(jnp.int32, (nh, nh * SSM_HEADDIM), 1)
    e = (c // SSM_HEADDIM == h).astype(F32)
    return (jnp.dot(_softplus(dt_raw + bias), e, precision=HI),)


def _ssm_norm_fn(w, y, z):
    y = y * _silu(z)
    return (y * lax.rsqrt(jnp.mean(y * y, axis=-1, keepdims=True) + EPS) * w,)


def _merge_fn(ya, yb, yc, ga, gb, gc):
    return (jax.nn.sigmoid(ga) * ya + jax.nn.sigmoid(gb) * yb + jax.nn.sigmoid(gc) * yc,)


def _loss_fn(w, x, tgt):
    y = x * lax.rsqrt(jnp.mean(x * x, axis=-1, keepdims=True) + EPS) * w
    err = y - tgt
    return (0.5 * jnp.mean(err * err, axis=-1, keepdims=True),)


rmsnorm = make_rowwise(_rmsnorm_fn, "rmsnorm", 1)
swiglu = make_rowwise(_swiglu_fn, "swiglu", 0, group_width=512)
silu_op = make_rowwise(_silu_fn, "silu", 0, group_width=512)
rope = make_rowwise(_rope_fn, "rope", 0, group_width=LANES, shared=(1, 2), nondiff=(1, 2))
hgrn_out = make_rowwise(_hgrn_out_fn, "hgrn_out", 1, group_width=HG_DK)
dt_expand = make_rowwise(_dt_expand_fn, "dt_expand", 1)
ssm_norm = make_rowwise(_ssm_norm_fn, "ssm_norm", 1, group_width=lambda w: w // SSM_GROUPS)
merge = make_rowwise(_merge_fn, "merge", 0, group_width=512)
loss_rows = make_rowwise(_loss_fn, "loss", 1, nondiff=(2,))


HALO = 8


def _conv_blocks(t, c):
    cw = _tile(c, 512)
    bt = _row_block(t, [cw])
    return cw, bt


def _conv_fwd_call(w, b, x):
    t, c = x.shape
    cw, bt = _conv_blocks(t, c)
    hb = bt // HALO

    def body(w_ref, b_ref, x_ref, prev_ref, y_ref):
        r = pl.program_id(1)
        prev = jnp.where(r == 0, 0.0, prev_ref[...])
        xx = jnp.concatenate([prev, x_ref[...]], axis=0)
        acc = jnp.zeros((bt, cw), F32) + b_ref[...]
        for k in range(SSM_CONV):
            sh = SSM_CONV - 1 - k
            xs = xx if sh == 0 else pltpu.roll(xx, sh, axis=0)
            acc = acc + w_ref[k:k + 1, :] * xs[HALO:, :]
        y_ref[...] = acc

    return pl.pallas_call(
        body, grid=(c // cw, t // bt),
        in_specs=[pl.BlockSpec((SSM_CONV, cw), lambda g, r: (0, g)), pl.BlockSpec((1, cw), lambda g, r: (0, g)),
                  pl.BlockSpec((bt, cw), lambda g, r: (r, g)),
                  pl.BlockSpec((HALO, cw), lambda g, r: (jnp.maximum(r * hb - 1, 0), g))],
        out_specs=pl.BlockSpec((bt, cw), lambda g, r: (r, g)),
        out_shape=jax.ShapeDtypeStruct((t, c), F32),
        compiler_params=_cparams("parallel", "parallel"), name="conv_fwd")(w, b, x, x)


def _conv_bwd_call(w, x, dy):
    t, c = x.shape
    cw, bt = _conv_blocks(t, c)
    hb = bt // HALO
    nr = t // bt

    def body(w_ref, x_ref, prev_ref, dy_ref, next_ref, dx_ref, dw_ref, db_ref):
        r = pl.program_id(1)
        prev = jnp.where(r == 0, 0.0, prev_ref[...])
        nxt = jnp.where(r == nr - 1, 0.0, next_ref[...])
        xx = jnp.concatenate([prev, x_ref[...]], axis=0)
        dd = jnp.concatenate([dy_ref[...], nxt], axis=0)
        dy_val = dy_ref[...]
        dx = jnp.zeros((bt, cw), F32)
        dws = []
        for k in range(SSM_CONV):
            sh = SSM_CONV - 1 - k
            xs = xx if sh == 0 else pltpu.roll(xx, sh, axis=0)
            ds = dd if sh == 0 else pltpu.roll(dd, bt + HALO - sh, axis=0)
            dx = dx + w_ref[k:k + 1, :] * ds[:bt, :]
            dws.append(jnp.sum(dy_val * xs[HALO:, :], axis=0, keepdims=True))
        dx_ref[...] = dx
        dw = jnp.concatenate(dws, axis=0)
        db = jnp.sum(dy_val, axis=0, keepdims=True)

        @pl.when(r == 0)
        def _():
            dw_ref[...] = dw
            db_ref[...] = db

        @pl.when(r != 0)
        def _():
            dw_ref[...] += dw
            db_ref[...] += db

    return pl.pallas_call(
        body, grid=(c // cw, nr),
        in_specs=[pl.BlockSpec((SSM_CONV, cw), lambda g, r: (0, g)),
                  pl.BlockSpec((bt, cw), lambda g, r: (r, g)),
                  pl.BlockSpec((HALO, cw), lambda g, r: (jnp.maximum(r * hb - 1, 0), g)),
                  pl.BlockSpec((bt, cw), lambda g, r: (r, g)),
                  pl.BlockSpec((HALO, cw), lambda g, r: (jnp.minimum((r + 1) * hb, nr * hb - 1), g))],
        out_specs=[pl.BlockSpec((bt, cw), lambda g, r: (r, g)),
                   pl.BlockSpec((SSM_CONV, cw), lambda g, r: (0, g)), pl.BlockSpec((1, cw), lambda g, r: (0, g))],
        out_shape=[jax.ShapeDtypeStruct((t, c), F32), jax.ShapeDtypeStruct((SSM_CONV, c), F32),
                   jax.ShapeDtypeStruct((1, c), F32)],
        compiler_params=_cparams("parallel", "arbitrary"), name="conv_bwd")(w, x, x, dy, dy)


@jax.custom_vjp
def conv(w, b, x):
    return _conv_fwd_call(w, b, x)


def _conv_vjp_fwd(w, b, x):
    return _conv_fwd_call(w, b, x), (w, x)


def _conv_vjp_bwd(res, dy):
    w, x = res
    dx, dw, db = _conv_bwd_call(w, x, dy)
    return dw, db, dx


conv.defvjp(_conv_vjp_fwd, _conv_vjp_bwd)


ATT_DQK = 2 * LANES
ATT_SCALE = (MLA_NOPE + MLA_ROPE) ** -0.5
NEG = -1e30


def _att_tile(t):
    return min(512, max(CHUNK, t // 4))


def _att_mask(s, tq):
    row = lax.broadcasted_iota(jnp.int32, (tq, tq), 0) // CHUNK
    col = lax.broadcasted_iota(jnp.int32, (tq, tq), 1) // CHUNK
    return jnp.where(col <= row, s, NEG)


def _att_fwd_call(q, k, v):
    t = q.shape[0]
    h = q.shape[1] // ATT_DQK
    dv = v.shape[1] // h
    tq = _att_tile(t)
    nq = t // tq
    dims_nt = (((1,), (1,)), ((), ()))

    def body(q_ref, k_ref, v_ref, o_ref, lse_ref, m_scr, l_scr, acc_scr):
        i, j = pl.program_id(1), pl.program_id(2)

        @pl.when(j == 0)
        def _():
            m_scr[...] = jnp.full_like(m_scr, NEG)
            l_scr[...] = jnp.zeros_like(l_scr)
            acc_scr[...] = jnp.zeros_like(acc_scr)

        def step(masked):
            s = lax.dot_general(q_ref[...], k_ref[...], dims_nt, preferred_element_type=F32) * ATT_SCALE
            if masked:
                s = _att_mask(s, tq)
            m_new = jnp.maximum(m_scr[...], jnp.max(s, axis=-1, keepdims=True))
            alpha = jnp.exp(m_scr[...] - m_new)
            p = jnp.exp(s - m_new)
            l_scr[...] = alpha * l_scr[...] + jnp.sum(p, axis=-1, keepdims=True)
            acc_scr[...] = alpha * acc_scr[...] + jnp.dot(p.astype(BF16), v_ref[...], preferred_element_type=F32)
            m_scr[...] = m_new

        pl.when(j < i)(functools.partial(step, False))

        @pl.when(j == i)
        def _():
            step(True)
            o_ref[...] = acc_scr[...] / l_scr[...]
            lse_ref[0] = m_scr[...] + jnp.log(l_scr[...])

    return pl.pallas_call(
        body, grid=(h, nq, nq),
        in_specs=[pl.BlockSpec((tq, ATT_DQK), lambda hh, i, j: (i, hh)),
                  pl.BlockSpec((tq, ATT_DQK), lambda hh, i, j: (jnp.minimum(j, i), hh)),
                  pl.BlockSpec((tq, dv), lambda hh, i, j: (jnp.minimum(j, i), hh))],
        out_specs=[pl.BlockSpec((tq, dv), lambda hh, i, j: (i, hh)),
                   pl.BlockSpec((1, tq, 1), lambda hh, i, j: (hh, i, 0))],
        out_shape=[jax.ShapeDtypeStruct((t, h * dv), F32), jax.ShapeDtypeStruct((h, t, 1), F32)],
        scratch_shapes=[pltpu.VMEM((tq, 1), F32), pltpu.VMEM((tq, 1), F32), pltpu.VMEM((tq, dv), F32)],
        compiler_params=_cparams("parallel", "parallel", "arbitrary"), name="att_fwd")(q, k, v)


def _att_dq_call(q, k, v, o, lse, do):
    t = q.shape[0]
    h = q.shape[1] // ATT_DQK
    dv = v.shape[1] // h
    tq = _att_tile(t)
    nq = t // tq
    dims_nt = (((1,), (1,)), ((), ()))

    def body(q_ref, k_ref, v_ref, o_ref, lse_ref, do_ref, dq_ref, delta_ref, acc_scr, d_scr):
        i, j = pl.program_id(1), pl.program_id(2)

        @pl.when(j == 0)
        def _():
            acc_scr[...] = jnp.zeros_like(acc_scr)
            d_scr[...] = jnp.sum(do_ref[...] * o_ref[...], axis=-1, keepdims=True)

        def step(masked):
            s = lax.dot_general(q_ref[...], k_ref[...], dims_nt, preferred_element_type=F32) * ATT_SCALE
            if masked:
                s = _att_mask(s, tq)
            p = jnp.exp(s - lse_ref[0])
            dp = lax.dot_general(do_ref[...].astype(BF16), v_ref[...], dims_nt, preferred_element_type=F32)
            ds = p * (dp - d_scr[...]) * ATT_SCALE
            acc_scr[...] += jnp.dot(ds.astype(BF16), k_ref[...], preferred_element_type=F32)

        pl.when(j < i)(functools.partial(step, False))

        @pl.when(j == i)
        def _():
            step(True)
            dq_ref[...] = acc_scr[...].astype(dq_ref.dtype)
            delta_ref[0] = d_scr[...]

    return pl.pallas_call(
        body, grid=(h, nq, nq),
        in_specs=[pl.BlockSpec((tq, ATT_DQK), lambda hh, i, j: (i, hh)),
                  pl.BlockSpec((tq, ATT_DQK), lambda hh, i, j: (jnp.minimum(j, i), hh)),
                  pl.BlockSpec((tq, dv), lambda hh, i, j: (jnp.minimum(j, i), hh)),
                  pl.BlockSpec((tq, dv), lambda hh, i, j: (i, hh)),
                  pl.BlockSpec((1, tq, 1), lambda hh, i, j: (hh, i, 0)),
                  pl.BlockSpec((tq, dv), lambda hh, i, j: (i, hh))],
        out_specs=[pl.BlockSpec((tq, ATT_DQK), lambda hh, i, j: (i, hh)),
                   pl.BlockSpec((1, tq, 1), lambda hh, i, j: (hh, i, 0))],
        out_shape=[jax.ShapeDtypeStruct(q.shape, q.dtype), jax.ShapeDtypeStruct((h, t, 1), F32)],
        scratch_shapes=[pltpu.VMEM((tq, ATT_DQK), F32), pltpu.VMEM((tq, 1), F32)],
        compiler_params=_cparams("parallel", "parallel", "arbitrary"), name="att_dq")(q, k, v, o, lse, do)


def _att_dkv_call(q, k, v, lse, delta, do):
    t = q.shape[0]
    h = q.shape[1] // ATT_DQK
    dv = v.shape[1] // h
    tq = _att_tile(t)
    nq = t // tq
    dims_nt = (((1,), (1,)), ((), ()))
    dims_tn = (((0,), (0,)), ((), ()))

    def body(q_ref, k_ref, v_ref, lse_ref, delta_ref, do_ref, dk_ref, dv_ref, dk_scr, dv_scr):
        j, i = pl.program_id(1), pl.program_id(2)

        @pl.when(i == 0)
        def _():
            dk_scr[...] = jnp.zeros_like(dk_scr)
            dv_scr[...] = jnp.zeros_like(dv_scr)

        def step(masked):
            s = lax.dot_general(q_ref[...], k_ref[...], dims_nt, preferred_element_type=F32) * ATT_SCALE
            if masked:
                s = _att_mask(s, tq)
            p = jnp.exp(s - lse_ref[0])
            do_b = do_ref[...].astype(BF16)
            dv_scr[...] += lax.dot_general(p.astype(BF16), do_b, dims_tn, preferred_element_type=F32)
            dp = lax.dot_general(do_b, v_ref[...], dims_nt, preferred_element_type=F32)
            ds = p * (dp - delta_ref[0]) * ATT_SCALE
            dk_scr[...] += lax.dot_general(ds.astype(BF16), q_ref[...], dims_tn, preferred_element_type=F32)

        pl.when(i > j)(functools.partial(step, False))
        pl.when(i == j)(functools.partial(step, True))

        @pl.when(i == nq - 1)
        def _():
            dk_ref[...] = dk_scr[...].astype(dk_ref.dtype)
            dv_ref[...] = dv_scr[...].astype(dv_ref.dtype)

    return pl.pallas_call(
        body, grid=(h, nq, nq),
        in_specs=[pl.BlockSpec((tq, ATT_DQK), lambda hh, j, i: (jnp.maximum(i, j), hh)),
                  pl.BlockSpec((tq, ATT_DQK), lambda hh, j, i: (j, hh)),
                  pl.BlockSpec((tq, dv), lambda hh, j, i: (j, hh)),
                  pl.BlockSpec((1, tq, 1), lambda hh, j, i: (hh, jnp.maximum(i, j), 0)),
                  pl.BlockSpec((1, tq, 1), lambda hh, j, i: (hh, jnp.maximum(i, j), 0)),
                  pl.BlockSpec((tq, dv), lambda hh, j, i: (jnp.maximum(i, j), hh))],
        out_specs=[pl.BlockSpec((tq, ATT_DQK), lambda hh, j, i: (j, hh)),
                   pl.BlockSpec((tq, dv), lambda hh, j, i: (j, hh))],
        out_shape=[jax.ShapeDtypeStruct(k.shape, k.dtype), jax.ShapeDtypeStruct(v.shape, v.dtype)],
        scratch_shapes=[pltpu.VMEM((tq, ATT_DQK), F32), pltpu.VMEM((tq, dv), F32)],
        compiler_params=_cparams("parallel", "parallel", "arbitrary"), name="att_dkv")(q, k, v, lse, delta, do)


@jax.custom_vjp
def attention(q, k, v):
    return _att_fwd_call(q, k, v)[0]


def _att_vjp_fwd(q, k, v):
    o, lse = _att_fwd_call(q, k, v)
    return o, (q, k, v, o, lse)


def _att_vjp_bwd(res, do):
    q, k, v, o, lse = res
    dq, delta = _att_dq_call(q, k, v, o, lse, do)
    dk, dv = _att_dkv_call(q, k, v, lse, delta, do)
    return dq, dk, dv


attention.defvjp(_att_vjp_fwd, _att_vjp_bwd)


MID = CHUNK // 2 - 1


def _tril(n):
    r = lax.broadcasted_iota(jnp.int32, (n, n), 0)
    c = lax.broadcasted_iota(jnp.int32, (n, n), 1)
    return c <= r


def _bdot(a, b, dims):
    return lax.dot_general(a.astype(BF16), b.astype(BF16), dims, preferred_element_type=F32)


NN = (((1,), (0,)), ((), ()))
NT = (((1,), (1,)), ((), ()))
TN = (((0,), (0,)), ((), ()))


def _hgrn_chunk(state, q_in, f_in, i_in, lb):
    tril = _tril(CHUNK)
    f = lb + (1.0 - lb) * jax.nn.sigmoid(f_in)
    logf = jnp.log(f)
    b = jnp.dot(tril.astype(F32), logf, precision=HI)
    q = _silu(q_in) * HG_DK ** -0.5
    k = 1.0 - f
    b_mid = b[MID:MID + 1, :]
    att = _bdot(q * jnp.exp(b - b_mid), k * jnp.exp(b_mid - b), NT)
    att = jnp.where(tril, att, 0.0)
    o = _bdot(q * jnp.exp(b), state, NT) + _bdot(att, i_in, NN)
    b_last = b[CHUNK - 1:CHUNK, :]
    new_state = jnp.exp(b_last) * state + _bdot(i_in, k * jnp.exp(b_last - b), TN)
    return o, new_state


def _hgrn_fwd_call(q, f, i, lb):
    t = q.shape[0]
    nc = t // CHUNK
    blk = pl.BlockSpec((CHUNK, HG_DK), lambda h, c: (c, h))

    def body(q_ref, f_ref, i_ref, lb_ref, o_ref, s_ref, s_scr):
        @pl.when(pl.program_id(1) == 0)
        def _():
            s_scr[...] = jnp.zeros_like(s_scr)

        s_ref[0, 0] = s_scr[...]
        o, ns = _hgrn_chunk(s_scr[...], q_ref[...], f_ref[...], i_ref[...], lb_ref[...])
        o_ref[...] = o
        s_scr[...] = ns

    return pl.pallas_call(
        body, grid=(HG_HEADS, nc),
        in_specs=[blk, blk, blk, pl.BlockSpec((1, HG_DK), lambda h, c: (0, h))],
        out_specs=[blk, pl.BlockSpec((1, 1, HG_DK, HG_DK), lambda h, c: (h, c, 0, 0))],
        out_shape=[jax.ShapeDtypeStruct((t, HG_WIDTH), F32), jax.ShapeDtypeStruct((HG_HEADS, nc, HG_DK, HG_DK), F32)],
        scratch_shapes=[pltpu.VMEM((HG_DK, HG_DK), F32)],
        compiler_params=_cparams("parallel", "arbitrary"), name="hgrn_fwd")(q, f, i, lb)


def _hgrn_bwd_call(q, f, i, lb, states, do):
    t = q.shape[0]
    nc = t // CHUNK
    blk = pl.BlockSpec((CHUNK, HG_DK), lambda h, c: (nc - 1 - c, h))
    row = pl.BlockSpec((1, HG_DK), lambda h, c: (0, h))

    def body(q_ref, f_ref, i_ref, lb_ref, s_ref, do_ref, dq_ref, df_ref, di_ref, dlb_ref, ds_scr):
        c = pl.program_id(1)

        @pl.when(c == 0)
        def _():
            ds_scr[...] = jnp.zeros_like(ds_scr)

        _, vjp = jax.vjp(_hgrn_chunk, s_ref[0, 0], q_ref[...], f_ref[...], i_ref[...], lb_ref[...])
        ds, dq, df, di, dlb = vjp((do_ref[...], ds_scr[...]))
        ds_scr[...] = ds
        dq_ref[...] = dq
        df_ref[...] = df
        di_ref[...] = di

        @pl.when(c == 0)
        def _():
            dlb_ref[...] = dlb

        @pl.when(c != 0)
        def _():
            dlb_ref[...] += dlb

    return pl.pallas_call(
        body, grid=(HG_HEADS, nc),
        in_specs=[blk, blk, blk, row, pl.BlockSpec((1, 1, HG_DK, HG_DK), lambda h, c: (h, nc - 1 - c, 0, 0)), blk],
        out_specs=[blk, blk, blk, row],
        out_shape=[jax.ShapeDtypeStruct((t, HG_WIDTH), F32)] * 3 + [jax.ShapeDtypeStruct((1, HG_WIDTH), F32)],
        scratch_shapes=[pltpu.VMEM((HG_DK, HG_DK), F32)],
        compiler_params=_cparams("parallel", "arbitrary"), name="hgrn_bwd")(q, f, i, lb, states, do)


@jax.custom_vjp
def hgrn_scan(q, f, i, lb):
    return _hgrn_fwd_call(q, f, i, lb)[0]


def _hgrn_vjp_fwd(q, f, i, lb):
    o, states = _hgrn_fwd_call(q, f, i, lb)
    return o, (q, f, i, lb, states)


def _hgrn_vjp_bwd(res, do):
    return tuple(_hgrn_bwd_call(*res, do))


hgrn_scan.defvjp(_hgrn_vjp_fwd, _hgrn_vjp_bwd)


def _ssd_chunk(state, xs, bm, cm, dtx, alog, dskip):
    assert CHUNK == SSM_HEADDIM and 2 * SSM_HEADDIM == LANES
    gw = xs.shape[1]
    trilf = _tril(CHUNK).astype(F32)
    da = dtx * (-jnp.exp(alog))
    a = jnp.dot(trilf, da, precision=HI)
    xdt = xs * dtx
    cb2 = _bdot(cm, jnp.concatenate([bm, bm], axis=0), NT)
    row = lax.broadcasted_iota(jnp.int32, (CHUNK, LANES), 0)
    src = lax.broadcasted_iota(jnp.int32, (CHUNK, LANES), 1) % CHUNK
    first_head = lax.broadcasted_iota(jnp.int32, (CHUNK, LANES), 1) < CHUNK
    ys = []
    for p in range(gw // LANES):
        lanes = slice(p * LANES, (p + 1) * LANES)
        a_src = jnp.sum(jnp.where(row <= src, da[:, lanes], 0.0), axis=0, keepdims=True)
        decay_ls = jnp.exp(jnp.where(src <= row, a[:, lanes] - a_src, NEG))
        x_pair = xdt[:, lanes]
        rhs = jnp.concatenate([jnp.where(first_head, x_pair, 0.0), jnp.where(first_head, 0.0, x_pair)], axis=0)
        ys.append(_bdot(cb2 * decay_ls, rhs, NN))
    y_diag = ys[0] if len(ys) == 1 else jnp.concatenate(ys, axis=1)
    y_off = jnp.exp(a) * _bdot(cm, state, NN)
    y = y_diag + y_off + xs * dskip
    a_last = a[CHUNK - 1:CHUNK, :]
    new_state = jnp.exp(a_last) * state + _bdot(bm, jnp.exp(a_last - a) * xdt, TN)
    return y, new_state


def _ssd_specs(inner, nc, rev):
    gw = inner // SSM_GROUPS
    nb = inner // SSM_STATE
    ci = (lambda c: nc - 1 - c) if rev else (lambda c: c)
    xs = pl.BlockSpec((CHUNK, gw), lambda g, c: (ci(c), g))
    bm = pl.BlockSpec((CHUNK, SSM_STATE), lambda g, c: (ci(c), nb + g))
    cm = pl.BlockSpec((CHUNK, SSM_STATE), lambda g, c: (ci(c), nb + SSM_GROUPS + g))
    row = pl.BlockSpec((1, gw), lambda g, c: (0, g))
    st = pl.BlockSpec((1, 1, SSM_STATE, gw), lambda g, c: (g, ci(c), 0, 0))
    return gw, xs, bm, cm, row, st


def _ssd_fwd_call(xbc, dtx, alog, dskip):
    t, inner = dtx.shape
    nc = t // CHUNK
    gw, xs_s, bm_s, cm_s, row, st = _ssd_specs(inner, nc, False)

    def body(xs_ref, bm_ref, cm_ref, dt_ref, a_ref, d_ref, y_ref, s_ref, s_scr):
        @pl.when(pl.program_id(1) == 0)
        def _():
            s_scr[...] = jnp.zeros_like(s_scr)

        s_ref[0, 0] = s_scr[...]
        y, ns = _ssd_chunk(s_scr[...], xs_ref[...], bm_ref[...], cm_ref[...], dt_ref[...], a_ref[...], d_ref[...])
        y_ref[...] = y
        s_scr[...] = ns

    return pl.pallas_call(
        body, grid=(SSM_GROUPS, nc), in_specs=[xs_s, bm_s, cm_s, xs_s, row, row],
        out_specs=[xs_s, st],
        out_shape=[jax.ShapeDtypeStruct((t, inner), F32), jax.ShapeDtypeStruct((SSM_GROUPS, nc, SSM_STATE, gw), F32)],
        scratch_shapes=[pltpu.VMEM((SSM_STATE, gw), F32)],
        compiler_params=_cparams("parallel", "arbitrary"), name="ssd_fwd")(xbc, xbc, xbc, dtx, alog, dskip)


def _ssd_bwd_call(xbc, dtx, alog, dskip, states, dy):
    t, inner = dtx.shape
    nc = t // CHUNK
    gw, xs_s, bm_s, cm_s, row, st = _ssd_specs(inner, nc, True)
    gn = pl.BlockSpec((CHUNK, SSM_STATE), lambda g, c: (nc - 1 - c, g))
    gn_shape = jax.ShapeDtypeStruct((t, SSM_GROUPS * SSM_STATE), F32)

    def body(xs_ref, bm_ref, cm_ref, dt_ref, a_ref, d_ref, s_ref, dy_ref,
             dxs_ref, dbm_ref, dcm_ref, ddt_ref, da_ref, dd_ref, ds_scr):
        c = pl.program_id(1)

        @pl.when(c == 0)
        def _():
            ds_scr[...] = jnp.zeros_like(ds_scr)

        _, vjp = jax.vjp(_ssd_chunk, s_ref[0, 0], xs_ref[...], bm_ref[...], cm_ref[...], dt_ref[...],
                         a_ref[...], d_ref[...])
        ds, dxs, dbm, dcm, ddt, da, dd = vjp((dy_ref[...], ds_scr[...]))
        ds_scr[...] = ds
        dxs_ref[...] = dxs
        dbm_ref[...] = dbm
        dcm_ref[...] = dcm
        ddt_ref[...] = ddt

        @pl.when(c == 0)
        def _():
            da_ref[...] = da
            dd_ref[...] = dd

        @pl.when(c != 0)
        def _():
            da_ref[...] += da
            dd_ref[...] += dd

    big = jax.ShapeDtypeStruct((t, inner), F32)
    small = jax.ShapeDtypeStruct((1, inner), F32)
    return pl.pallas_call(
        body, grid=(SSM_GROUPS, nc), in_specs=[xs_s, bm_s, cm_s, xs_s, row, row, st, xs_s],
        out_specs=[xs_s, gn, gn, xs_s, row, row],
        out_shape=[big, gn_shape, gn_shape, big, small, small],
        scratch_shapes=[pltpu.VMEM((SSM_STATE, gw), F32)],
        compiler_params=_cparams("parallel", "arbitrary"), name="ssd_bwd")(xbc, xbc, xbc, dtx, alog, dskip, states, dy)


@jax.custom_vjp
def ssd_scan(xbc, dtx, alog, dskip):
    return _ssd_fwd_call(xbc, dtx, alog, dskip)[0]


def _ssd_vjp_fwd(xbc, dtx, alog, dskip):
    y, states = _ssd_fwd_call(xbc, dtx, alog, dskip)
    return y, (xbc, dtx, alog, dskip, states)


def _ssd_vjp_bwd(res, dy):
    dxs, dbm, dcm, ddt, da, dd = _ssd_bwd_call(*res, dy)
    return jnp.concatenate([dxs, dbm, dcm], axis=1), ddt, da, dd


ssd_scan.defvjp(_ssd_vjp_fwd, _ssd_vjp_bwd)


ANY = pl.BlockSpec(memory_space=pl.ANY)


def _my_pos():
    return lax.axis_index("x"), lax.axis_index("y"), lax.axis_index("c")


def all_gather(x_shard):
    r, c_ = x_shard.shape

    def body(x_ref, out_ref, send_sems, recv_sems, local_sem):
        x, y, c = _my_pos()
        me, sibling = (x, y, c), (x, y, 1 - c)
        chips = [(1 - x, y), (x, 1 - y), (1 - x, 1 - y)]

        def slot(px, py, pc):
            return out_ref.at[4 * px + 2 * py + pc]

        def copy(k, block, to, src=None):
            return pltpu.make_async_remote_copy(
                src_ref=slot(*block) if src is None else src, dst_ref=slot(*block),
                send_sem=send_sems.at[k], recv_sem=recv_sems.at[k], device_id=to, device_id_type=MESH)

        mine = pltpu.make_async_copy(x_ref, slot(*me), local_sem)
        mine.start()
        first = [copy(0, me, sibling, src=x_ref)]
        first += [copy(1 + j, me, (*chip, c), src=x_ref) for j, chip in enumerate(chips)]
        for cp in first:
            cp.start()
        passed = [copy(4 + j, (*chip, c), sibling) for j, chip in enumerate(chips)]
        for j, chip in enumerate(chips):
            copy(1 + j, (*chip, c), me).wait_recv()
            passed[j].start()
        copy(0, sibling, me).wait_recv()
        for j, chip in enumerate(chips):
            copy(4 + j, (*chip, 1 - c), me).wait_recv()
        for cp in first + passed:
            cp.wait_send()
        mine.wait()

    return pl.pallas_call(
        body, out_shape=jax.ShapeDtypeStruct((N_DEV, r, c_), x_shard.dtype), in_specs=[ANY], out_specs=ANY,
        scratch_shapes=[pltpu.SemaphoreType.DMA((N_DEV - 1,)), pltpu.SemaphoreType.DMA((N_DEV - 1,)),
                        pltpu.SemaphoreType.DMA],
        name="all_gather")(x_shard)


def exchange(x):
    def body(x_ref, out_ref, send_sems, recv_sems, local_sem):
        x, y, c = _my_pos()
        me = 4 * x + 2 * y + c
        local = pltpu.make_async_copy(x_ref.at[me], out_ref.at[me], local_sem)
        local.start()
        copies = []
        for k in range(1, N_DEV):
            px = 1 - x if k & 4 else x
            py = 1 - y if k & 2 else y
            pc = 1 - c if k & 1 else c
            cp = pltpu.make_async_remote_copy(
                src_ref=x_ref.at[4 * px + 2 * py + pc], dst_ref=out_ref.at[me],
                send_sem=send_sems.at[k - 1], recv_sem=recv_sems.at[k - 1],
                device_id=(px, py, pc), device_id_type=MESH)
            cp.start()
            copies.append(cp)
        for cp in copies:
            cp.wait_recv()
        for cp in copies:
            cp.wait_send()
        local.wait()

    return pl.pallas_call(
        body, out_shape=jax.ShapeDtypeStruct(x.shape, x.dtype), in_specs=[ANY], out_specs=ANY,
        scratch_shapes=[pltpu.SemaphoreType.DMA((N_DEV - 1,)), pltpu.SemaphoreType.DMA((N_DEV - 1,)),
                        pltpu.SemaphoreType.DMA],
        name="exchange")(x)


def _in_chunks(fn, arr, axis):
    rows = arr.shape[axis]
    pieces = 1
    while (arr.size * arr.dtype.itemsize) // pieces > COMM_BYTES and rows % (2 * pieces) == 0:
        pieces *= 2
    step = rows // pieces
    outs = [fn(lax.slice_in_dim(arr, s, s + step, axis=axis)) for s in range(0, rows, step)]
    return outs[0] if len(outs) == 1 else jnp.concatenate(outs, axis=1)


def reduce_adamw(parts, w, m, v):
    p, r, c_ = parts.shape
    br = _row_block(r, [c_]) if r % 16 == 0 else r
    c1 = 1.0 - ADAM_B1 ** ADAM_STEP
    c2 = 1.0 - ADAM_B2 ** ADAM_STEP

    def body(p_ref, w_ref, m_ref, v_ref, g_ref, d_ref, m2_ref, v2_ref):
        g = p_ref[0].astype(F32)
        for i in range(1, p):
            g = g + p_ref[i].astype(F32)
        m2 = ADAM_B1 * m_ref[...] + (1.0 - ADAM_B1) * g
        v2 = ADAM_B2 * v_ref[...] + (1.0 - ADAM_B2) * (g * g)
        g_ref[...] = g
        m2_ref[...] = m2
        v2_ref[...] = v2
        d_ref[...] = -ADAM_LR * ((m2 / c1) / (jnp.sqrt(v2 / c2) + ADAM_EPS) + ADAM_WD * w_ref[...])

    blk = pl.BlockSpec((br, c_), lambda i: (i, 0))
    return pl.pallas_call(
        body, grid=(r // br,), in_specs=[pl.BlockSpec((p, br, c_), lambda i: (0, i, 0)), blk, blk, blk],
        out_specs=[blk] * 4, out_shape=[jax.ShapeDtypeStruct((r, c_), F32)] * 4,
        compiler_params=_cparams("parallel"), name="reduce_adamw")(parts, w, m, v)


WEIGHTS = ['ffn1_norm', 'ffn1_wi', 'ffn1_wo', 'mix_norm', 'w_in', 'mla_q_norm', 'mla_w_uq', 'mla_kv_norm',
           'mla_w_ukv', 'hgrn_lb_logits', 'hgrn_norm', 'ssm_conv_w', 'ssm_conv_b', 'ssm_a_log', 'ssm_dt_bias',
           'ssm_d', 'ssm_norm', 'w_o_mla', 'w_o_hgrn', 'w_o_ssm', 'w_out', 'ffn2_norm', 'ffn2_wi', 'ffn2_wo',
           'final_norm']
COL_SHARDED = ('ffn1_wi', 'w_in', 'mla_w_uq', 'mla_w_ukv', 'ffn2_wi')
ROW_SHARDED = ('ffn1_wo', 'w_o_mla', 'w_o_hgrn', 'w_o_ssm', 'w_out', 'ffn2_wo')
BIG = tuple(n for n in WEIGHTS if n in COL_SHARDED + ROW_SHARDED)
CONV_W = 'ssm_conv_w'
REPLICATED = tuple(n for n in WEIGHTS if n not in BIG and n != CONV_W)


def _segment_plan(n, sizes):
    plan, off = [], 0
    for s in sizes:
        a, b = off, off + s
        plan.append([(j, max(a, j * n) - j * n, min(b, (j + 1) * n) - j * n)
                     for j in range(a // n, (b - 1) // n + 1)])
        off = b
    return plan


@functools.partial(jax.custom_vjp, nondiff_argnums=(1,))
def col_segments(blocks, sizes):
    outs = []
    for pieces in _segment_plan(blocks.shape[-1], sizes):
        cut = [blocks[j][:, lo:hi] for j, lo, hi in pieces]
        outs.append(cut[0] if len(cut) == 1 else jnp.concatenate(cut, axis=1))
    return tuple(outs)


def _col_segments_fwd(blocks, sizes):
    return col_segments(blocks, sizes), blocks.shape[-1]


def _col_segments_bwd(sizes, n, gs):
    per_block = [[] for _ in range(N_DEV)]
    for g, pieces in zip(gs, _segment_plan(n, sizes)):
        off = 0
        for j, lo, hi in pieces:
            per_block[j].append(g[:, off:off + hi - lo])
            off += hi - lo
    return (jnp.stack([p[0] if len(p) == 1 else jnp.concatenate(p, axis=1) for p in per_block]),)


col_segments.defvjp(_col_segments_fwd, _col_segments_bwd)


@jax.custom_vjp
def layers_of(g4):
    return tuple(g4[:, layer] for layer in range(g4.shape[1]))


def _layers_of_fwd(g4):
    return layers_of(g4), None


def _layers_of_bwd(_, gs):
    return (jnp.stack(gs, axis=1),)


layers_of.defvjp(_layers_of_fwd, _layers_of_bwd)


def _rope_tables(t):
    half = MLA_ROPE // 2
    inv = 1.0 / (ROPE_THETA ** (jnp.arange(0, MLA_ROPE, 2, dtype=F32) / MLA_ROPE))
    ang = jnp.arange(t, dtype=F32)[:, None] * inv[None, :]
    reps = LANES // half
    return jnp.tile(jnp.cos(ang), (1, reps)), jnp.tile(jnp.sin(ang), (1, reps))


def _ffn(x, norm, wi, wo):
    dff = wo.shape[0]
    h = rmsnorm(norm[None], x)[0]
    wg, wu = col_segments(wi, (dff, dff))
    return x + 0.5 * mm(swiglu(mm(h, wg), mm(h, wu))[0], wo)


def _per_head(w, widths, pad_to):
    k = w.shape[0]
    w3 = w.reshape(k, -1, sum(widths))
    outs, off = [], 0
    for wd in widths:
        part = w3[:, :, off:off + wd]
        if wd < pad_to:
            part = jnp.pad(part, ((0, 0), (0, 0), (0, pad_to - wd)))
        outs.append(part.reshape(k, -1))
        off += wd
    return outs


def _layer(x, p, lb, cos, sin):
    t, d = x.shape
    inner = 2 * d
    conv_dim = inner + 2 * SSM_GROUPS * SSM_STATE
    n_ssm_heads = inner // SSM_HEADDIM
    x = _ffn(x, p['ffn1_norm'], p['ffn1_wi'], p['ffn1_wo'])

    h = rmsnorm(p['mix_norm'][None], x)[0]
    sizes = (MLA_Q_RANK, MLA_KV_RANK, MLA_ROPE, HG_WIDTH, HG_WIDTH, HG_WIDTH, HG_WIDTH,
             inner, conv_dim, n_ssm_heads, d, d, d)
    (w_q, w_kv, w_kpe, w_hq, w_hf, w_hi, w_hg, w_z, w_xbc, w_dt, w_ga, w_gb, w_gc) = col_segments(p['w_in'], sizes)

    qn = rmsnorm(p['mla_q_norm'][None], mm(h, w_q))[0]
    kvn = rmsnorm(p['mla_kv_norm'][None], mm(h, w_kv))[0]
    w_uq, = col_segments(p['mla_w_uq'], (N_DEV * p['mla_w_uq'].shape[-1],))
    w_ukv, = col_segments(p['mla_w_ukv'], (N_DEV * p['mla_w_ukv'].shape[-1],))
    wq_nope, wq_pe = _per_head(w_uq, (MLA_NOPE, MLA_ROPE), LANES)
    wk_nope, wv = _per_head(w_ukv, (MLA_NOPE, MLA_V), LANES)
    q_nope = mm(qn, wq_nope)
    q_pe = rope(mm(qn, wq_pe), cos, sin)[0]
    k_nope = mm(kvn, wk_nope)
    v = mm(kvn, wv)
    k_rot = rope(mm(h, jnp.pad(w_kpe, ((0, 0), (0, LANES - MLA_ROPE)))), cos, sin)[0]
    q = jnp.concatenate([q_nope.reshape(t, MLA_HEADS, LANES), q_pe.reshape(t, MLA_HEADS, LANES)], axis=2)
    k = jnp.concatenate([k_nope.reshape(t, MLA_HEADS, LANES),
                         jnp.broadcast_to(k_rot[:, None, :], (t, MLA_HEADS, LANES))], axis=2)
    o = attention(q.reshape(t, -1).astype(BF16), k.reshape(t, -1).astype(BF16), v.astype(BF16))
    y_a = mm(o, p['w_o_mla'])

    o = hgrn_scan(mm(h, w_hq), mm(h, w_hf), mm(h, w_hi), lb[None])
    o = hgrn_out(p['hgrn_norm'][None], o, mm(h, w_hg))[0]
    y_b = mm(o, p['w_o_hgrn'])

    xbc = silu_op(conv(p['ssm_conv_w'], p['ssm_conv_b'][None], mm(h, w_xbc)))[0]
    dtx = dt_expand(p['ssm_dt_bias'][None], mm(h, w_dt))[0]
    y = ssd_scan(xbc, dtx, jnp.repeat(p['ssm_a_log'], SSM_HEADDIM)[None], jnp.repeat(p['ssm_d'], SSM_HEADDIM)[None])
    y = ssm_norm(p['ssm_norm'][None], y, mm(h, w_z))[0]
    y_c = mm(y, p['w_o_ssm'])

    merged = merge(y_a, y_b, y_c, mm(h, w_ga), mm(h, w_gb), mm(h, w_gc))[0]
    x = x + mm(merged, p['w_out'])
    return _ffn(x, p['ffn2_norm'], p['ffn2_wi'], p['ffn2_wo'])


def _local_loss(params, x, target):
    depth = params['ffn1_norm'].shape[0]
    cos, sin = _rope_tables(x.shape[0])
    prob = jax.nn.softmax(params['hgrn_lb_logits'], axis=0)
    lower = jnp.cumsum(prob, axis=0) - prob[0:1]
    per_layer = {n: layers_of(params[n]) for n in BIG}
    for layer in range(depth):
        p = {}
        for n in WEIGHTS:
            if n in ROW_SHARDED:
                p[n] = per_layer[n][layer].reshape(-1, params[n].shape[-1])
            elif n in COL_SHARDED:
                p[n] = per_layer[n][layer]
            elif n != 'final_norm':
                p[n] = params[n][layer]
        x = _layer(x, p, lower[layer], cos, sin)
    return jnp.sum(loss_rows(params['final_norm'][None], x, target)[0])


def _pack_vec(arrays, rows):
    flat = jnp.concatenate([a.reshape(-1) for a in arrays])
    return jnp.pad(flat, (0, rows * COMM_COLS - flat.shape[0])).reshape(rows, COMM_COLS)


def _unpack(flat, shapes, lead=()):
    flat = flat.reshape(lead + (-1,))
    outs, off = [], 0
    for s in shapes:
        n = 1
        for dim in s:
            n *= dim
        outs.append(flat[..., off:off + n].reshape(lead + tuple(s)))
        off += n
    return outs


def _round_up(n, m):
    return -(-n // m) * m


def _step(a):
    x = a['x'][0]
    target = a['loss_target'][0]
    me = 4 * lax.axis_index("x") + 2 * lax.axis_index("y") + lax.axis_index("c")

    conv_shape = a[CONV_W].shape
    conv_full_shape = conv_shape[:-1] + (conv_shape[-1] * N_DEV,)
    rep_shapes = [a[n].shape for n in REPLICATED]
    n_small = 1 + sum(a[n].size for n in REPLICATED) + a[CONV_W].size * N_DEV
    small_rows = _round_up(-(-n_small // COMM_COLS), 8)

    params = {}
    for n in BIG:
        rows2d = a[n].astype(BF16).reshape(-1, a[n].shape[-1])
        params[n] = _in_chunks(all_gather, rows2d, 0).reshape((N_DEV,) + a[n].shape)
    conv_blocks = _unpack(all_gather(_pack_vec([a[CONV_W]], small_rows)), [conv_shape], lead=(N_DEV,))[0]
    params[CONV_W] = jnp.moveaxis(conv_blocks, 0, -2).reshape(conv_full_shape)
    for n in REPLICATED:
        params[n] = a[n]

    loss, (gp, gx) = jax.value_and_grad(_local_loss, argnums=(0, 1))(params, x, target)

    big_out = [{}, {}, {}, {}]
    for n in BIG:
        width = a[n].shape[-1]
        parts = _in_chunks(exchange, gp[n].reshape(N_DEV, -1, width), 1)
        res = reduce_adamw(parts, *[a[pre + n].reshape(-1, width) for pre in ('', 'm_', 'v_')])
        for kind in range(4):
            big_out[kind][n] = res[kind].reshape(a[n].shape)

    small = _pack_vec([loss.reshape(1)] + [gp[n] for n in REPLICATED] + [gp[CONV_W]], small_rows)
    zero1, one1 = jnp.zeros((1,), F32), jnp.ones((1,), F32)
    zero_c, one_c = jnp.zeros(conv_full_shape, F32), jnp.ones(conv_full_shape, F32)
    small_w = _pack_vec([zero1] + [a[n] for n in REPLICATED] + [zero_c], small_rows)
    small_m = _pack_vec([zero1] + [a['m_' + n] for n in REPLICATED] + [zero_c], small_rows)
    small_v = _pack_vec([one1] + [a['v_' + n] for n in REPLICATED] + [one_c], small_rows)
    res = reduce_adamw(all_gather(small), small_w, small_m, small_v)
    small_out = []
    for r in res:
        pieces = _unpack(r, [(1,)] + rep_shapes + [conv_full_shape])
        small_out.append((pieces[0], dict(zip(REPLICATED, pieces[1:-1])), pieces[-1]))
    total_loss = small_out[0][0][0]

    width = conv_shape[-1]
    g_conv = lax.dynamic_slice_in_dim(small_out[0][2], me * width, width, axis=len(conv_shape) - 1)
    conv_rows = -(-a[CONV_W].size // COMM_COLS)
    conv_res = reduce_adamw(_pack_vec([g_conv], conv_rows)[None],
                            *[_pack_vec([a[pre + CONV_W]], conv_rows) for pre in ('', 'm_', 'v_')])
    conv_out = [_unpack(r, [conv_shape])[0] for r in conv_res]

    outs = [total_loss, gx[None]]
    for kind in range(4):
        for n in WEIGHTS:
            if n in BIG:
                outs.append(big_out[kind][n])
            elif n == CONV_W:
                outs.append(conv_out[kind])
            else:
                outs.append(small_out[kind][1][n])
    return tuple(outs)


def kernel(x, ffn1_norm, ffn1_wi, ffn1_wo, mix_norm, w_in, mla_q_norm, mla_w_uq, mla_kv_norm, mla_w_ukv, hgrn_lb_logits, hgrn_norm, ssm_conv_w, ssm_conv_b, ssm_a_log, ssm_dt_bias, ssm_d, ssm_norm, w_o_mla, w_o_hgrn, w_o_ssm, w_out, ffn2_norm, ffn2_wi, ffn2_wo, final_norm, loss_target, m_ffn1_norm, m_ffn1_wi, m_ffn1_wo, m_mix_norm, m_w_in, m_mla_q_norm, m_mla_w_uq, m_mla_kv_norm, m_mla_w_ukv, m_hgrn_lb_logits, m_hgrn_norm, m_ssm_conv_w, m_ssm_conv_b, m_ssm_a_log, m_ssm_dt_bias, m_ssm_d, m_ssm_norm, m_w_o_mla, m_w_o_hgrn, m_w_o_ssm, m_w_out, m_ffn2_norm, m_ffn2_wi, m_ffn2_wo, m_final_norm, v_ffn1_norm, v_ffn1_wi, v_ffn1_wo, v_mix_norm, v_w_in, v_mla_q_norm, v_mla_w_uq, v_mla_kv_norm, v_mla_w_ukv, v_hgrn_lb_logits, v_hgrn_norm, v_ssm_conv_w, v_ssm_conv_b, v_ssm_a_log, v_ssm_dt_bias, v_ssm_d, v_ssm_norm, v_w_o_mla, v_w_o_hgrn, v_w_o_ssm, v_w_out, v_ffn2_norm, v_ffn2_wi, v_ffn2_wo, v_final_norm):
    return _step(dict(locals()))
```

```python
import functools

import jax
import jax.numpy as jnp
from jax import lax
from jax.experimental import pallas as pl
from jax.experimental.pallas import tpu as pltpu

F32 = jnp.float32
BF16 = jnp.bfloat16
HI = lax.Precision.HIGHEST
MESH = pl.DeviceIdType.MESH

EPS = 1e-6
CHUNK = 64
N_DEV = 8

MLA_HEADS = 16
MLA_Q_RANK = 512
MLA_KV_RANK = 512
MLA_NOPE = 128
MLA_ROPE = 64
MLA_V = 128
ROPE_THETA = 10000.0
HG_HEADS = 16
HG_DK = 128
HG_WIDTH = HG_HEADS * HG_DK
SSM_HEADDIM = 64
SSM_GROUPS = 8
SSM_STATE = 128
SSM_CONV = 4

ADAM_LR = 0.001
ADAM_B1 = 0.9
ADAM_B2 = 0.999
ADAM_EPS = 1e-08
ADAM_WD = 0.01
ADAM_STEP = 10

LANES = 128
VMEM_LIMIT = 48 * 1024 * 1024
ROW_BLOCK_ELEMS = 128 * 1024
COMM_COLS = 1024
COMM_BYTES = 128 * 1024 * 1024


def _cparams(*sem):
    return pltpu.CompilerParams(dimension_semantics=sem, vmem_limit_bytes=VMEM_LIMIT)


def _tile(dim, pref):
    t = pref
    while t >= LANES:
        if dim % t == 0:
            return t
        t //= 2
    return dim


def _mm_call(a, b, mode, out_dtype, name):
    if mode == "nn":
        (m, k), (k2, n) = a.shape, b.shape
    elif mode == "nt":
        (m, k), (n, k2) = a.shape, b.shape
    else:
        (k, m), (k2, n) = a.shape, b.shape
    assert k == k2, (a.shape, b.shape, mode)
    tm, tn, tk = _tile(m, 1024), _tile(n, 1024), _tile(k, 1024)
    nk = k // tk
    if mode == "nn":
        a_spec = pl.BlockSpec((tm, tk), lambda i, j, kk: (i, kk))
        b_spec = pl.BlockSpec((tk, tn), lambda i, j, kk: (kk, j))
        dims = (((1,), (0,)), ((), ()))
    elif mode == "nt":
        a_spec = pl.BlockSpec((tm, tk), lambda i, j, kk: (i, kk))
        b_spec = pl.BlockSpec((tn, tk), lambda i, j, kk: (j, kk))
        dims = (((1,), (1,)), ((), ()))
    else:
        a_spec = pl.BlockSpec((tk, tm), lambda i, j, kk: (kk, i))
        b_spec = pl.BlockSpec((tk, tn), lambda i, j, kk: (kk, j))
        dims = (((0,), (0,)), ((), ()))

    def body(a_ref, b_ref, o_ref, acc_ref):
        kk = pl.program_id(2)

        @pl.when(kk == 0)
        def _():
            acc_ref[...] = jnp.zeros_like(acc_ref)

        acc_ref[...] += lax.dot_general(a_ref[...].astype(BF16), b_ref[...].astype(BF16), dims,
                                        preferred_element_type=F32)

        @pl.when(kk == nk - 1)
        def _():
            o_ref[...] = acc_ref[...].astype(o_ref.dtype)

    return pl.pallas_call(
        body, grid=(m // tm, n // tn, nk), in_specs=[a_spec, b_spec],
        out_specs=pl.BlockSpec((tm, tn), lambda i, j, kk: (i, j)),
        out_shape=jax.ShapeDtypeStruct((m, n), out_dtype),
        scratch_shapes=[pltpu.VMEM((tm, tn), F32)],
        compiler_params=_cparams("parallel", "parallel", "arbitrary"), name=name)(a, b)


@jax.custom_vjp
def mm(a, w):
    return _mm_call(a.astype(BF16), w, "nn", F32, "mm_fwd")


def _mm_fwd(a, w):
    a16 = a.astype(BF16)
    return _mm_call(a16, w, "nn", F32, "mm_fwd"), (a16, w)


def _mm_bwd(res, g):
    a16, w = res
    g16 = g.astype(BF16)
    return _mm_call(g16, w, "nt", F32, "mm_da"), _mm_call(a16, g16, "tn", w.dtype, "mm_dw")


mm.defvjp(_mm_fwd, _mm_bwd)


def _row_block(t, widths):
    bt = 8
    while 2 * bt * max(widths) <= ROW_BLOCK_ELEMS:
        bt *= 2
    while t % bt:
        bt //= 2
    return bt


def make_rowwise(fn, name, n_par, group_width=None, shared=(), nondiff=()):
    def specs(args):
        t = max(a.shape[0] for a in args)
        cut = next(a for i, a in enumerate(args) if i >= n_par and i not in shared)
        gw = group_width(cut.shape[1]) if callable(group_width) else group_width
        groups = cut.shape[1] // gw if gw else 1
        ws = [a.shape[1] if i in shared else a.shape[1] // groups for i, a in enumerate(args)]
        ows = out_widths(ws)
        bt = _row_block(t, ws + ows)
        sp = []
        for i, a in enumerate(args):
            col = (lambda g: 0) if i in shared else (lambda g: g)
            if i < n_par:
                sp.append(pl.BlockSpec((1, ws[i]), lambda g, r, col=col: (0, col(g))))
            else:
                sp.append(pl.BlockSpec((bt, ws[i]), lambda g, r, col=col: (r, col(g))))
        return t, bt, groups, ows, sp

    def out_widths(ws):
        blocks = [jax.ShapeDtypeStruct((1 if i < n_par else 8, w), F32) for i, w in enumerate(ws)]
        return [o.shape[1] for o in jax.eval_shape(fn, *blocks)]

    def fwd_call(*args):
        t, bt, groups, ows, in_specs = specs(args)
        n_in = len(args)

        def body(*refs):
            outs = fn(*[r[...] for r in refs[:n_in]])
            for r, o in zip(refs[n_in:], outs):
                r[...] = o

        return pl.pallas_call(
            body, grid=(groups, t // bt), in_specs=in_specs,
            out_specs=[pl.BlockSpec((bt, w), lambda g, r: (r, g)) for w in ows],
            out_shape=[jax.ShapeDtypeStruct((t, w * groups), F32) for w in ows],
            compiler_params=_cparams("parallel", "parallel"), name=name + "_fwd")(*args)

    def bwd_call(args, gs):
        t, bt, groups, ows, in_specs = specs(args)
        n_in, n_out = len(args), len(gs)
        diff = [i for i in range(n_in) if i not in nondiff]
        g_specs = [pl.BlockSpec((bt, w), lambda g, r: (r, g)) for w in ows]
        o_specs, o_shapes = [], []
        for i in diff:
            o_specs.append(in_specs[i])
            o_shapes.append(jax.ShapeDtypeStruct(args[i].shape, F32))

        def body(*refs):
            r_idx = pl.program_id(1)
            vals = [r[...] for r in refs[:n_in]]
            cts = tuple(r[...] for r in refs[n_in:n_in + n_out])

            def f_diff(*dv):
                full = list(vals)
                for i, v in zip(diff, dv):
                    full[i] = v
                return tuple(fn(*full))

            _, vjp = jax.vjp(f_diff, *[vals[i] for i in diff])
            grads = vjp(cts)
            for i, g_val, ref in zip(diff, grads, refs[n_in + n_out:]):
                if i < n_par:
                    @pl.when(r_idx == 0)
                    def _(ref=ref, g_val=g_val):
                        ref[...] = g_val

                    @pl.when(r_idx != 0)
                    def _(ref=ref, g_val=g_val):
                        ref[...] += g_val
                else:
                    ref[...] = g_val

        outs = pl.pallas_call(
            body, grid=(groups, t // bt), in_specs=in_specs + g_specs, out_specs=o_specs, out_shape=o_shapes,
            compiler_params=_cparams("parallel", "arbitrary"), name=name + "_bwd")(*args, *gs)
        full = [jnp.zeros_like(a) for a in args]
        for i, o in zip(diff, outs):
            full[i] = o
        return tuple(full)

    @jax.custom_vjp
    def op(*args):
        return tuple(fwd_call(*args))

    def op_fwd(*args):
        return tuple(fwd_call(*args)), args

    def op_bwd(args, gs):
        return bwd_call(args, gs)

    op.defvjp(op_fwd, op_bwd)
    return op


def _silu(x):
    return x * jax.nn.sigmoid(x)


def _rmsnorm_fn(w, x):
    return (x * lax.rsqrt(jnp.mean(x * x, axis=-1, keepdims=True) + EPS) * w,)


def _swiglu_fn(g, u):
    return (_silu(g) * u,)


def _silu_fn(x):
    return (_silu(x),)


def _rope_fn(x, cos, sin):
    i = lax.broadcasted_iota(jnp.int32, (LANES, LANES), 0)
    j = lax.broadcasted_iota(jnp.int32, (LANES, LANES), 1)
    half = MLA_ROPE // 2
    first = (j % MLA_ROPE) < half
    p = jnp.where(first & (i == j + half), -1.0, 0.0) + jnp.where((~first) & (i == j - half), 1.0, 0.0)
    return (x * cos + jnp.dot(x, p.astype(F32), precision=HI) * sin,)


def _hgrn_out_fn(w, o, g):
    return (o * lax.rsqrt(jnp.mean(o * o, axis=-1, keepdims=True) + EPS) * w * _silu(g),)


def _softplus(x):
    return jnp.maximum(x, 0.0) + jnp.log(1.0 + jnp.exp(-jnp.abs(x)))


def _dt_expand_fn(bias, dt_raw):
    nh = dt_raw.shape[1]
    h = lax.broadcasted_iota(jnp.int32, (nh, nh * SSM_HEADDIM), 0)
    c = lax.broadcasted_iota(jnp.int32, (nh, nh * SSM_HEADDIM), 1)
    e = (c // SSM_HEADDIM == h).astype(F32)
    return (jnp.dot(_softplus(dt_raw + bias), e, precision=HI),)


def _ssm_norm_fn(w, y, z):
    y = y * _silu(z)
    return (y * lax.rsqrt(jnp.mean(y * y, axis=-1, keepdims=True) + EPS) * w,)


def _merge_fn(ya, yb, yc, ga, gb, gc):
    return (jax.nn.sigmoid(ga) * ya + jax.nn.sigmoid(gb) * yb + jax.nn.sigmoid(gc) * yc,)


def _loss_fn(w, x, tgt):
    y = x * lax.rsqrt(jnp.mean(x * x, axis=-1, keepdims=True) + EPS) * w
    err = y - tgt
    return (0.5 * jnp.mean(err * err, axis=-1, keepdims=True),)


rmsnorm = make_rowwise(_rmsnorm_fn, "rmsnorm", 1)
swiglu = make_rowwise(_swiglu_fn, "swiglu", 0, group_width=512)
silu_op = make_rowwise(_silu_fn, "silu", 0, group_width=512)
rope = make_rowwise(_rope_fn, "rope", 0, group_width=LANES, shared=(1, 2), nondiff=(1, 2))
hgrn_out = make_rowwise(_hgrn_out_fn, "hgrn_out", 1, group_width=HG_DK)
dt_expand = make_rowwise(_dt_expand_fn, "dt_expand", 1)
ssm_norm = make_rowwise(_ssm_norm_fn, "ssm_norm", 1, group_width=lambda w: w // SSM_GROUPS)
merge = make_rowwise(_merge_fn, "merge", 0, group_width=512)
loss_rows = make_rowwise(_loss_fn, "loss", 1, nondiff=(2,))


HALO = 8


def _conv_blocks(t, c):
    cw = _tile(c, 512)
    bt = _row_block(t, [cw])
    return cw, bt


def _conv_fwd_call(w, b, x):
    t, c = x.shape
    cw, bt = _conv_blocks(t, c)
    hb = bt // HALO

    def body(w_ref, b_ref, x_ref, prev_ref, y_ref):
        r = pl.program_id(1)
        prev = jnp.where(r == 0, 0.0, prev_ref[...])
        xx = jnp.concatenate([prev, x_ref[...]], axis=0)
        acc = jnp.zeros((bt, cw), F32) + b_ref[...]
        for k in range(SSM_CONV):
            sh = SSM_CONV - 1 - k
            xs = xx if sh == 0 else pltpu.roll(xx, sh, axis=0)
            acc = acc + w_ref[k:k + 1, :] * xs[HALO:, :]
        y_ref[...] = acc

    return pl.pallas_call(
        body, grid=(c // cw, t // bt),
        in_specs=[pl.BlockSpec((SSM_CONV, cw), lambda g, r: (0, g)), pl.BlockSpec((1, cw), lambda g, r: (0, g)),
                  pl.BlockSpec((bt, cw), lambda g, r: (r, g)),
                  pl.BlockSpec((HALO, cw), lambda g, r: (jnp.maximum(r * hb - 1, 0), g))],
        out_specs=pl.BlockSpec((bt, cw), lambda g, r: (r, g)),
        out_shape=jax.ShapeDtypeStruct((t, c), F32),
        compiler_params=_cparams("parallel", "parallel"), name="conv_fwd")(w, b, x, x)


def _conv_bwd_call(w, x, dy):
    t, c = x.shape
    cw, bt = _conv_blocks(t, c)
    hb = bt // HALO
    nr = t // bt

    def body(w_ref, x_ref, prev_ref, dy_ref, next_ref, dx_ref, dw_ref, db_ref):
        r = pl.program_id(1)
        prev = jnp.where(r == 0, 0.0, prev_ref[...])
        nxt = jnp.where(r == nr - 1, 0.0, next_ref[...])
        xx = jnp.concatenate([prev, x_ref[...]], axis=0)
        dd = jnp.concatenate([dy_ref[...], nxt], axis=0)
        dy_val = dy_ref[...]
        dx = jnp.zeros((bt, cw), F32)
        dws = []
        for k in range(SSM_CONV):
            sh = SSM_CONV - 1 - k
            xs = xx if sh == 0 else pltpu.roll(xx, sh, axis=0)
            ds = dd if sh == 0 else pltpu.roll(dd, bt + HALO - sh, axis=0)
            dx = dx + w_ref[k:k + 1, :] * ds[:bt, :]
            dws.append(jnp.sum(dy_val * xs[HALO:, :], axis=0, keepdims=True))
        dx_ref[...] = dx
        dw = jnp.concatenate(dws, axis=0)
        db = jnp.sum(dy_val, axis=0, keepdims=True)

        @pl.when(r == 0)
        def _():
            dw_ref[...] = dw
            db_ref[...] = db

        @pl.when(r != 0)
        def _():
            dw_ref[...] += dw
            db_ref[...] += db

    return pl.pallas_call(
        body, grid=(c // cw, nr),
        in_specs=[pl.BlockSpec((SSM_CONV, cw), lambda g, r: (0, g)),
                  pl.BlockSpec((bt, cw), lambda g, r: (r, g)),
                  pl.BlockSpec((HALO, cw), lambda g, r: (jnp.maximum(r * hb - 1, 0), g)),
                  pl.BlockSpec((bt, cw), lambda g, r: (r, g)),
                  pl.BlockSpec((HALO, cw), lambda g, r: (jnp.minimum((r + 1) * hb, nr * hb - 1), g))],
        out_specs=[pl.BlockSpec((bt, cw), lambda g, r: (r, g)),
                   pl.BlockSpec((SSM_CONV, cw), lambda g, r: (0, g)), pl.BlockSpec((1, cw), lambda g, r: (0, g))],
        out_shape=[jax.ShapeDtypeStruct((t, c), F32), jax.ShapeDtypeStruct((SSM_CONV, c), F32),
                   jax.ShapeDtypeStruct((1, c), F32)],
        compiler_params=_cparams("parallel", "arbitrary"), name="conv_bwd")(w, x, x, dy, dy)


@jax.custom_vjp
def conv(w, b, x):
    return _conv_fwd_call(w, b, x)


def _conv_vjp_fwd(w, b, x):
    return _conv_fwd_call(w, b, x), (w, x)


def _conv_vjp_bwd(res, dy):
    w, x = res
    dx, dw, db = _conv_bwd_call(w, x, dy)
    return dw, db, dx


conv.defvjp(_conv_vjp_fwd, _conv_vjp_bwd)


ATT_DQK = 2 * LANES
ATT_SCALE = (MLA_NOPE + MLA_ROPE) ** -0.5
NEG = -1e30


def _att_tiles(t):
    tq = min(512, max(CHUNK, t // 4))
    tk = min(2 * tq, t)
    return tq, tk, tk // tq


def _att_mask(s, i, j, tq, tk):
    row = (i * tq + lax.broadcasted_iota(jnp.int32, (tq, tk), 0)) // CHUNK
    col = (j * tk + lax.broadcasted_iota(jnp.int32, (tq, tk), 1)) // CHUNK
    return jnp.where(col <= row, s, NEG)


def _grid_marks(n0, n1, n2):
    a, b, c = pl.program_id(0), pl.program_id(1), pl.program_id(2)
    inner0 = (b == 0) & (c == 0)
    return (a == 0) & inner0, (a == n0 // 2) & inner0, (a == n0 - 1) & (b == n1 - 1) & (c == n2 - 1)


def _att_fwd_call(q, k, v, shards=()):
    t = q.shape[0]
    h = q.shape[1] // ATT_DQK
    dv = v.shape[1] // h
    tq, tk, ratio = _att_tiles(t)
    nq, nk = t // tq, t // tk
    dims_nt = (((1,), (1,)), ((), ()))
    n_c = len(shards)

    def body(q_ref, k_ref, v_ref, *rest):
        x_refs, rest = rest[:n_c], rest[n_c:]
        o_ref, lse_ref = rest[:2]
        g_refs, rest = rest[2:2 + n_c], rest[2 + n_c:]
        m_scr, l_scr, acc_scr = rest[:3]
        i, j = pl.program_id(1), pl.program_id(2)
        if n_c:
            start, middle, finish = _gather_phases(x_refs, g_refs, *rest[3:])
            first, mid, last = _grid_marks(h, nq, nk)
            pl.when(first)(start)
            pl.when(mid)(middle)

        @pl.when(j == 0)
        def _():
            m_scr[...] = jnp.full_like(m_scr, NEG)
            l_scr[...] = jnp.zeros_like(l_scr)
            acc_scr[...] = jnp.zeros_like(acc_scr)

        def step(masked):
            s = lax.dot_general(q_ref[...], k_ref[...], dims_nt, preferred_element_type=F32) * ATT_SCALE
            if masked:
                s = _att_mask(s, i, j, tq, tk)
            m_new = jnp.maximum(m_scr[...], jnp.max(s, axis=-1, keepdims=True))
            alpha = jnp.exp(m_scr[...] - m_new)
            p = jnp.exp(s - m_new)
            l_scr[...] = alpha * l_scr[...] + jnp.sum(p, axis=-1, keepdims=True)
            acc_scr[...] = alpha * acc_scr[...] + jnp.dot(p.astype(BF16), v_ref[...], preferred_element_type=F32)
            m_scr[...] = m_new

        pl.when(j < i // ratio)(functools.partial(step, False))

        @pl.when(j == i // ratio)
        def _():
            step(True)
            o_ref[...] = acc_scr[...] / l_scr[...]
            lse_ref[0] = m_scr[...] + jnp.log(l_scr[...])

        if n_c:
            pl.when(last)(finish)

    outs = pl.pallas_call(
        body, grid=(h, nq, nk),
        in_specs=[pl.BlockSpec((tq, ATT_DQK), lambda hh, i, j: (i, hh)),
                  pl.BlockSpec((tk, ATT_DQK), lambda hh, i, j: (jnp.minimum(j, i // ratio), hh)),
                  pl.BlockSpec((tk, dv), lambda hh, i, j: (jnp.minimum(j, i // ratio), hh))] + [ANY] * n_c,
        out_specs=[pl.BlockSpec((tq, dv), lambda hh, i, j: (i, hh)),
                   pl.BlockSpec((1, tq, 1), lambda hh, i, j: (hh, i, 0))] + [ANY] * n_c,
        out_shape=[jax.ShapeDtypeStruct((t, h * dv), F32), jax.ShapeDtypeStruct((h, t, 1), F32)]
        + [jax.ShapeDtypeStruct((N_DEV,) + s.shape, s.dtype) for s in shards],
        scratch_shapes=[pltpu.VMEM((tq, 1), F32), pltpu.VMEM((tq, 1), F32), pltpu.VMEM((tq, dv), F32)]
        + (_comm_scratch(n_c) if n_c else []),
        compiler_params=_cparams("arbitrary", "arbitrary", "arbitrary"), name="att_fwd")(q, k, v, *shards)
    return outs[0], outs[1], tuple(outs[2:])


def _att_dq_call(q, k, v, o, lse, do, grads=()):
    t = q.shape[0]
    h = q.shape[1] // ATT_DQK
    dv = v.shape[1] // h
    tq, tk, ratio = _att_tiles(t)
    nq, nk = t // tq, t // tk
    dims_nt = (((1,), (1,)), ((), ()))
    n_c = len(grads)

    def body(q_ref, k_ref, v_ref, o_ref, lse_ref, do_ref, *rest):
        x_refs, rest = rest[:n_c], rest[n_c:]
        dq_ref, delta_ref = rest[:2]
        p_refs, rest = rest[2:2 + n_c], rest[2 + n_c:]
        acc_scr, d_scr = rest[:2]
        i, j = pl.program_id(1), pl.program_id(2)
        if n_c:
            start, finish = _exchange_phases(x_refs, p_refs, *rest[2:])
            first, _, last = _grid_marks(h, nq, nk)
            pl.when(first)(start)

        @pl.when(j == 0)
        def _():
            acc_scr[...] = jnp.zeros_like(acc_scr)
            d_scr[...] = jnp.sum(do_ref[...] * o_ref[...], axis=-1, keepdims=True)

        def step(masked):
            s = lax.dot_general(q_ref[...], k_ref[...], dims_nt, preferred_element_type=F32) * ATT_SCALE
            if masked:
                s = _att_mask(s, i, j, tq, tk)
            p = jnp.exp(s - lse_ref[0])
            dp = lax.dot_general(do_ref[...].astype(BF16), v_ref[...], dims_nt, preferred_element_type=F32)
            ds = p * (dp - d_scr[...]) * ATT_SCALE
            acc_scr[...] += jnp.dot(ds.astype(BF16), k_ref[...], preferred_element_type=F32)

        pl.when(j < i // ratio)(functools.partial(step, False))

        @pl.when(j == i // ratio)
        def _():
            step(True)
            dq_ref[...] = acc_scr[...].astype(dq_ref.dtype)
            delta_ref[0] = d_scr[...]

        if n_c:
            pl.when(last)(finish)

    outs = pl.pallas_call(
        body, grid=(h, nq, nk),
        in_specs=[pl.BlockSpec((tq, ATT_DQK), lambda hh, i, j: (i, hh)),
                  pl.BlockSpec((tk, ATT_DQK), lambda hh, i, j: (jnp.minimum(j, i // ratio), hh)),
                  pl.BlockSpec((tk, dv), lambda hh, i, j: (jnp.minimum(j, i // ratio), hh)),
                  pl.BlockSpec((tq, dv), lambda hh, i, j: (i, hh)),
                  pl.BlockSpec((1, tq, 1), lambda hh, i, j: (hh, i, 0)),
                  pl.BlockSpec((tq, dv), lambda hh, i, j: (i, hh))] + [ANY] * n_c,
        out_specs=[pl.BlockSpec((tq, ATT_DQK), lambda hh, i, j: (i, hh)),
                   pl.BlockSpec((1, tq, 1), lambda hh, i, j: (hh, i, 0))] + [ANY] * n_c,
        out_shape=[jax.ShapeDtypeStruct(q.shape, q.dtype), jax.ShapeDtypeStruct((h, t, 1), F32)]
        + [jax.ShapeDtypeStruct(g.shape, g.dtype) for g in grads],
        scratch_shapes=[pltpu.VMEM((tq, ATT_DQK), F32), pltpu.VMEM((tq, 1), F32)]
        + (_comm_scratch(n_c) if n_c else []),
        compiler_params=_cparams("arbitrary", "arbitrary", "arbitrary"), name="att_dq")(q, k, v, o, lse, do, *grads)
    return outs[0], outs[1], tuple(outs[2:])


def _att_dkv_call(q, k, v, lse, delta, do, grads=()):
    t = q.shape[0]
    h = q.shape[1] // ATT_DQK
    dv = v.shape[1] // h
    tq, tk, ratio = _att_tiles(t)
    nq, nk = t // tq, t // tk
    dims_nt = (((1,), (1,)), ((), ()))
    dims_tn = (((0,), (0,)), ((), ()))
    n_c = len(grads)

    def body(q_ref, k_ref, v_ref, lse_ref, delta_ref, do_ref, *rest):
        x_refs, rest = rest[:n_c], rest[n_c:]
        dk_ref, dv_ref = rest[:2]
        p_refs, rest = rest[2:2 + n_c], rest[2 + n_c:]
        dk_scr, dv_scr = rest[:2]
        j, i = pl.program_id(1), pl.program_id(2)
        if n_c:
            start, finish = _exchange_phases(x_refs, p_refs, *rest[2:])
            first, _, last = _grid_marks(h, nk, nq)
            pl.when(first)(start)

        @pl.when(i == 0)
        def _():
            dk_scr[...] = jnp.zeros_like(dk_scr)
            dv_scr[...] = jnp.zeros_like(dv_scr)

        def step(masked):
            s = lax.dot_general(q_ref[...], k_ref[...], dims_nt, preferred_element_type=F32) * ATT_SCALE
            if masked:
                s = _att_mask(s, i, j, tq, tk)
            p = jnp.exp(s - lse_ref[0])
            do_b = do_ref[...].astype(BF16)
            dv_scr[...] += lax.dot_general(p.astype(BF16), do_b, dims_tn, preferred_element_type=F32)
            dp = lax.dot_general(do_b, v_ref[...], dims_nt, preferred_element_type=F32)
            ds = p * (dp - delta_ref[0]) * ATT_SCALE
            dk_scr[...] += lax.dot_general(ds.astype(BF16), q_ref[...], dims_tn, preferred_element_type=F32)

        pl.when(i // ratio > j)(functools.partial(step, False))
        pl.when(i // ratio == j)(functools.partial(step, True))

        @pl.when(i == nq - 1)
        def _():
            dk_ref[...] = dk_scr[...].astype(dk_ref.dtype)
            dv_ref[...] = dv_scr[...].astype(dv_ref.dtype)

        if n_c:
            pl.when(last)(finish)

    def qi(i, j):
        return jnp.maximum(i, j * ratio)

    outs = pl.pallas_call(
        body, grid=(h, nk, nq),
        in_specs=[pl.BlockSpec((tq, ATT_DQK), lambda hh, j, i: (qi(i, j), hh)),
                  pl.BlockSpec((tk, ATT_DQK), lambda hh, j, i: (j, hh)),
                  pl.BlockSpec((tk, dv), lambda hh, j, i: (j, hh)),
                  pl.BlockSpec((1, tq, 1), lambda hh, j, i: (hh, qi(i, j), 0)),
                  pl.BlockSpec((1, tq, 1), lambda hh, j, i: (hh, qi(i, j), 0)),
                  pl.BlockSpec((tq, dv), lambda hh, j, i: (qi(i, j), hh))] + [ANY] * n_c,
        out_specs=[pl.BlockSpec((tk, ATT_DQK), lambda hh, j, i: (j, hh)),
                   pl.BlockSpec((tk, dv), lambda hh, j, i: (j, hh))] + [ANY] * n_c,
        out_shape=[jax.ShapeDtypeStruct(k.shape, k.dtype), jax.ShapeDtypeStruct(v.shape, v.dtype)]
        + [jax.ShapeDtypeStruct(g.shape, g.dtype) for g in grads],
        scratch_shapes=[pltpu.VMEM((tk, ATT_DQK), F32), pltpu.VMEM((tk, dv), F32)]
        + (_comm_scratch(n_c) if n_c else []),
        compiler_params=_cparams("arbitrary", "arbitrary", "arbitrary"), name="att_dkv")(
            q, k, v, lse, delta, do, *grads)
    return outs[0], outs[1], tuple(outs[2:])


@jax.custom_vjp
def attention(q, k, v, shards):
    o, _, gathered = _att_fwd_call(q, k, v, tuple(s.astype(BF16) for s in shards))
    return o, gathered


def _att_vjp_fwd(q, k, v, shards):
    o, lse, gathered = _att_fwd_call(q, k, v, tuple(s.astype(BF16) for s in shards))
    return (o, gathered), (q, k, v, o, lse)


def _att_vjp_bwd(res, cts):
    q, k, v, o, lse = res
    do, g_gathered = cts
    sizes = [g.size for g in g_gathered]
    cut = 0
    while cut < len(sizes) and 2 * sum(sizes[:cut + 1]) <= sum(sizes):
        cut += 1
    dq, delta, parts_a = _att_dq_call(q, k, v, o, lse, do, tuple(g_gathered[:cut]))
    dk, dv, parts_b = _att_dkv_call(q, k, v, lse, delta, do, tuple(g_gathered[cut:]))
    return dq, dk, dv, tuple(sum_parts(p) for p in parts_a + parts_b)


attention.defvjp(_att_vjp_fwd, _att_vjp_bwd)


MID = CHUNK // 2 - 1


def _tril(n):
    r = lax.broadcasted_iota(jnp.int32, (n, n), 0)
    c = lax.broadcasted_iota(jnp.int32, (n, n), 1)
    return c <= r


def _bdot(a, b, dims):
    return lax.dot_general(a.astype(BF16), b.astype(BF16), dims, preferred_element_type=F32)


NN = (((1,), (0,)), ((), ()))
NT = (((1,), (1,)), ((), ()))
TN = (((0,), (0,)), ((), ()))


def _hgrn_chunk(state, q_in, f_in, i_in, lb):
    tril = _tril(CHUNK)
    f = lb + (1.0 - lb) * jax.nn.sigmoid(f_in)
    logf = jnp.log(f)
    b = jnp.dot(tril.astype(F32), logf, precision=HI)
    q = _silu(q_in) * HG_DK ** -0.5
    k = 1.0 - f
    b_mid = b[MID:MID + 1, :]
    att = _bdot(q * jnp.exp(b - b_mid), k * jnp.exp(b_mid - b), NT)
    att = jnp.where(tril, att, 0.0)
    o = _bdot(q * jnp.exp(b), state, NT) + _bdot(att, i_in, NN)
    b_last = b[CHUNK - 1:CHUNK, :]
    new_state = jnp.exp(b_last) * state + _bdot(i_in, k * jnp.exp(b_last - b), TN)
    return o, new_state


HG_STEP_CHUNKS = 4


def _hgrn_step_rows(t):
    n = HG_STEP_CHUNKS
    while (t // CHUNK) % n:
        n //= 2
    return n * CHUNK


def _hgrn_chunks(state, q_in, f_in, i_in, lb):
    outs = []
    for c in range(q_in.shape[0] // CHUNK):
        rows = slice(c * CHUNK, (c + 1) * CHUNK)
        o, state = _hgrn_chunk(state, q_in[rows], f_in[rows], i_in[rows], lb)
        outs.append(o)
    return (outs[0] if len(outs) == 1 else jnp.concatenate(outs, axis=0)), state


def _hgrn_fwd_call(q, f, i, lb):
    t = q.shape[0]
    rows = _hgrn_step_rows(t)
    nc = t // rows
    blk = pl.BlockSpec((rows, HG_DK), lambda h, c: (c, h))

    def body(q_ref, f_ref, i_ref, lb_ref, o_ref, s_ref, s_scr):
        @pl.when(pl.program_id(1) == 0)
        def _():
            s_scr[...] = jnp.zeros_like(s_scr)

        s_ref[0, 0] = s_scr[...]
        o, ns = _hgrn_chunks(s_scr[...], q_ref[...], f_ref[...], i_ref[...], lb_ref[...])
        o_ref[...] = o
        s_scr[...] = ns

    return pl.pallas_call(
        body, grid=(HG_HEADS, nc),
        in_specs=[blk, blk, blk, pl.BlockSpec((1, HG_DK), lambda h, c: (0, h))],
        out_specs=[blk, pl.BlockSpec((1, 1, HG_DK, HG_DK), lambda h, c: (h, c, 0, 0))],
        out_shape=[jax.ShapeDtypeStruct((t, HG_WIDTH), F32), jax.ShapeDtypeStruct((HG_HEADS, nc, HG_DK, HG_DK), F32)],
        scratch_shapes=[pltpu.VMEM((HG_DK, HG_DK), F32)],
        compiler_params=_cparams("parallel", "arbitrary"), name="hgrn_fwd")(q, f, i, lb)


def _hgrn_bwd_call(q, f, i, lb, states, do):
    t = q.shape[0]
    rows = _hgrn_step_rows(t)
    nc = t // rows
    blk = pl.BlockSpec((rows, HG_DK), lambda h, c: (nc - 1 - c, h))
    row = pl.BlockSpec((1, HG_DK), lambda h, c: (0, h))

    def body(q_ref, f_ref, i_ref, lb_ref, s_ref, do_ref, dq_ref, df_ref, di_ref, dlb_ref, ds_scr):
        c = pl.program_id(1)

        @pl.when(c == 0)
        def _():
            ds_scr[...] = jnp.zeros_like(ds_scr)

        _, vjp = jax.vjp(_hgrn_chunks, s_ref[0, 0], q_ref[...], f_ref[...], i_ref[...], lb_ref[...])
        ds, dq, df, di, dlb = vjp((do_ref[...], ds_scr[...]))
        ds_scr[...] = ds
        dq_ref[...] = dq
        df_ref[...] = df
        di_ref[...] = di

        @pl.when(c == 0)
        def _():
            dlb_ref[...] = dlb

        @pl.when(c != 0)
        def _():
            dlb_ref[...] += dlb

    return pl.pallas_call(
        body, grid=(HG_HEADS, nc),
        in_specs=[blk, blk, blk, row, pl.BlockSpec((1, 1, HG_DK, HG_DK), lambda h, c: (h, nc - 1 - c, 0, 0)), blk],
        out_specs=[blk, blk, blk, row],
        out_shape=[jax.ShapeDtypeStruct((t, HG_WIDTH), F32)] * 3 + [jax.ShapeDtypeStruct((1, HG_WIDTH), F32)],
        scratch_shapes=[pltpu.VMEM((HG_DK, HG_DK), F32)],
        compiler_params=_cparams("parallel", "arbitrary"), name="hgrn_bwd")(q, f, i, lb, states, do)


@jax.custom_vjp
def hgrn_scan(q, f, i, lb):
    return _hgrn_fwd_call(q, f, i, lb)[0]


def _hgrn_vjp_fwd(q, f, i, lb):
    o, states = _hgrn_fwd_call(q, f, i, lb)
    return o, (q, f, i, lb, states)


def _hgrn_vjp_bwd(res, do):
    return tuple(_hgrn_bwd_call(*res, do))


hgrn_scan.defvjp(_hgrn_vjp_fwd, _hgrn_vjp_bwd)


def _ssd_chunk(state, xs, bm, cm, dtx, alog, dskip):
    assert CHUNK == SSM_HEADDIM and 2 * SSM_HEADDIM == LANES
    gw = xs.shape[1]
    trilf = _tril(CHUNK).astype(F32)
    da = dtx * (-jnp.exp(alog))
    a = jnp.dot(trilf, da, precision=HI)
    xdt = xs * dtx
    cb2 = _bdot(cm, jnp.concatenate([bm, bm], axis=0), NT)
    row = lax.broadcasted_iota(jnp.int32, (CHUNK, LANES), 0)
    src = lax.broadcasted_iota(jnp.int32, (CHUNK, LANES), 1) % CHUNK
    first_head = lax.broadcasted_iota(jnp.int32, (CHUNK, LANES), 1) < CHUNK
    ys = []
    for p in range(gw // LANES):
        lanes = slice(p * LANES, (p + 1) * LANES)
        a_src = jnp.sum(jnp.where(row <= src, da[:, lanes], 0.0), axis=0, keepdims=True)
        decay_ls = jnp.exp(jnp.where(src <= row, a[:, lanes] - a_src, NEG))
        x_pair = xdt[:, lanes]
        rhs = jnp.concatenate([jnp.where(first_head, x_pair, 0.0), jnp.where(first_head, 0.0, x_pair)], axis=0)
        ys.append(_bdot(cb2 * decay_ls, rhs, NN))
    y_diag = ys[0] if len(ys) == 1 else jnp.concatenate(ys, axis=1)
    y_off = jnp.exp(a) * _bdot(cm, state, NN)
    y = y_diag + y_off + xs * dskip
    a_last = a[CHUNK - 1:CHUNK, :]
    new_state = jnp.exp(a_last) * state + _bdot(bm, jnp.exp(a_last - a) * xdt, TN)
    return y, new_state


def _ssd_specs(inner, nc, rev):
    gw = inner // SSM_GROUPS
    nb = inner // SSM_STATE
    ci = (lambda c: nc - 1 - c) if rev else (lambda c: c)
    xs = pl.BlockSpec((CHUNK, gw), lambda g, c: (ci(c), g))
    bm = pl.BlockSpec((CHUNK, SSM_STATE), lambda g, c: (ci(c), nb + g))
    cm = pl.BlockSpec((CHUNK, SSM_STATE), lambda g, c: (ci(c), nb + SSM_GROUPS + g))
    row = pl.BlockSpec((1, gw), lambda g, c: (0, g))
    st = pl.BlockSpec((1, 1, SSM_STATE, gw), lambda g, c: (g, ci(c), 0, 0))
    return gw, xs, bm, cm, row, st


def _ssd_fwd_call(xbc, dtx, alog, dskip):
    t, inner = dtx.shape
    nc = t // CHUNK
    gw, xs_s, bm_s, cm_s, row, st = _ssd_specs(inner, nc, False)

    def body(xs_ref, bm_ref, cm_ref, dt_ref, a_ref, d_ref, y_ref, s_ref, s_scr):
        @pl.when(pl.program_id(1) == 0)
        def _():
            s_scr[...] = jnp.zeros_like(s_scr)

        s_ref[0, 0] = s_scr[...]
        y, ns = _ssd_chunk(s_scr[...], xs_ref[...], bm_ref[...], cm_ref[...], dt_ref[...], a_ref[...], d_ref[...])
        y_ref[...] = y
        s_scr[...] = ns

    return pl.pallas_call(
        body, grid=(SSM_GROUPS, nc), in_specs=[xs_s, bm_s, cm_s, xs_s, row, row],
        out_specs=[xs_s, st],
        out_shape=[jax.ShapeDtypeStruct((t, inner), F32), jax.ShapeDtypeStruct((SSM_GROUPS, nc, SSM_STATE, gw), F32)],
        scratch_shapes=[pltpu.VMEM((SSM_STATE, gw), F32)],
        compiler_params=_cparams("parallel", "arbitrary"), name="ssd_fwd")(xbc, xbc, xbc, dtx, alog, dskip)


def _ssd_bwd_call(xbc, dtx, alog, dskip, states, dy):
    t, inner = dtx.shape
    nc = t // CHUNK
    gw, xs_s, bm_s, cm_s, row, st = _ssd_specs(inner, nc, True)
    gn = pl.BlockSpec((CHUNK, SSM_STATE), lambda g, c: (nc - 1 - c, g))
    gn_shape = jax.ShapeDtypeStruct((t, SSM_GROUPS * SSM_STATE), F32)

    def body(xs_ref, bm_ref, cm_ref, dt_ref, a_ref, d_ref, s_ref, dy_ref,
             dxs_ref, dbm_ref, dcm_ref, ddt_ref, da_ref, dd_ref, ds_scr):
        c = pl.program_id(1)

        @pl.when(c == 0)
        def _():
            ds_scr[...] = jnp.zeros_like(ds_scr)

        _, vjp = jax.vjp(_ssd_chunk, s_ref[0, 0], xs_ref[...], bm_ref[...], cm_ref[...], dt_ref[...],
                         a_ref[...], d_ref[...])
        ds, dxs, dbm, dcm, ddt, da, dd = vjp((dy_ref[...], ds_scr[...]))
        ds_scr[...] = ds
        dxs_ref[...] = dxs
        dbm_ref[...] = dbm
        dcm_ref[...] = dcm
        ddt_ref[...] = ddt

        @pl.when(c == 0)
        def _():
            da_ref[...] = da
            dd_ref[...] = dd

        @pl.when(c != 0)
        def _():
            da_ref[...] += da
            dd_ref[...] += dd

    big = jax.ShapeDtypeStruct((t, inner), F32)
    small = jax.ShapeDtypeStruct((1, inner), F32)
    return pl.pallas_call(
        body, grid=(SSM_GROUPS, nc), in_specs=[xs_s, bm_s, cm_s, xs_s, row, row, st, xs_s],
        out_specs=[xs_s, gn, gn, xs_s, row, row],
        out_shape=[big, gn_shape, gn_shape, big, small, small],
        scratch_shapes=[pltpu.VMEM((SSM_STATE, gw), F32)],
        compiler_params=_cparams("parallel", "arbitrary"), name="ssd_bwd")(xbc, xbc, xbc, dtx, alog, dskip, states, dy)


@jax.custom_vjp
def ssd_scan(xbc, dtx, alog, dskip):
    return _ssd_fwd_call(xbc, dtx, alog, dskip)[0]


def _ssd_vjp_fwd(xbc, dtx, alog, dskip):
    y, states = _ssd_fwd_call(xbc, dtx, alog, dskip)
    return y, (xbc, dtx, alog, dskip, states)


def _ssd_vjp_bwd(res, dy):
    dxs, dbm, dcm, ddt, da, dd = _ssd_bwd_call(*res, dy)
    return jnp.concatenate([dxs, dbm, dcm], axis=1), ddt, da, dd


ssd_scan.defvjp(_ssd_vjp_fwd, _ssd_vjp_bwd)


ANY = pl.BlockSpec(memory_space=pl.ANY)


def _my_pos():
    return lax.axis_index("x"), lax.axis_index("y"), lax.axis_index("c")


def _comm_scratch(n):
    return [pltpu.SemaphoreType.DMA((n, N_DEV - 1)), pltpu.SemaphoreType.DMA((n, N_DEV - 1)),
            pltpu.SemaphoreType.DMA((n,))]


def _gather_phases(x_refs, out_refs, send_sems, recv_sems, local_sems):
    x, y, c = _my_pos()
    me, sibling = (x, y, c), (x, y, 1 - c)
    chips = [(1 - x, y), (x, 1 - y), (1 - x, 1 - y)]

    def slot(t, px, py, pc):
        return out_refs[t].at[4 * px + 2 * py + pc]

    def copy(t, k, block, to, own=False):
        return pltpu.make_async_remote_copy(
            src_ref=x_refs[t] if own else slot(t, *block), dst_ref=slot(t, *block),
            send_sem=send_sems.at[t, k], recv_sem=recv_sems.at[t, k], device_id=to, device_id_type=MESH)

    def mine(t):
        return pltpu.make_async_copy(x_refs[t], slot(t, *me), local_sems.at[t])

    def first(t):
        return [copy(t, 0, me, sibling, own=True)] + [copy(t, 1 + j, me, (*chip, c), own=True)
                                                      for j, chip in enumerate(chips)]

    def passed(t):
        return [copy(t, 4 + j, (*chip, c), sibling) for j, chip in enumerate(chips)]

    def start():
        for t in range(len(x_refs)):
            mine(t).start()
            for cp in first(t):
                cp.start()

    def middle():
        for t in range(len(x_refs)):
            for j, chip in enumerate(chips):
                copy(t, 1 + j, (*chip, c), me).wait_recv()
                copy(t, 4 + j, (*chip, c), sibling).start()

    def finish():
        for t in range(len(x_refs)):
            copy(t, 0, sibling, me).wait_recv()
            for j, chip in enumerate(chips):
                copy(t, 4 + j, (*chip, 1 - c), me).wait_recv()
            for cp in first(t) + passed(t):
                cp.wait_send()
            mine(t).wait()

    return start, middle, finish


def _exchange_phases(x_refs, out_refs, send_sems, recv_sems, local_sems):
    x, y, c = _my_pos()
    me = 4 * x + 2 * y + c

    def local(t):
        return pltpu.make_async_copy(x_refs[t].at[me], out_refs[t].at[me], local_sems.at[t])

    def copies(t):
        out = []
        for k in range(1, N_DEV):
            px = 1 - x if k & 4 else x
            py = 1 - y if k & 2 else y
            pc = 1 - c if k & 1 else c
            out.append(pltpu.make_async_remote_copy(
                src_ref=x_refs[t].at[4 * px + 2 * py + pc], dst_ref=out_refs[t].at[me],
                send_sem=send_sems.at[t, k - 1], recv_sem=recv_sems.at[t, k - 1],
                device_id=(px, py, pc), device_id_type=MESH))
        return out

    def start():
        for t in range(len(x_refs)):
            local(t).start()
            for cp in copies(t):
                cp.start()

    def finish():
        for t in range(len(x_refs)):
            for cp in copies(t):
                cp.wait_recv()
            for cp in copies(t):
                cp.wait_send()
            local(t).wait()

    return start, finish


def all_gather(x_shard):
    def body(x_ref, out_ref, send_sems, recv_sems, local_sems):
        start, middle, finish = _gather_phases([x_ref], [out_ref], send_sems, recv_sems, local_sems)
        start()
        middle()
        finish()

    return pl.pallas_call(
        body, out_shape=jax.ShapeDtypeStruct((N_DEV,) + x_shard.shape, x_shard.dtype), in_specs=[ANY],
        out_specs=ANY, scratch_shapes=_comm_scratch(1), name="all_gather")(x_shard)


def exchange(x):
    def body(x_ref, out_ref, send_sems, recv_sems, local_sems):
        start, finish = _exchange_phases([x_ref], [out_ref], send_sems, recv_sems, local_sems)
        start()
        finish()

    return pl.pallas_call(
        body, out_shape=jax.ShapeDtypeStruct(x.shape, x.dtype), in_specs=[ANY], out_specs=ANY,
        scratch_shapes=_comm_scratch(1), name="exchange")(x)


def sum_parts(parts):
    p, r, c_ = parts.shape
    br = _row_block(r, [c_]) if r % 16 == 0 else r

    def body(p_ref, o_ref):
        acc = p_ref[0].astype(F32)
        for i in range(1, p):
            acc = acc + p_ref[i].astype(F32)
        o_ref[...] = acc

    return pl.pallas_call(
        body, grid=(r // br,), in_specs=[pl.BlockSpec((p, br, c_), lambda i: (0, i, 0))],
        out_specs=pl.BlockSpec((br, c_), lambda i: (i, 0)), out_shape=jax.ShapeDtypeStruct((r, c_), F32),
        compiler_params=_cparams("parallel"), name="sum_parts")(parts)


@jax.custom_vjp
def gather_op(shard):
    return all_gather(shard.astype(BF16))


def _gather_op_fwd(shard):
    return all_gather(shard.astype(BF16)), None


def _gather_op_bwd(_, g):
    return (sum_parts(_in_chunks(exchange, g, 1)),)


gather_op.defvjp(_gather_op_fwd, _gather_op_bwd)


def _in_chunks(fn, arr, axis):
    rows = arr.shape[axis]
    pieces = 1
    while (arr.size * arr.dtype.itemsize) // pieces > COMM_BYTES and rows % (2 * pieces) == 0:
        pieces *= 2
    step = rows // pieces
    outs = [fn(lax.slice_in_dim(arr, s, s + step, axis=axis)) for s in range(0, rows, step)]
    return outs[0] if len(outs) == 1 else jnp.concatenate(outs, axis=1)


def reduce_adamw(parts, w, m, v):
    p, r, c_ = parts.shape
    br = _row_block(r, [c_]) if r % 16 == 0 else r
    c1 = 1.0 - ADAM_B1 ** ADAM_STEP
    c2 = 1.0 - ADAM_B2 ** ADAM_STEP

    def body(p_ref, w_ref, m_ref, v_ref, g_ref, d_ref, m2_ref, v2_ref):
        g = p_ref[0].astype(F32)
        for i in range(1, p):
            g = g + p_ref[i].astype(F32)
        m2 = ADAM_B1 * m_ref[...] + (1.0 - ADAM_B1) * g
        v2 = ADAM_B2 * v_ref[...] + (1.0 - ADAM_B2) * (g * g)
        g_ref[...] = g
        m2_ref[...] = m2
        v2_ref[...] = v2
        d_ref[...] = -ADAM_LR * ((m2 / c1) / (jnp.sqrt(v2 / c2) + ADAM_EPS) + ADAM_WD * w_ref[...])

    blk = pl.BlockSpec((br, c_), lambda i: (i, 0))
    return pl.pallas_call(
        body, grid=(r // br,), in_specs=[pl.BlockSpec((p, br, c_), lambda i: (0, i, 0)), blk, blk, blk],
        out_specs=[blk] * 4, out_shape=[jax.ShapeDtypeStruct((r, c_), F32)] * 4,
        compiler_params=_cparams("parallel"), name="reduce_adamw")(parts, w, m, v)


WEIGHTS = ['ffn1_norm', 'ffn1_wi', 'ffn1_wo', 'mix_norm', 'w_in', 'mla_q_norm', 'mla_w_uq', 'mla_kv_norm',
           'mla_w_ukv', 'hgrn_lb_logits', 'hgrn_norm', 'ssm_conv_w', 'ssm_conv_b', 'ssm_a_log', 'ssm_dt_bias',
           'ssm_d', 'ssm_norm', 'w_o_mla', 'w_o_hgrn', 'w_o_ssm', 'w_out', 'ffn2_norm', 'ffn2_wi', 'ffn2_wo',
           'final_norm']
COL_SHARDED = ('ffn1_wi', 'w_in', 'mla_w_uq', 'mla_w_ukv', 'ffn2_wi')
ROW_SHARDED = ('ffn1_wo', 'w_o_mla', 'w_o_hgrn', 'w_o_ssm', 'w_out', 'ffn2_wo')
BIG = tuple(n for n in WEIGHTS if n in COL_SHARDED + ROW_SHARDED)
PRE = ('ffn1_wi', 'ffn1_wo', 'w_in', 'mla_w_uq', 'mla_w_ukv')
POST = ('w_o_mla', 'w_o_hgrn', 'w_o_ssm', 'w_out', 'ffn2_wi', 'ffn2_wo')
CONV_W = 'ssm_conv_w'
REPLICATED = tuple(n for n in WEIGHTS if n not in BIG and n != CONV_W)


def _segment_plan(n, sizes):
    plan, off = [], 0
    for s in sizes:
        a, b = off, off + s
        plan.append([(j, max(a, j * n) - j * n, min(b, (j + 1) * n) - j * n)
                     for j in range(a // n, (b - 1) // n + 1)])
        off = b
    return plan


@functools.partial(jax.custom_vjp, nondiff_argnums=(1,))
def col_segments(blocks, sizes):
    outs = []
    for pieces in _segment_plan(blocks.shape[-1], sizes):
        cut = [blocks[j][:, lo:hi] for j, lo, hi in pieces]
        outs.append(cut[0] if len(cut) == 1 else jnp.concatenate(cut, axis=1))
    return tuple(outs)


def _col_segments_fwd(blocks, sizes):
    return col_segments(blocks, sizes), blocks.shape[-1]


def _col_segments_bwd(sizes, n, gs):
    per_block = [[] for _ in range(N_DEV)]
    for g, pieces in zip(gs, _segment_plan(n, sizes)):
        off = 0
        for j, lo, hi in pieces:
            per_block[j].append(g[:, off:off + hi - lo])
            off += hi - lo
    return (jnp.stack([p[0] if len(p) == 1 else jnp.concatenate(p, axis=1) for p in per_block]),)


col_segments.defvjp(_col_segments_fwd, _col_segments_bwd)


def _rope_tables(t):
    half = MLA_ROPE // 2
    inv = 1.0 / (ROPE_THETA ** (jnp.arange(0, MLA_ROPE, 2, dtype=F32) / MLA_ROPE))
    ang = jnp.arange(t, dtype=F32)[:, None] * inv[None, :]
    reps = LANES // half
    return jnp.tile(jnp.cos(ang), (1, reps)), jnp.tile(jnp.sin(ang), (1, reps))


def _ffn(x, norm, wi, wo):
    dff = wo.shape[0]
    h = rmsnorm(norm[None], x)[0]
    wg, wu = col_segments(wi, (dff, dff))
    return x + 0.5 * mm(swiglu(mm(h, wg), mm(h, wu))[0], wo)


def _per_head(w, widths, pad_to):
    k = w.shape[0]
    w3 = w.reshape(k, -1, sum(widths))
    outs, off = [], 0
    for wd in widths:
        part = w3[:, :, off:off + wd]
        if wd < pad_to:
            part = jnp.pad(part, ((0, 0), (0, 0), (0, pad_to - wd)))
        outs.append(part.reshape(k, -1))
        off += wd
    return outs


def _layer(x, p, lb, cos, sin, carried):
    t, d = x.shape
    inner = 2 * d
    conv_dim = inner + 2 * SSM_GROUPS * SSM_STATE
    n_ssm_heads = inner // SSM_HEADDIM
    x = _ffn(x, p['ffn1_norm'], p['ffn1_wi'], p['ffn1_wo'])

    h = rmsnorm(p['mix_norm'][None], x)[0]
    sizes = (MLA_Q_RANK, MLA_KV_RANK, MLA_ROPE, HG_WIDTH, HG_WIDTH, HG_WIDTH, HG_WIDTH,
             inner, conv_dim, n_ssm_heads, d, d, d)
    (w_q, w_kv, w_kpe, w_hq, w_hf, w_hi, w_hg, w_z, w_xbc, w_dt, w_ga, w_gb, w_gc) = col_segments(p['w_in'], sizes)

    qn = rmsnorm(p['mla_q_norm'][None], mm(h, w_q))[0]
    kvn = rmsnorm(p['mla_kv_norm'][None], mm(h, w_kv))[0]
    w_uq, = col_segments(p['mla_w_uq'], (N_DEV * p['mla_w_uq'].shape[-1],))
    w_ukv, = col_segments(p['mla_w_ukv'], (N_DEV * p['mla_w_ukv'].shape[-1],))
    wq_nope, wq_pe = _per_head(w_uq, (MLA_NOPE, MLA_ROPE), LANES)
    wk_nope, wv = _per_head(w_ukv, (MLA_NOPE, MLA_V), LANES)
    q_nope = mm(qn, wq_nope)
    q_pe = rope(mm(qn, wq_pe), cos, sin)[0]
    k_nope = mm(kvn, wk_nope)
    v = mm(kvn, wv)
    k_rot = rope(mm(h, jnp.pad(w_kpe, ((0, 0), (0, LANES - MLA_ROPE)))), cos, sin)[0]
    q = jnp.concatenate([q_nope.reshape(t, MLA_HEADS, LANES), q_pe.reshape(t, MLA_HEADS, LANES)], axis=2)
    k = jnp.concatenate([k_nope.reshape(t, MLA_HEADS, LANES),
                         jnp.broadcast_to(k_rot[:, None, :], (t, MLA_HEADS, LANES))], axis=2)
    names = tuple(carried)
    o, blocks = attention(q.reshape(t, -1).astype(BF16), k.reshape(t, -1).astype(BF16), v.astype(BF16),
                          tuple(carried[n] for n in names))
    arrived = dict(zip(names, blocks))
    p = dict(p)
    for n in POST:
        p[n] = _usable(n, arrived[(n, 0)])
    y_a = mm(o, p['w_o_mla'])

    o = hgrn_scan(mm(h, w_hq), mm(h, w_hf), mm(h, w_hi), lb[None])
    o = hgrn_out(p['hgrn_norm'][None], o, mm(h, w_hg))[0]
    y_b = mm(o, p['w_o_hgrn'])

    xbc = silu_op(conv(p['ssm_conv_w'], p['ssm_conv_b'][None], mm(h, w_xbc)))[0]
    dtx = dt_expand(p['ssm_dt_bias'][None], mm(h, w_dt))[0]
    y = ssd_scan(xbc, dtx, jnp.repeat(p['ssm_a_log'], SSM_HEADDIM)[None], jnp.repeat(p['ssm_d'], SSM_HEADDIM)[None])
    y = ssm_norm(p['ssm_norm'][None], y, mm(h, w_z))[0]
    y_c = mm(y, p['w_o_ssm'])

    merged = merge(y_a, y_b, y_c, mm(h, w_ga), mm(h, w_gb), mm(h, w_gc))[0]
    x = x + mm(merged, p['w_out'])
    return _ffn(x, p['ffn2_norm'], p['ffn2_wi'], p['ffn2_wo']), arrived


def _usable(name, blocks):
    return blocks.reshape(-1, blocks.shape[-1]) if name in ROW_SHARDED else blocks


def _local_loss(shards, params, x, target):
    depth = params['ffn1_norm'].shape[0]
    cos, sin = _rope_tables(x.shape[0])
    prob = jax.nn.softmax(params['hgrn_lb_logits'], axis=0)
    lower = jnp.cumsum(prob, axis=0) - prob[0:1]
    pre = {n: gather_op(shards[n][0]) for n in PRE}
    for layer in range(depth):
        p = {n: params[n][layer] for n in REPLICATED + (CONV_W,) if n != 'final_norm'}
        for n in PRE:
            p[n] = _usable(n, pre[n])
        carried = {(n, 0): shards[n][layer] for n in POST}
        if layer + 1 < depth:
            carried.update({(n, 1): shards[n][layer + 1] for n in PRE})
        x, arrived = _layer(x, p, lower[layer], cos, sin, carried)
        pre = {n: arrived.get((n, 1)) for n in PRE}
    return jnp.sum(loss_rows(params['final_norm'][None], x, target)[0])


def _pack_vec(arrays, rows):
    flat = jnp.concatenate([a.reshape(-1) for a in arrays])
    return jnp.pad(flat, (0, rows * COMM_COLS - flat.shape[0])).reshape(rows, COMM_COLS)


def _unpack(flat, shapes, lead=()):
    flat = flat.reshape(lead + (-1,))
    outs, off = [], 0
    for s in shapes:
        n = 1
        for dim in s:
            n *= dim
        outs.append(flat[..., off:off + n].reshape(lead + tuple(s)))
        off += n
    return outs


def _round_up(n, m):
    return -(-n // m) * m


def _step(a):
    x = a['x'][0]
    target = a['loss_target'][0]
    me = 4 * lax.axis_index("x") + 2 * lax.axis_index("y") + lax.axis_index("c")

    conv_shape = a[CONV_W].shape
    conv_full_shape = conv_shape[:-1] + (conv_shape[-1] * N_DEV,)
    rep_shapes = [a[n].shape for n in REPLICATED]
    n_small = 1 + sum(a[n].size for n in REPLICATED) + a[CONV_W].size * N_DEV
    small_rows = _round_up(-(-n_small // COMM_COLS), 8)

    params = {}
    conv_blocks = _unpack(all_gather(_pack_vec([a[CONV_W]], small_rows)), [conv_shape], lead=(N_DEV,))[0]
    params[CONV_W] = jnp.moveaxis(conv_blocks, 0, -2).reshape(conv_full_shape)
    for n in REPLICATED:
        params[n] = a[n]
    depth = a['ffn1_norm'].shape[0]
    shards = {n: [a[n][layer] for layer in range(depth)] for n in BIG}

    loss, (gs, gp, gx) = jax.value_and_grad(_local_loss, argnums=(0, 1, 2))(shards, params, x, target)

    big_out = [{}, {}, {}, {}]
    for n in BIG:
        width = a[n].shape[-1]
        grad = jnp.stack(gs[n]).reshape(1, -1, width)
        res = reduce_adamw(grad, *[a[pre + n].reshape(-1, width) for pre in ('', 'm_', 'v_')])
        for kind in range(4):
            big_out[kind][n] = res[kind].reshape(a[n].shape)

    small = _pack_vec([loss.reshape(1)] + [gp[n] for n in REPLICATED] + [gp[CONV_W]], small_rows)
    zero1, one1 = jnp.zeros((1,), F32), jnp.ones((1,), F32)
    zero_c, one_c = jnp.zeros(conv_full_shape, F32), jnp.ones(conv_full_shape, F32)
    small_w = _pack_vec([zero1] + [a[n] for n in REPLICATED] + [zero_c], small_rows)
    small_m = _pack_vec([zero1] + [a['m_' + n] for n in REPLICATED] + [zero_c], small_rows)
    small_v = _pack_vec([one1] + [a['v_' + n] for n in REPLICATED] + [one_c], small_rows)
    res = reduce_adamw(all_gather(small), small_w, small_m, small_v)
    small_out = []
    for r in res:
        pieces = _unpack(r, [(1,)] + rep_shapes + [conv_full_shape])
        small_out.append((pieces[0], dict(zip(REPLICATED, pieces[1:-1])), pieces[-1]))
    total_loss = small_out[0][0][0]

    width = conv_shape[-1]
    g_conv = lax.dynamic_slice_in_dim(small_out[0][2], me * width, width, axis=len(conv_shape) - 1)
    conv_rows = -(-a[CONV_W].size // COMM_COLS)
    conv_res = reduce_adamw(_pack_vec([g_conv], conv_rows)[None],
                            *[_pack_vec([a[pre + CONV_W]], conv_rows) for pre in ('', 'm_', 'v_')])
    conv_out = [_unpack(r, [conv_shape])[0] for r in conv_res]

    outs = [total_loss, gx[None]]
    for kind in range(4):
        for n in WEIGHTS:
            if n in BIG:
                outs.append(big_out[kind][n])
            elif n == CONV_W:
                outs.append(conv_out[kind])
            else:
                outs.append(small_out[kind][1][n])
    return tuple(outs)


def kernel(x, ffn1_norm, ffn1_wi, ffn1_wo, mix_norm, w_in, mla_q_norm, mla_w_uq, mla_kv_norm, mla_w_ukv, hgrn_lb_logits, hgrn_norm, ssm_conv_w, ssm_conv_b, ssm_a_log, ssm_dt_bias, ssm_d, ssm_norm, w_o_mla, w_o_hgrn, w_o_ssm, w_out, ffn2_norm, ffn2_wi, ffn2_wo, final_norm, loss_target, m_ffn1_norm, m_ffn1_wi, m_ffn1_wo, m_mix_norm, m_w_in, m_mla_q_norm, m_mla_w_uq, m_mla_kv_norm, m_mla_w_ukv, m_hgrn_lb_logits, m_hgrn_norm, m_ssm_conv_w, m_ssm_conv_b, m_ssm_a_log, m_ssm_dt_bias, m_ssm_d, m_ssm_norm, m_w_o_mla, m_w_o_hgrn, m_w_o_ssm, m_w_out, m_ffn2_norm, m_ffn2_wi, m_ffn2_wo, m_final_norm, v_ffn1_norm, v_ffn1_wi, v_ffn1_wo, v_mix_norm, v_w_in, v_mla_q_norm, v_mla_w_uq, v_mla_kv_norm, v_mla_w_ukv, v_hgrn_lb_logits, v_hgrn_norm, v_ssm_conv_w, v_ssm_conv_b, v_ssm_a_log, v_ssm_dt_bias, v_ssm_d, v_ssm_norm, v_w_o_mla, v_w_o_hgrn, v_w_o_ssm, v_w_out, v_ffn2_norm, v_ffn2_wi, v_ffn2_wo, v_final_norm):
    return _step(dict(locals()))
```

```python
import functools

import jax
import jax.numpy as jnp
from jax import lax
from jax.experimental import pallas as pl
from jax.experimental.pallas import tpu as pltpu

F32 = jnp.float32
BF16 = jnp.bfloat16
HI = lax.Precision.HIGHEST
MESH = pl.DeviceIdType.MESH

EPS = 1e-6
CHUNK = 64
N_DEV = 8

MLA_HEADS = 16
MLA_Q_RANK = 512
MLA_KV_RANK = 512
MLA_NOPE = 128
MLA_ROPE = 64
MLA_V = 128
ROPE_THETA = 10000.0
HG_HEADS = 16
HG_DK = 128
HG_WIDTH = HG_HEADS * HG_DK
SSM_HEADDIM = 64
SSM_GROUPS = 8
SSM_STATE = 128
SSM_CONV = 4

ADAM_LR = 0.001
ADAM_B1 = 0.9
ADAM_B2 = 0.999
ADAM_EPS = 1e-08
ADAM_WD = 0.01
ADAM_STEP = 10

LANES = 128
VMEM_LIMIT = 48 * 1024 * 1024
ROW_BLOCK_ELEMS = 256 * 1024
MM_TILE_M, MM_TILE_N, MM_TILE_K = 1408, 1536, 1536
COMM_COLS = 1024
COMM_BYTES = 128 * 1024 * 1024


def _cparams(*sem):
    return pltpu.CompilerParams(dimension_semantics=sem, vmem_limit_bytes=VMEM_LIMIT)


def _tile(dim, pref):
    best = dim
    for t in range(LANES, min(dim, pref) + 1, LANES):
        if dim % t == 0:
            best = t
    return best


def _mm_call(a, b, mode, out_dtype, name):
    if mode == "nn":
        (m, k), (k2, n) = a.shape, b.shape
    elif mode == "nt":
        (m, k), (n, k2) = a.shape, b.shape
    else:
        (k, m), (k2, n) = a.shape, b.shape
    assert k == k2, (a.shape, b.shape, mode)
    tm, tn, tk = _tile(m, MM_TILE_M), _tile(n, MM_TILE_N), _tile(k, MM_TILE_K)
    nk = k // tk
    if mode == "nn":
        a_spec = pl.BlockSpec((tm, tk), lambda i, j, kk: (i, kk))
        b_spec = pl.BlockSpec((tk, tn), lambda i, j, kk: (kk, j))
        dims = (((1,), (0,)), ((), ()))
    elif mode == "nt":
        a_spec = pl.BlockSpec((tm, tk), lambda i, j, kk: (i, kk))
        b_spec = pl.BlockSpec((tn, tk), lambda i, j, kk: (j, kk))
        dims = (((1,), (1,)), ((), ()))
    else:
        a_spec = pl.BlockSpec((tk, tm), lambda i, j, kk: (kk, i))
        b_spec = pl.BlockSpec((tk, tn), lambda i, j, kk: (kk, j))
        dims = (((0,), (0,)), ((), ()))

    def body(a_ref, b_ref, o_ref, acc_ref):
        kk = pl.program_id(2)

        @pl.when(kk == 0)
        def _():
            acc_ref[...] = jnp.zeros_like(acc_ref)

        acc_ref[...] += lax.dot_general(a_ref[...].astype(BF16), b_ref[...].astype(BF16), dims,
                                        preferred_element_type=F32)

        @pl.when(kk == nk - 1)
        def _():
            o_ref[...] = acc_ref[...].astype(o_ref.dtype)

    return pl.pallas_call(
        body, grid=(m // tm, n // tn, nk), in_specs=[a_spec, b_spec],
        out_specs=pl.BlockSpec((tm, tn), lambda i, j, kk: (i, j)),
        out_shape=jax.ShapeDtypeStruct((m, n), out_dtype),
        scratch_shapes=[pltpu.VMEM((tm, tn), F32)],
        compiler_params=_cparams("parallel", "parallel", "arbitrary"), name=name)(a, b)


@jax.custom_vjp
def mm(a, w):
    return _mm_call(a.astype(BF16), w, "nn", F32, "mm_fwd")


def _mm_fwd(a, w):
    a16 = a.astype(BF16)
    return _mm_call(a16, w, "nn", F32, "mm_fwd"), (a16, w)


def _mm_bwd(res, g):
    a16, w = res
    g16 = g.astype(BF16)
    return _mm_call(g16, w, "nt", F32, "mm_da"), _mm_call(a16, g16, "tn", w.dtype, "mm_dw")


mm.defvjp(_mm_fwd, _mm_bwd)


def _row_block(t, widths):
    bt = 8
    while 2 * bt * max(widths) <= ROW_BLOCK_ELEMS:
        bt *= 2
    while t % bt:
        bt //= 2
    return bt


def make_rowwise(fn, name, n_par, group_width=None, shared=(), nondiff=()):
    def specs(args):
        t = max(a.shape[0] for a in args)
        cut = next(a for i, a in enumerate(args) if i >= n_par and i not in shared)
        gw = group_width(cut.shape[1]) if callable(group_width) else group_width
        groups = cut.shape[1] // gw if gw else 1
        ws = [a.shape[1] if i in shared else a.shape[1] // groups for i, a in enumerate(args)]
        ows = out_widths(ws)
        bt = _row_block(t, ws + ows)
        sp = []
        for i, a in enumerate(args):
            col = (lambda g: 0) if i in shared else (lambda g: g)
            if i < n_par:
                sp.append(pl.BlockSpec((1, ws[i]), lambda g, r, col=col: (0, col(g))))
            else:
                sp.append(pl.BlockSpec((bt, ws[i]), lambda g, r, col=col: (r, col(g))))
        return t, bt, groups, ows, sp

    def out_widths(ws):
        blocks = [jax.ShapeDtypeStruct((1 if i < n_par else 8, w), F32) for i, w in enumerate(ws)]
        return [o.shape[1] for o in jax.eval_shape(fn, *blocks)]

    def fwd_call(*args):
        t, bt, groups, ows, in_specs = specs(args)
        n_in = len(args)

        def body(*refs):
            outs = fn(*[r[...] for r in refs[:n_in]])
            for r, o in zip(refs[n_in:], outs):
                r[...] = o

        return pl.pallas_call(
            body, grid=(groups, t // bt), in_specs=in_specs,
            out_specs=[pl.BlockSpec((bt, w), lambda g, r: (r, g)) for w in ows],
            out_shape=[jax.ShapeDtypeStruct((t, w * groups), F32) for w in ows],
            compiler_params=_cparams("parallel", "parallel"), name=name + "_fwd")(*args)

    def bwd_call(args, gs):
        t, bt, groups, ows, in_specs = specs(args)
        n_in, n_out = len(args), len(gs)
        diff = [i for i in range(n_in) if i not in nondiff]
        g_specs = [pl.BlockSpec((bt, w), lambda g, r: (r, g)) for w in ows]
        o_specs, o_shapes = [], []
        for i in diff:
            o_specs.append(in_specs[i])
            o_shapes.append(jax.ShapeDtypeStruct(args[i].shape, F32))

        def body(*refs):
            r_idx = pl.program_id(1)
            vals = [r[...] for r in refs[:n_in]]
            cts = tuple(r[...] for r in refs[n_in:n_in + n_out])

            def f_diff(*dv):
                full = list(vals)
                for i, v in zip(diff, dv):
                    full[i] = v
                return tuple(fn(*full))

            _, vjp = jax.vjp(f_diff, *[vals[i] for i in diff])
            grads = vjp(cts)
            for i, g_val, ref in zip(diff, grads, refs[n_in + n_out:]):
                if i < n_par:
                    @pl.when(r_idx == 0)
                    def _(ref=ref, g_val=g_val):
                        ref[...] = g_val

                    @pl.when(r_idx != 0)
                    def _(ref=ref, g_val=g_val):
                        ref[...] += g_val
                else:
                    ref[...] = g_val

        outs = pl.pallas_call(
            body, grid=(groups, t // bt), in_specs=in_specs + g_specs, out_specs=o_specs, out_shape=o_shapes,
            compiler_params=_cparams("parallel", "arbitrary"), name=name + "_bwd")(*args, *gs)
        full = [jnp.zeros_like(a) for a in args]
        for i, o in zip(diff, outs):
            full[i] = o
        return tuple(full)

    @jax.custom_vjp
    def op(*args):
        return tuple(fwd_call(*args))

    def op_fwd(*args):
        return tuple(fwd_call(*args)), args

    def op_bwd(args, gs):
        return bwd_call(args, gs)

    op.defvjp(op_fwd, op_bwd)
    return op


def _silu(x):
    return x * jax.nn.sigmoid(x)


def _rmsnorm_fn(w, x):
    return (x * lax.rsqrt(jnp.mean(x * x, axis=-1, keepdims=True) + EPS) * w,)


def _swiglu_fn(g, u):
    return (_silu(g) * u,)


def _silu_fn(x):
    return (_silu(x),)


def _rope_fn(x, cos, sin):
    i = lax.broadcasted_iota(jnp.int32, (LANES, LANES), 0)
    j = lax.broadcasted_iota(jnp.int32, (LANES, LANES), 1)
    half = MLA_ROPE // 2
    first = (j % MLA_ROPE) < half
    p = jnp.where(first & (i == j + half), -1.0, 0.0) + jnp.where((~first) & (i == j - half), 1.0, 0.0)
    return (x * cos + jnp.dot(x, p.astype(F32), precision=HI) * sin,)


def _hgrn_out_fn(w, o, g):
    return (o * lax.rsqrt(jnp.mean(o * o, axis=-1, keepdims=True) + EPS) * w * _silu(g),)


def _softplus(x):
    return jnp.maximum(x, 0.0) + jnp.log(1.0 + jnp.exp(-jnp.abs(x)))


def _dt_expand_fn(bias, dt_raw):
    nh = dt_raw.shape[1]
    h = lax.broadcasted_iota(jnp.int32, (nh, nh * SSM_HEADDIM), 0)
    c = lax.broadcasted_iota(jnp.int32, (nh, nh * SSM_HEADDIM), 1)
    e = (c // SSM_HEADDIM == h).astype(F32)
    return (jnp.dot(_softplus(dt_raw + bias), e, precision=HI),)


def _ssm_norm_fn(w, y, z):
    y = y * _silu(z)
    return (y * lax.rsqrt(jnp.mean(y * y, axis=-1, keepdims=True) + EPS) * w,)


def _merge_fn(ya, yb, yc, ga, gb, gc):
    return (jax.nn.sigmoid(ga) * ya + jax.nn.sigmoid(gb) * yb + jax.nn.sigmoid(gc) * yc,)


def _loss_fn(w, x, tgt):
    y = x * lax.rsqrt(jnp.mean(x * x, axis=-1, keepdims=True) + EPS) * w
    err = y - tgt
    return (0.5 * jnp.mean(err * err, axis=-1, keepdims=True),)


rmsnorm = make_rowwise(_rmsnorm_fn, "rmsnorm", 1)
swiglu = make_rowwise(_swiglu_fn, "swiglu", 0, group_width=512)
silu_op = make_rowwise(_silu_fn, "silu", 0, group_width=512)
rope = make_rowwise(_rope_fn, "rope", 0, group_width=LANES, shared=(1, 2), nondiff=(1, 2))
hgrn_out = make_rowwise(_hgrn_out_fn, "hgrn_out", 1, group_width=HG_DK)
dt_expand = make_rowwise(_dt_expand_fn, "dt_expand", 1)
ssm_norm = make_rowwise(_ssm_norm_fn, "ssm_norm", 1, group_width=lambda w: w // SSM_GROUPS)
merge = make_rowwise(_merge_fn, "merge", 0, group_width=512)
loss_rows = make_rowwise(_loss_fn, "loss", 1, nondiff=(2,))


HALO = 8


def _conv_blocks(t, c):
    cw = _tile(c, 512)
    bt = _row_block(t, [cw])
    return cw, bt


def _conv_fwd_call(w, b, x):
    t, c = x.shape
    cw, bt = _conv_blocks(t, c)
    hb = bt // HALO

    def body(w_ref, b_ref, x_ref, prev_ref, y_ref):
        r = pl.program_id(1)
        prev = jnp.where(r == 0, 0.0, prev_ref[...])
        xx = jnp.concatenate([prev, x_ref[...]], axis=0)
        acc = jnp.zeros((bt, cw), F32) + b_ref[...]
        for k in range(SSM_CONV):
            sh = SSM_CONV - 1 - k
            xs = xx if sh == 0 else pltpu.roll(xx, sh, axis=0)
            acc = acc + w_ref[k:k + 1, :] * xs[HALO:, :]
        y_ref[...] = acc

    return pl.pallas_call(
        body, grid=(c // cw, t // bt),
        in_specs=[pl.BlockSpec((SSM_CONV, cw), lambda g, r: (0, g)), pl.BlockSpec((1, cw), lambda g, r: (0, g)),
                  pl.BlockSpec((bt, cw), lambda g, r: (r, g)),
                  pl.BlockSpec((HALO, cw), lambda g, r: (jnp.maximum(r * hb - 1, 0), g))],
        out_specs=pl.BlockSpec((bt, cw), lambda g, r: (r, g)),
        out_shape=jax.ShapeDtypeStruct((t, c), F32),
        compiler_params=_cparams("parallel", "parallel"), name="conv_fwd")(w, b, x, x)


def _conv_bwd_call(w, x, dy):
    t, c = x.shape
    cw, bt = _conv_blocks(t, c)
    hb = bt // HALO
    nr = t // bt

    def body(w_ref, x_ref, prev_ref, dy_ref, next_ref, dx_ref, dw_ref, db_ref):
        r = pl.program_id(1)
        prev = jnp.where(r == 0, 0.0, prev_ref[...])
        nxt = jnp.where(r == nr - 1, 0.0, next_ref[...])
        xx = jnp.concatenate([prev, x_ref[...]], axis=0)
        dd = jnp.concatenate([dy_ref[...], nxt], axis=0)
        dy_val = dy_ref[...]
        dx = jnp.zeros((bt, cw), F32)
        dws = []
        for k in range(SSM_CONV):
            sh = SSM_CONV - 1 - k
            xs = xx if sh == 0 else pltpu.roll(xx, sh, axis=0)
            ds = dd if sh == 0 else pltpu.roll(dd, bt + HALO - sh, axis=0)
            dx = dx + w_ref[k:k + 1, :] * ds[:bt, :]
            dws.append(jnp.sum(dy_val * xs[HALO:, :], axis=0, keepdims=True))
        dx_ref[...] = dx
        dw = jnp.concatenate(dws, axis=0)
        db = jnp.sum(dy_val, axis=0, keepdims=True)

        @pl.when(r == 0)
        def _():
            dw_ref[...] = dw
            db_ref[...] = db

        @pl.when(r != 0)
        def _():
            dw_ref[...] += dw
            db_ref[...] += db

    return pl.pallas_call(
        body, grid=(c // cw, nr),
        in_specs=[pl.BlockSpec((SSM_CONV, cw), lambda g, r: (0, g)),
                  pl.BlockSpec((bt, cw), lambda g, r: (r, g)),
                  pl.BlockSpec((HALO, cw), lambda g, r: (jnp.maximum(r * hb - 1, 0), g)),
                  pl.BlockSpec((bt, cw), lambda g, r: (r, g)),
                  pl.BlockSpec((HALO, cw), lambda g, r: (jnp.minimum((r + 1) * hb, nr * hb - 1), g))],
        out_specs=[pl.BlockSpec((bt, cw), lambda g, r: (r, g)),
                   pl.BlockSpec((SSM_CONV, cw), lambda g, r: (0, g)), pl.BlockSpec((1, cw), lambda g, r: (0, g))],
        out_shape=[jax.ShapeDtypeStruct((t, c), F32), jax.ShapeDtypeStruct((SSM_CONV, c), F32),
                   jax.ShapeDtypeStruct((1, c), F32)],
        compiler_params=_cparams("parallel", "arbitrary"), name="conv_bwd")(w, x, x, dy, dy)


@jax.custom_vjp
def conv(w, b, x):
    return _conv_fwd_call(w, b, x)


def _conv_vjp_fwd(w, b, x):
    return _conv_fwd_call(w, b, x), (w, x)


def _conv_vjp_bwd(res, dy):
    w, x = res
    dx, dw, db = _conv_bwd_call(w, x, dy)
    return dw, db, dx


conv.defvjp(_conv_vjp_fwd, _conv_vjp_bwd)


ATT_DQK = 2 * LANES
ATT_SCALE = (MLA_NOPE + MLA_ROPE) ** -0.5
NEG = -1e30


def _att_tiles(t):
    tq = min(512, max(CHUNK, t // 4))
    tk = min(2 * tq, t)
    return tq, tk, tk // tq


def _att_mask(s, i, j, tq, tk):
    row = (i * tq + lax.broadcasted_iota(jnp.int32, (tq, tk), 0)) // CHUNK
    col = (j * tk + lax.broadcasted_iota(jnp.int32, (tq, tk), 1)) // CHUNK
    return jnp.where(col <= row, s, NEG)


def _grid_marks(n0, n1, n2):
    a, b, c = pl.program_id(0), pl.program_id(1), pl.program_id(2)
    inner0 = (b == 0) & (c == 0)
    return (a == 0) & inner0, (a == n0 // 2) & inner0, (a == n0 - 1) & (b == n1 - 1) & (c == n2 - 1)


def _att_fwd_call(q, k, v, shards=()):
    t = q.shape[0]
    h = q.shape[1] // ATT_DQK
    dv = v.shape[1] // h
    tq, tk, ratio = _att_tiles(t)
    nq, nk = t // tq, t // tk
    dims_nt = (((1,), (1,)), ((), ()))
    n_c = len(shards)

    def body(q_ref, k_ref, v_ref, *rest):
        x_refs, rest = rest[:n_c], rest[n_c:]
        o_ref, lse_ref = rest[:2]
        g_refs, rest = rest[2:2 + n_c], rest[2 + n_c:]
        m_scr, l_scr, acc_scr = rest[:3]
        i, j = pl.program_id(1), pl.program_id(2)
        if n_c:
            start, middle, finish = _gather_phases(x_refs, g_refs, *rest[3:])
            first, mid, last = _grid_marks(h, nq, nk)
            pl.when(first)(start)
            pl.when(mid)(middle)

        @pl.when(j == 0)
        def _():
            m_scr[...] = jnp.full_like(m_scr, NEG)
            l_scr[...] = jnp.zeros_like(l_scr)
            acc_scr[...] = jnp.zeros_like(acc_scr)

        def step(masked):
            s = lax.dot_general(q_ref[...], k_ref[...], dims_nt, preferred_element_type=F32) * ATT_SCALE
            if masked:
                s = _att_mask(s, i, j, tq, tk)
            m_new = jnp.maximum(m_scr[...], jnp.max(s, axis=-1, keepdims=True))
            alpha = jnp.exp(m_scr[...] - m_new)
            p = jnp.exp(s - m_new)
            l_scr[...] = alpha * l_scr[...] + jnp.sum(p, axis=-1, keepdims=True)
            acc_scr[...] = alpha * acc_scr[...] + jnp.dot(p.astype(BF16), v_ref[...], preferred_element_type=F32)
            m_scr[...] = m_new

        pl.when(j < i // ratio)(functools.partial(step, False))

        @pl.when(j == i // ratio)
        def _():
            step(True)
            o_ref[...] = acc_scr[...] / l_scr[...]
            lse_ref[0] = m_scr[...] + jnp.log(l_scr[...])

        if n_c:
            pl.when(last)(finish)

    outs = pl.pallas_call(
        body, grid=(h, nq, nk),
        in_specs=[pl.BlockSpec((tq, ATT_DQK), lambda hh, i, j: (i, hh)),
                  pl.BlockSpec((tk, ATT_DQK), lambda hh, i, j: (jnp.minimum(j, i // ratio), hh)),
                  pl.BlockSpec((tk, dv), lambda hh, i, j: (jnp.minimum(j, i // ratio), hh))] + [ANY] * n_c,
        out_specs=[pl.BlockSpec((tq, dv), lambda hh, i, j: (i, hh)),
                   pl.BlockSpec((1, tq, 1), lambda hh, i, j: (hh, i, 0))] + [ANY] * n_c,
        out_shape=[jax.ShapeDtypeStruct((t, h * dv), F32), jax.ShapeDtypeStruct((h, t, 1), F32)]
        + [jax.ShapeDtypeStruct((N_DEV,) + s.shape, s.dtype) for s in shards],
        scratch_shapes=[pltpu.VMEM((tq, 1), F32), pltpu.VMEM((tq, 1), F32), pltpu.VMEM((tq, dv), F32)]
        + (_comm_scratch(n_c) if n_c else []),
        compiler_params=_cparams("arbitrary", "arbitrary", "arbitrary"), name="att_fwd")(q, k, v, *shards)
    return outs[0], outs[1], tuple(outs[2:])


def _att_dq_call(q, k, v, o, lse, do, grads=()):
    t = q.shape[0]
    h = q.shape[1] // ATT_DQK
    dv = v.shape[1] // h
    tq, tk, ratio = _att_tiles(t)
    nq, nk = t // tq, t // tk
    dims_nt = (((1,), (1,)), ((), ()))
    n_c = len(grads)

    def body(q_ref, k_ref, v_ref, o_ref, lse_ref, do_ref, *rest):
        x_refs, rest = rest[:n_c], rest[n_c:]
        dq_ref, delta_ref = rest[:2]
        p_refs, rest = rest[2:2 + n_c], rest[2 + n_c:]
        acc_scr, d_scr = rest[:2]
        i, j = pl.program_id(1), pl.program_id(2)
        if n_c:
            start, finish = _exchange_phases(x_refs, p_refs, *rest[2:])
            first, _, last = _grid_marks(h, nq, nk)
            pl.when(first)(start)

        @pl.when(j == 0)
        def _():
            acc_scr[...] = jnp.zeros_like(acc_scr)
            d_scr[...] = jnp.sum(do_ref[...] * o_ref[...], axis=-1, keepdims=True)

        def step(masked):
            s = lax.dot_general(q_ref[...], k_ref[...], dims_nt, preferred_element_type=F32) * ATT_SCALE
            if masked:
                s = _att_mask(s, i, j, tq, tk)
            p = jnp.exp(s - lse_ref[0])
            dp = lax.dot_general(do_ref[...].astype(BF16), v_ref[...], dims_nt, preferred_element_type=F32)
            ds = p * (dp - d_scr[...]) * ATT_SCALE
            acc_scr[...] += jnp.dot(ds.astype(BF16), k_ref[...], preferred_element_type=F32)

        pl.when(j < i // ratio)(functools.partial(step, False))

        @pl.when(j == i // ratio)
        def _():
            step(True)
            dq_ref[...] = acc_scr[...].astype(dq_ref.dtype)
            delta_ref[0] = d_scr[...]

        if n_c:
            pl.when(last)(finish)

    outs = pl.pallas_call(
        body, grid=(h, nq, nk),
        in_specs=[pl.BlockSpec((tq, ATT_DQK), lambda hh, i, j: (i, hh)),
                  pl.BlockSpec((tk, ATT_DQK), lambda hh, i, j: (jnp.minimum(j, i // ratio), hh)),
                  pl.BlockSpec((tk, dv), lambda hh, i, j: (jnp.minimum(j, i // ratio), hh)),
                  pl.BlockSpec((tq, dv), lambda hh, i, j: (i, hh)),
                  pl.BlockSpec((1, tq, 1), lambda hh, i, j: (hh, i, 0)),
                  pl.BlockSpec((tq, dv), lambda hh, i, j: (i, hh))] + [ANY] * n_c,
        out_specs=[pl.BlockSpec((tq, ATT_DQK), lambda hh, i, j: (i, hh)),
                   pl.BlockSpec((1, tq, 1), lambda hh, i, j: (hh, i, 0))] + [ANY] * n_c,
        out_shape=[jax.ShapeDtypeStruct(q.shape, q.dtype), jax.ShapeDtypeStruct((h, t, 1), F32)]
        + [jax.ShapeDtypeStruct(g.shape, g.dtype) for g in grads],
        scratch_shapes=[pltpu.VMEM((tq, ATT_DQK), F32), pltpu.VMEM((tq, 1), F32)]
        + (_comm_scratch(n_c) if n_c else []),
        compiler_params=_cparams("arbitrary", "arbitrary", "arbitrary"), name="att_dq")(q, k, v, o, lse, do, *grads)
    return outs[0], outs[1], tuple(outs[2:])


def _att_dkv_call(q, k, v, lse, delta, do, grads=()):
    t = q.shape[0]
    h = q.shape[1] // ATT_DQK
    dv = v.shape[1] // h
    tq, tk, ratio = _att_tiles(t)
    nq, nk = t // tq, t // tk
    dims_nt = (((1,), (1,)), ((), ()))
    dims_tn = (((0,), (0,)), ((), ()))
    n_c = len(grads)

    def body(q_ref, k_ref, v_ref, lse_ref, delta_ref, do_ref, *rest):
        x_refs, rest = rest[:n_c], rest[n_c:]
        dk_ref, dv_ref = rest[:2]
        p_refs, rest = rest[2:2 + n_c], rest[2 + n_c:]
        dk_scr, dv_scr = rest[:2]
        j, i = pl.program_id(1), pl.program_id(2)
        if n_c:
            start, finish = _exchange_phases(x_refs, p_refs, *rest[2:])
            first, _, last = _grid_marks(h, nk, nq)
            pl.when(first)(start)

        @pl.when(i == 0)
        def _():
            dk_scr[...] = jnp.zeros_like(dk_scr)
            dv_scr[...] = jnp.zeros_like(dv_scr)

        def step(masked):
            s = lax.dot_general(q_ref[...], k_ref[...], dims_nt, preferred_element_type=F32) * ATT_SCALE
            if masked:
                s = _att_mask(s, i, j, tq, tk)
            p = jnp.exp(s - lse_ref[0])
            do_b = do_ref[...].astype(BF16)
            dv_scr[...] += lax.dot_general(p.astype(BF16), do_b, dims_tn, preferred_element_type=F32)
            dp = lax.dot_general(do_b, v_ref[...], dims_nt, preferred_element_type=F32)
            ds = p * (dp - delta_ref[0]) * ATT_SCALE
            dk_scr[...] += lax.dot_general(ds.astype(BF16), q_ref[...], dims_tn, preferred_element_type=F32)

        pl.when(i // ratio > j)(functools.partial(step, False))
        pl.when(i // ratio == j)(functools.partial(step, True))

        @pl.when(i == nq - 1)
        def _():
            dk_ref[...] = dk_scr[...].astype(dk_ref.dtype)
            dv_ref[...] = dv_scr[...].astype(dv_ref.dtype)

        if n_c:
            pl.when(last)(finish)

    def qi(i, j):
        return jnp.maximum(i, j * ratio)

    outs = pl.pallas_call(
        body, grid=(h, nk, nq),
        in_specs=[pl.BlockSpec((tq, ATT_DQK), lambda hh, j, i: (qi(i, j), hh)),
                  pl.BlockSpec((tk, ATT_DQK), lambda hh, j, i: (j, hh)),
                  pl.BlockSpec((tk, dv), lambda hh, j, i: (j, hh)),
                  pl.BlockSpec((1, tq, 1), lambda hh, j, i: (hh, qi(i, j), 0)),
                  pl.BlockSpec((1, tq, 1), lambda hh, j, i: (hh, qi(i, j), 0)),
                  pl.BlockSpec((tq, dv), lambda hh, j, i: (qi(i, j), hh))] + [ANY] * n_c,
        out_specs=[pl.BlockSpec((tk, ATT_DQK), lambda hh, j, i: (j, hh)),
                   pl.BlockSpec((tk, dv), lambda hh, j, i: (j, hh))] + [ANY] * n_c,
        out_shape=[jax.ShapeDtypeStruct(k.shape, k.dtype), jax.ShapeDtypeStruct(v.shape, v.dtype)]
        + [jax.ShapeDtypeStruct(g.shape, g.dtype) for g in grads],
        scratch_shapes=[pltpu.VMEM((tk, ATT_DQK), F32), pltpu.VMEM((tk, dv), F32)]
        + (_comm_scratch(n_c) if n_c else []),
        compiler_params=_cparams("arbitrary", "arbitrary", "arbitrary"), name="att_dkv")(
            q, k, v, lse, delta, do, *grads)
    return outs[0], outs[1], tuple(outs[2:])


@jax.custom_vjp
def attention(q, k, v, shards):
    o, _, gathered = _att_fwd_call(q, k, v, tuple(s.astype(BF16) for s in shards))
    return o, gathered


def _att_vjp_fwd(q, k, v, shards):
    o, lse, gathered = _att_fwd_call(q, k, v, tuple(s.astype(BF16) for s in shards))
    return (o, gathered), (q, k, v, o, lse)


def _att_vjp_bwd(res, cts):
    q, k, v, o, lse = res
    do, g_gathered = cts
    sizes = [g.size for g in g_gathered]
    cut = 0
    while cut < len(sizes) and 2 * sum(sizes[:cut + 1]) <= sum(sizes):
        cut += 1
    dq, delta, parts_a = _att_dq_call(q, k, v, o, lse, do, tuple(g_gathered[:cut]))
    dk, dv, parts_b = _att_dkv_call(q, k, v, lse, delta, do, tuple(g_gathered[cut:]))
    return dq, dk, dv, tuple(sum_parts(p) for p in parts_a + parts_b)


attention.defvjp(_att_vjp_fwd, _att_vjp_bwd)


MID = CHUNK // 2 - 1


def _tril(n):
    r = lax.broadcasted_iota(jnp.int32, (n, n), 0)
    c = lax.broadcasted_iota(jnp.int32, (n, n), 1)
    return c <= r


def _bdot(a, b, dims):
    return lax.dot_general(a.astype(BF16), b.astype(BF16), dims, preferred_element_type=F32)


NN = (((1,), (0,)), ((), ()))
NT = (((1,), (1,)), ((), ()))
TN = (((0,), (0,)), ((), ()))


def _hgrn_chunk(state, q_in, f_in, i_in, lb):
    tril = _tril(CHUNK)
    f = lb + (1.0 - lb) * jax.nn.sigmoid(f_in)
    logf = jnp.log(f)
    b = jnp.dot(tril.astype(F32), logf, precision=HI)
    q = _silu(q_in) * HG_DK ** -0.5
    k = 1.0 - f
    b_mid = b[MID:MID + 1, :]
    att = _bdot(q * jnp.exp(b - b_mid), k * jnp.exp(b_mid - b), NT)
    att = jnp.where(tril, att, 0.0)
    o = _bdot(q * jnp.exp(b), state, NT) + _bdot(att, i_in, NN)
    b_last = b[CHUNK - 1:CHUNK, :]
    new_state = jnp.exp(b_last) * state + _bdot(i_in, k * jnp.exp(b_last - b), TN)
    return o, new_state


HG_STEP_CHUNKS = 4


def _hgrn_step_rows(t):
    n = HG_STEP_CHUNKS
    while (t // CHUNK) % n:
        n //= 2
    return n * CHUNK


def _hgrn_chunks(state, q_in, f_in, i_in, lb):
    outs = []
    for c in range(q_in.shape[0] // CHUNK):
        rows = slice(c * CHUNK, (c + 1) * CHUNK)
        o, state = _hgrn_chunk(state, q_in[rows], f_in[rows], i_in[rows], lb)
        outs.append(o)
    return (outs[0] if len(outs) == 1 else jnp.concatenate(outs, axis=0)), state


def _hgrn_fwd_call(q, f, i, lb):
    t = q.shape[0]
    rows = _hgrn_step_rows(t)
    nc = t // rows
    blk = pl.BlockSpec((rows, HG_DK), lambda h, c: (c, h))

    def body(q_ref, f_ref, i_ref, lb_ref, o_ref, s_ref, s_scr):
        @pl.when(pl.program_id(1) == 0)
        def _():
            s_scr[...] = jnp.zeros_like(s_scr)

        s_ref[0, 0] = s_scr[...]
        o, ns = _hgrn_chunks(s_scr[...], q_ref[...], f_ref[...], i_ref[...], lb_ref[...])
        o_ref[...] = o
        s_scr[...] = ns

    return pl.pallas_call(
        body, grid=(HG_HEADS, nc),
        in_specs=[blk, blk, blk, pl.BlockSpec((1, HG_DK), lambda h, c: (0, h))],
        out_specs=[blk, pl.BlockSpec((1, 1, HG_DK, HG_DK), lambda h, c: (h, c, 0, 0))],
        out_shape=[jax.ShapeDtypeStruct((t, HG_WIDTH), F32), jax.ShapeDtypeStruct((HG_HEADS, nc, HG_DK, HG_DK), F32)],
        scratch_shapes=[pltpu.VMEM((HG_DK, HG_DK), F32)],
        compiler_params=_cparams("parallel", "arbitrary"), name="hgrn_fwd")(q, f, i, lb)


def _hgrn_bwd_call(q, f, i, lb, states, do):
    t = q.shape[0]
    rows = _hgrn_step_rows(t)
    nc = t // rows
    blk = pl.BlockSpec((rows, HG_DK), lambda h, c: (nc - 1 - c, h))
    row = pl.BlockSpec((1, HG_DK), lambda h, c: (0, h))

    def body(q_ref, f_ref, i_ref, lb_ref, s_ref, do_ref, dq_ref, df_ref, di_ref, dlb_ref, ds_scr):
        c = pl.program_id(1)

        @pl.when(c == 0)
        def _():
            ds_scr[...] = jnp.zeros_like(ds_scr)

        _, vjp = jax.vjp(_hgrn_chunks, s_ref[0, 0], q_ref[...], f_ref[...], i_ref[...], lb_ref[...])
        ds, dq, df, di, dlb = vjp((do_ref[...], ds_scr[...]))
        ds_scr[...] = ds
        dq_ref[...] = dq
        df_ref[...] = df
        di_ref[...] = di

        @pl.when(c == 0)
        def _():
            dlb_ref[...] = dlb

        @pl.when(c != 0)
        def _():
            dlb_ref[...] += dlb

    return pl.pallas_call(
        body, grid=(HG_HEADS, nc),
        in_specs=[blk, blk, blk, row, pl.BlockSpec((1, 1, HG_DK, HG_DK), lambda h, c: (h, nc - 1 - c, 0, 0)), blk],
        out_specs=[blk, blk, blk, row],
        out_shape=[jax.ShapeDtypeStruct((t, HG_WIDTH), F32)] * 3 + [jax.ShapeDtypeStruct((1, HG_WIDTH), F32)],
        scratch_shapes=[pltpu.VMEM((HG_DK, HG_DK), F32)],
        compiler_params=_cparams("parallel", "arbitrary"), name="hgrn_bwd")(q, f, i, lb, states, do)


@jax.custom_vjp
def hgrn_scan(q, f, i, lb):
    return _hgrn_fwd_call(q, f, i, lb)[0]


def _hgrn_vjp_fwd(q, f, i, lb):
    o, states = _hgrn_fwd_call(q, f, i, lb)
    return o, (q, f, i, lb, states)


def _hgrn_vjp_bwd(res, do):
    return tuple(_hgrn_bwd_call(*res, do))


hgrn_scan.defvjp(_hgrn_vjp_fwd, _hgrn_vjp_bwd)


def _ssd_chunk(state, xs, bm, cm, dtx, alog, dskip):
    assert CHUNK == SSM_HEADDIM and 2 * SSM_HEADDIM == LANES
    gw = xs.shape[1]
    trilf = _tril(CHUNK).astype(F32)
    da = dtx * (-jnp.exp(alog))
    a = jnp.dot(trilf, da, precision=HI)
    xdt = xs * dtx
    cb2 = _bdot(cm, jnp.concatenate([bm, bm], axis=0), NT)
    row = lax.broadcasted_iota(jnp.int32, (CHUNK, LANES), 0)
    src = lax.broadcasted_iota(jnp.int32, (CHUNK, LANES), 1) % CHUNK
    first_head = lax.broadcasted_iota(jnp.int32, (CHUNK, LANES), 1) < CHUNK
    ys = []
    for p in range(gw // LANES):
        lanes = slice(p * LANES, (p + 1) * LANES)
        a_src = jnp.sum(jnp.where(row <= src, da[:, lanes], 0.0), axis=0, keepdims=True)
        decay_ls = jnp.exp(jnp.where(src <= row, a[:, lanes] - a_src, NEG))
        x_pair = xdt[:, lanes]
        rhs = jnp.concatenate([jnp.where(first_head, x_pair, 0.0), jnp.where(first_head, 0.0, x_pair)], axis=0)
        ys.append(_bdot(cb2 * decay_ls, rhs, NN))
    y_diag = ys[0] if len(ys) == 1 else jnp.concatenate(ys, axis=1)
    y_off = jnp.exp(a) * _bdot(cm, state, NN)
    y = y_diag + y_off + xs * dskip
    a_last = a[CHUNK - 1:CHUNK, :]
    new_state = jnp.exp(a_last) * state + _bdot(bm, jnp.exp(a_last - a) * xdt, TN)
    return y, new_state


SSD_STEP_CHUNKS = 2


def _ssd_step_rows(t):
    n = SSD_STEP_CHUNKS
    while (t // CHUNK) % n:
        n //= 2
    return n * CHUNK


def _ssd_chunks(state, xs, bm, cm, dtx, alog, dskip):
    outs = []
    for c in range(xs.shape[0] // CHUNK):
        rows = slice(c * CHUNK, (c + 1) * CHUNK)
        y, state = _ssd_chunk(state, xs[rows], bm[rows], cm[rows], dtx[rows], alog, dskip)
        outs.append(y)
    return (outs[0] if len(outs) == 1 else jnp.concatenate(outs, axis=0)), state


def _ssd_specs(inner, rows, nc, rev):
    gw = inner // SSM_GROUPS
    nb = inner // SSM_STATE
    ci = (lambda c: nc - 1 - c) if rev else (lambda c: c)
    xs = pl.BlockSpec((rows, gw), lambda g, c: (ci(c), g))
    bm = pl.BlockSpec((rows, SSM_STATE), lambda g, c: (ci(c), nb + g))
    cm = pl.BlockSpec((rows, SSM_STATE), lambda g, c: (ci(c), nb + SSM_GROUPS + g))
    row = pl.BlockSpec((1, gw), lambda g, c: (0, g))
    st = pl.BlockSpec((1, 1, SSM_STATE, gw), lambda g, c: (g, ci(c), 0, 0))
    return gw, xs, bm, cm, row, st


def _ssd_fwd_call(xbc, dtx, alog, dskip):
    t, inner = dtx.shape
    rows = _ssd_step_rows(t)
    nc = t // rows
    gw, xs_s, bm_s, cm_s, row, st = _ssd_specs(inner, rows, nc, False)

    def body(xs_ref, bm_ref, cm_ref, dt_ref, a_ref, d_ref, y_ref, s_ref, s_scr):
        @pl.when(pl.program_id(1) == 0)
        def _():
            s_scr[...] = jnp.zeros_like(s_scr)

        s_ref[0, 0] = s_scr[...]
        y, ns = _ssd_chunks(s_scr[...], xs_ref[...], bm_ref[...], cm_ref[...], dt_ref[...], a_ref[...], d_ref[...])
        y_ref[...] = y
        s_scr[...] = ns

    return pl.pallas_call(
        body, grid=(SSM_GROUPS, nc), in_specs=[xs_s, bm_s, cm_s, xs_s, row, row],
        out_specs=[xs_s, st],
        out_shape=[jax.ShapeDtypeStruct((t, inner), F32), jax.ShapeDtypeStruct((SSM_GROUPS, nc, SSM_STATE, gw), F32)],
        scratch_shapes=[pltpu.VMEM((SSM_STATE, gw), F32)],
        compiler_params=_cparams("parallel", "arbitrary"), name="ssd_fwd")(xbc, xbc, xbc, dtx, alog, dskip)


def _ssd_bwd_call(xbc, dtx, alog, dskip, states, dy):
    t, inner = dtx.shape
    rows = _ssd_step_rows(t)
    nc = t // rows
    gw, xs_s, bm_s, cm_s, row, st = _ssd_specs(inner, rows, nc, True)
    gn = pl.BlockSpec((rows, SSM_STATE), lambda g, c: (nc - 1 - c, g))
    gn_shape = jax.ShapeDtypeStruct((t, SSM_GROUPS * SSM_STATE), F32)

    def body(xs_ref, bm_ref, cm_ref, dt_ref, a_ref, d_ref, s_ref, dy_ref,
             dxs_ref, dbm_ref, dcm_ref, ddt_ref, da_ref, dd_ref, ds_scr):
        c = pl.program_id(1)

        @pl.when(c == 0)
        def _():
            ds_scr[...] = jnp.zeros_like(ds_scr)

        _, vjp = jax.vjp(_ssd_chunks, s_ref[0, 0], xs_ref[...], bm_ref[...], cm_ref[...], dt_ref[...],
                         a_ref[...], d_ref[...])
        ds, dxs, dbm, dcm, ddt, da, dd = vjp((dy_ref[...], ds_scr[...]))
        ds_scr[...] = ds
        dxs_ref[...] = dxs
        dbm_ref[...] = dbm
        dcm_ref[...] = dcm
        ddt_ref[...] = ddt

        @pl.when(c == 0)
        def _():
            da_ref[...] = da
            dd_ref[...] = dd

        @pl.when(c != 0)
        def _():
            da_ref[...] += da
            dd_ref[...] += dd

    big = jax.ShapeDtypeStruct((t, inner), F32)
    small = jax.ShapeDtypeStruct((1, inner), F32)
    return pl.pallas_call(
        body, grid=(SSM_GROUPS, nc), in_specs=[xs_s, bm_s, cm_s, xs_s, row, row, st, xs_s],
        out_specs=[xs_s, gn, gn, xs_s, row, row],
        out_shape=[big, gn_shape, gn_shape, big, small, small],
        scratch_shapes=[pltpu.VMEM((SSM_STATE, gw), F32)],
        compiler_params=_cparams("parallel", "arbitrary"), name="ssd_bwd")(xbc, xbc, xbc, dtx, alog, dskip, states, dy)


@jax.custom_vjp
def ssd_scan(xbc, dtx, alog, dskip):
    return _ssd_fwd_call(xbc, dtx, alog, dskip)[0]


def _ssd_vjp_fwd(xbc, dtx, alog, dskip):
    y, states = _ssd_fwd_call(xbc, dtx, alog, dskip)
    return y, (xbc, dtx, alog, dskip, states)


def _ssd_vjp_bwd(res, dy):
    dxs, dbm, dcm, ddt, da, dd = _ssd_bwd_call(*res, dy)
    return jnp.concatenate([dxs, dbm, dcm], axis=1), ddt, da, dd


ssd_scan.defvjp(_ssd_vjp_fwd, _ssd_vjp_bwd)


ANY = pl.BlockSpec(memory_space=pl.ANY)


def _my_pos():
    return lax.axis_index("x"), lax.axis_index("y"), lax.axis_index("c")


def _comm_scratch(n):
    return [pltpu.SemaphoreType.DMA((n, N_DEV - 1)), pltpu.SemaphoreType.DMA((n, N_DEV - 1)),
            pltpu.SemaphoreType.DMA((n,))]


def _gather_phases(x_refs, out_refs, send_sems, recv_sems, local_sems):
    x, y, c = _my_pos()
    me, sibling = (x, y, c), (x, y, 1 - c)
    chips = [(1 - x, y), (x, 1 - y), (1 - x, 1 - y)]

    def slot(t, px, py, pc):
        return out_refs[t].at[4 * px + 2 * py + pc]

    def copy(t, k, block, to, own=False):
        return pltpu.make_async_remote_copy(
            src_ref=x_refs[t] if own else slot(t, *block), dst_ref=slot(t, *block),
            send_sem=send_sems.at[t, k], recv_sem=recv_sems.at[t, k], device_id=to, device_id_type=MESH)

    def mine(t):
        return pltpu.make_async_copy(x_refs[t], slot(t, *me), local_sems.at[t])

    def first(t):
        return [copy(t, 0, me, sibling, own=True)] + [copy(t, 1 + j, me, (*chip, c), own=True)
                                                      for j, chip in enumerate(chips)]

    def passed(t):
        return [copy(t, 4 + j, (*chip, c), sibling) for j, chip in enumerate(chips)]

    def start():
        for t in range(len(x_refs)):
            mine(t).start()
            for cp in first(t):
                cp.start()

    def middle():
        for t in range(len(x_refs)):
            for j, chip in enumerate(chips):
                copy(t, 1 + j, (*chip, c), me).wait_recv()
                copy(t, 4 + j, (*chip, c), sibling).start()

    def finish():
        for t in range(len(x_refs)):
            copy(t, 0, sibling, me).wait_recv()
            for j, chip in enumerate(chips):
                copy(t, 4 + j, (*chip, 1 - c), me).wait_recv()
            for cp in first(t) + passed(t):
                cp.wait_send()
            mine(t).wait()

    return start, middle, finish


def _exchange_phases(x_refs, out_refs, send_sems, recv_sems, local_sems):
    x, y, c = _my_pos()
    me = 4 * x + 2 * y + c

    def local(t):
        return pltpu.make_async_copy(x_refs[t].at[me], out_refs[t].at[me], local_sems.at[t])

    def copies(t):
        out = []
        for k in range(1, N_DEV):
            px = 1 - x if k & 4 else x
            py = 1 - y if k & 2 else y
            pc = 1 - c if k & 1 else c
            out.append(pltpu.make_async_remote_copy(
                src_ref=x_refs[t].at[4 * px + 2 * py + pc], dst_ref=out_refs[t].at[me],
                send_sem=send_sems.at[t, k - 1], recv_sem=recv_sems.at[t, k - 1],
                device_id=(px, py, pc), device_id_type=MESH))
        return out

    def start():
        for t in range(len(x_refs)):
            local(t).start()
            for cp in copies(t):
                cp.start()

    def finish():
        for t in range(len(x_refs)):
            for cp in copies(t):
                cp.wait_recv()
            for cp in copies(t):
                cp.wait_send()
            local(t).wait()

    return start, finish


def all_gather(x_shard):
    def body(x_ref, out_ref, send_sems, recv_sems, local_sems):
        start, middle, finish = _gather_phases([x_ref], [out_ref], send_sems, recv_sems, local_sems)
        start()
        middle()
        finish()

    return pl.pallas_call(
        body, out_shape=jax.ShapeDtypeStruct((N_DEV,) + x_shard.shape, x_shard.dtype), in_specs=[ANY],
        out_specs=ANY, scratch_shapes=_comm_scratch(1), name="all_gather")(x_shard)


def exchange(x):
    def body(x_ref, out_ref, send_sems, recv_sems, local_sems):
        start, finish = _exchange_phases([x_ref], [out_ref], send_sems, recv_sems, local_sems)
        start()
        finish()

    return pl.pallas_call(
        body, out_shape=jax.ShapeDtypeStruct(x.shape, x.dtype), in_specs=[ANY], out_specs=ANY,
        scratch_shapes=_comm_scratch(1), name="exchange")(x)


def sum_parts(parts):
    p, r, c_ = parts.shape
    br = _row_block(r, [c_]) if r % 16 == 0 else r

    def body(p_ref, o_ref):
        acc = p_ref[0].astype(F32)
        for i in range(1, p):
            acc = acc + p_ref[i].astype(F32)
        o_ref[...] = acc

    return pl.pallas_call(
        body, grid=(r // br,), in_specs=[pl.BlockSpec((p, br, c_), lambda i: (0, i, 0))],
        out_specs=pl.BlockSpec((br, c_), lambda i: (i, 0)), out_shape=jax.ShapeDtypeStruct((r, c_), F32),
        compiler_params=_cparams("parallel"), name="sum_parts")(parts)


@jax.custom_vjp
def gather_op(shard):
    return all_gather(shard.astype(BF16))


def _gather_op_fwd(shard):
    return all_gather(shard.astype(BF16)), None


def _gather_op_bwd(_, g):
    return (sum_parts(_in_chunks(exchange, g, 1)),)


gather_op.defvjp(_gather_op_fwd, _gather_op_bwd)


def _in_chunks(fn, arr, axis):
    rows = arr.shape[axis]
    pieces = 1
    while (arr.size * arr.dtype.itemsize) // pieces > COMM_BYTES and rows % (2 * pieces) == 0:
        pieces *= 2
    step = rows // pieces
    outs = [fn(lax.slice_in_dim(arr, s, s + step, axis=axis)) for s in range(0, rows, step)]
    return outs[0] if len(outs) == 1 else jnp.concatenate(outs, axis=1)


def reduce_adamw(parts, w, m, v):
    p, r, c_ = parts.shape
    br = _row_block(r, [c_]) if r % 16 == 0 else r
    c1 = 1.0 - ADAM_B1 ** ADAM_STEP
    c2 = 1.0 - ADAM_B2 ** ADAM_STEP

    def body(p_ref, w_ref, m_ref, v_ref, g_ref, d_ref, m2_ref, v2_ref):
        g = p_ref[0].astype(F32)
        for i in range(1, p):
            g = g + p_ref[i].astype(F32)
        m2 = ADAM_B1 * m_ref[...] + (1.0 - ADAM_B1) * g
        v2 = ADAM_B2 * v_ref[...] + (1.0 - ADAM_B2) * (g * g)
        g_ref[...] = g
        m2_ref[...] = m2
        v2_ref[...] = v2
        d_ref[...] = -ADAM_LR * ((m2 / c1) / (jnp.sqrt(v2 / c2) + ADAM_EPS) + ADAM_WD * w_ref[...])

    blk = pl.BlockSpec((br, c_), lambda i: (i, 0))
    return pl.pallas_call(
        body, grid=(r // br,), in_specs=[pl.BlockSpec((p, br, c_), lambda i: (0, i, 0)), blk, blk, blk],
        out_specs=[blk] * 4, out_shape=[jax.ShapeDtypeStruct((r, c_), F32)] * 4,
        compiler_params=_cparams("parallel"), name="reduce_adamw")(parts, w, m, v)


WEIGHTS = ['ffn1_norm', 'ffn1_wi', 'ffn1_wo', 'mix_norm', 'w_in', 'mla_q_norm', 'mla_w_uq', 'mla_kv_norm',
           'mla_w_ukv', 'hgrn_lb_logits', 'hgrn_norm', 'ssm_conv_w', 'ssm_conv_b', 'ssm_a_log', 'ssm_dt_bias',
           'ssm_d', 'ssm_norm', 'w_o_mla', 'w_o_hgrn', 'w_o_ssm', 'w_out', 'ffn2_norm', 'ffn2_wi', 'ffn2_wo',
           'final_norm']
COL_SHARDED = ('ffn1_wi', 'w_in', 'mla_w_uq', 'mla_w_ukv', 'ffn2_wi')
ROW_SHARDED = ('ffn1_wo', 'w_o_mla', 'w_o_hgrn', 'w_o_ssm', 'w_out', 'ffn2_wo')
BIG = tuple(n for n in WEIGHTS if n in COL_SHARDED + ROW_SHARDED)
PRE = ('ffn1_wi', 'ffn1_wo', 'w_in', 'mla_w_uq', 'mla_w_ukv')
POST = ('w_o_mla', 'w_o_hgrn', 'w_o_ssm', 'w_out', 'ffn2_wi', 'ffn2_wo')
CONV_W = 'ssm_conv_w'
REPLICATED = tuple(n for n in WEIGHTS if n not in BIG and n != CONV_W)


def _segment_plan(n, sizes):
    plan, off = [], 0
    for s in sizes:
        a, b = off, off + s
        plan.append([(j, max(a, j * n) - j * n, min(b, (j + 1) * n) - j * n)
                     for j in range(a // n, (b - 1) // n + 1)])
        off = b
    return plan


@functools.partial(jax.custom_vjp, nondiff_argnums=(1,))
def col_segments(blocks, sizes):
    outs = []
    for pieces in _segment_plan(blocks.shape[-1], sizes):
        cut = [blocks[j][:, lo:hi] for j, lo, hi in pieces]
        outs.append(cut[0] if len(cut) == 1 else jnp.concatenate(cut, axis=1))
    return tuple(outs)


def _col_segments_fwd(blocks, sizes):
    return col_segments(blocks, sizes), blocks.shape[-1]


def _col_segments_bwd(sizes, n, gs):
    per_block = [[] for _ in range(N_DEV)]
    for g, pieces in zip(gs, _segment_plan(n, sizes)):
        off = 0
        for j, lo, hi in pieces:
            per_block[j].append(g[:, off:off + hi - lo])
            off += hi - lo
    return (jnp.stack([p[0] if len(p) == 1 else jnp.concatenate(p, axis=1) for p in per_block]),)


col_segments.defvjp(_col_segments_fwd, _col_segments_bwd)


def _rope_tables(t):
    half = MLA_ROPE // 2
    inv = 1.0 / (ROPE_THETA ** (jnp.arange(0, MLA_ROPE, 2, dtype=F32) / MLA_ROPE))
    ang = jnp.arange(t, dtype=F32)[:, None] * inv[None, :]
    reps = LANES // half
    return jnp.tile(jnp.cos(ang), (1, reps)), jnp.tile(jnp.sin(ang), (1, reps))


def _ffn(x, norm, wi, wo):
    dff = wo.shape[0]
    h = rmsnorm(norm[None], x)[0]
    wg, wu = col_segments(wi, (dff, dff))
    return x + 0.5 * mm(swiglu(mm(h, wg), mm(h, wu))[0], wo)


def _per_head(w, widths, pad_to):
    k = w.shape[0]
    w3 = w.reshape(k, -1, sum(widths))
    outs, off = [], 0
    for wd in widths:
        part = w3[:, :, off:off + wd]
        if wd < pad_to:
            part = jnp.pad(part, ((0, 0), (0, 0), (0, pad_to - wd)))
        outs.append(part.reshape(k, -1))
        off += wd
    return outs


def _layer(x, p, lb, cos, sin, carried):
    t, d = x.shape
    inner = 2 * d
    conv_dim = inner + 2 * SSM_GROUPS * SSM_STATE
    n_ssm_heads = inner // SSM_HEADDIM
    x = _ffn(x, p['ffn1_norm'], p['ffn1_wi'], p['ffn1_wo'])

    h = rmsnorm(p['mix_norm'][None], x)[0]
    sizes = (MLA_Q_RANK, MLA_KV_RANK, MLA_ROPE, HG_WIDTH, HG_WIDTH, HG_WIDTH, HG_WIDTH,
             inner, conv_dim, n_ssm_heads, d, d, d)
    (w_q, w_kv, w_kpe, w_hq, w_hf, w_hi, w_hg, w_z, w_xbc, w_dt, w_ga, w_gb, w_gc) = col_segments(p['w_in'], sizes)

    qn = rmsnorm(p['mla_q_norm'][None], mm(h, w_q))[0]
    kvn = rmsnorm(p['mla_kv_norm'][None], mm(h, w_kv))[0]
    w_uq, = col_segments(p['mla_w_uq'], (N_DEV * p['mla_w_uq'].shape[-1],))
    w_ukv, = col_segments(p['mla_w_ukv'], (N_DEV * p['mla_w_ukv'].shape[-1],))
    wq_nope, wq_pe = _per_head(w_uq, (MLA_NOPE, MLA_ROPE), LANES)
    wk_nope, wv = _per_head(w_ukv, (MLA_NOPE, MLA_V), LANES)
    q_nope = mm(qn, wq_nope)
    q_pe = rope(mm(qn, wq_pe), cos, sin)[0]
    k_nope = mm(kvn, wk_nope)
    v = mm(kvn, wv)
    k_rot = rope(mm(h, jnp.pad(w_kpe, ((0, 0), (0, LANES - MLA_ROPE)))), cos, sin)[0]
    q = jnp.concatenate([q_nope.reshape(t, MLA_HEADS, LANES), q_pe.reshape(t, MLA_HEADS, LANES)], axis=2)
    k = jnp.concatenate([k_nope.reshape(t, MLA_HEADS, LANES),
                         jnp.broadcast_to(k_rot[:, None, :], (t, MLA_HEADS, LANES))], axis=2)
    names = tuple(carried)
    o, blocks = attention(q.reshape(t, -1).astype(BF16), k.reshape(t, -1).astype(BF16), v.astype(BF16),
                          tuple(carried[n] for n in names))
    arrived = dict(zip(names, blocks))
    p = dict(p)
    for n in POST:
        p[n] = _usable(n, arrived[(n, 0)])
    y_a = mm(o, p['w_o_mla'])

    o = hgrn_scan(mm(h, w_hq), mm(h, w_hf), mm(h, w_hi), lb[None])
    o = hgrn_out(p['hgrn_norm'][None], o, mm(h, w_hg))[0]
    y_b = mm(o, p['w_o_hgrn'])

    xbc = silu_op(conv(p['ssm_conv_w'], p['ssm_conv_b'][None], mm(h, w_xbc)))[0]
    dtx = dt_expand(p['ssm_dt_bias'][None], mm(h, w_dt))[0]
    y = ssd_scan(xbc, dtx, jnp.repeat(p['ssm_a_log'], SSM_HEADDIM)[None], jnp.repeat(p['ssm_d'], SSM_HEADDIM)[None])
    y = ssm_norm(p['ssm_norm'][None], y, mm(h, w_z))[0]
    y_c = mm(y, p['w_o_ssm'])

    merged = merge(y_a, y_b, y_c, mm(h, w_ga), mm(h, w_gb), mm(h, w_gc))[0]
    x = x + mm(merged, p['w_out'])
    return _ffn(x, p['ffn2_norm'], p['ffn2_wi'], p['ffn2_wo']), arrived


def _usable(name, blocks):
    return blocks.reshape(-1, blocks.shape[-1]) if name in ROW_SHARDED else blocks


def _local_loss(shards, params, x, target):
    depth = params['ffn1_norm'].shape[0]
    cos, sin = _rope_tables(x.shape[0])
    prob = jax.nn.softmax(params['hgrn_lb_logits'], axis=0)
    lower = jnp.cumsum(prob, axis=0) - prob[0:1]
    pre = {n: gather_op(shards[n][0]) for n in PRE}
    for layer in range(depth):
        p = {n: params[n][layer] for n in REPLICATED + (CONV_W,) if n != 'final_norm'}
        for n in PRE:
            p[n] = _usable(n, pre[n])
        carried = {(n, 0): shards[n][layer] for n in POST}
        if layer + 1 < depth:
            carried.update({(n, 1): shards[n][layer + 1] for n in PRE})
        x, arrived = _layer(x, p, lower[layer], cos, sin, carried)
        pre = {n: arrived.get((n, 1)) for n in PRE}
    return jnp.sum(loss_rows(params['final_norm'][None], x, target)[0])


def _pack_vec(arrays, rows):
    flat = jnp.concatenate([a.reshape(-1) for a in arrays])
    return jnp.pad(flat, (0, rows * COMM_COLS - flat.shape[0])).reshape(rows, COMM_COLS)


def _unpack(flat, shapes, lead=()):
    flat = flat.reshape(lead + (-1,))
    outs, off = [], 0
    for s in shapes:
        n = 1
        for dim in s:
            n *= dim
        outs.append(flat[..., off:off + n].reshape(lead + tuple(s)))
        off += n
    return outs


def _round_up(n, m):
    return -(-n // m) * m


def _step(a):
    x = a['x'][0]
    target = a['loss_target'][0]
    me = 4 * lax.axis_index("x") + 2 * lax.axis_index("y") + lax.axis_index("c")

    conv_shape = a[CONV_W].shape
    conv_full_shape = conv_shape[:-1] + (conv_shape[-1] * N_DEV,)
    rep_shapes = [a[n].shape for n in REPLICATED]
    n_small = 1 + sum(a[n].size for n in REPLICATED) + a[CONV_W].size * N_DEV
    small_rows = _round_up(-(-n_small // COMM_COLS), 8)

    params = {}
    conv_blocks = _unpack(all_gather(_pack_vec([a[CONV_W]], small_rows)), [conv_shape], lead=(N_DEV,))[0]
    params[CONV_W] = jnp.moveaxis(conv_blocks, 0, -2).reshape(conv_full_shape)
    for n in REPLICATED:
        params[n] = a[n]
    depth = a['ffn1_norm'].shape[0]
    shards = {n: [a[n][layer] for layer in range(depth)] for n in BIG}

    loss, (gs, gp, gx) = jax.value_and_grad(_local_loss, argnums=(0, 1, 2))(shards, params, x, target)

    big_out = [{}, {}, {}, {}]
    for n in BIG:
        width = a[n].shape[-1]
        grad = jnp.stack(gs[n]).reshape(1, -1, width)
        res = reduce_adamw(grad, *[a[pre + n].reshape(-1, width) for pre in ('', 'm_', 'v_')])
        for kind in range(4):
            big_out[kind][n] = res[kind].reshape(a[n].shape)

    small = _pack_vec([loss.reshape(1)] + [gp[n] for n in REPLICATED] + [gp[CONV_W]], small_rows)
    zero1, one1 = jnp.zeros((1,), F32), jnp.ones((1,), F32)
    zero_c, one_c = jnp.zeros(conv_full_shape, F32), jnp.ones(conv_full_shape, F32)
    small_w = _pack_vec([zero1] + [a[n] for n in REPLICATED] + [zero_c], small_rows)
    small_m = _pack_vec([zero1] + [a['m_' + n] for n in REPLICATED] + [zero_c], small_rows)
    small_v = _pack_vec([one1] + [a['v_' + n] for n in REPLICATED] + [one_c], small_rows)
    res = reduce_adamw(all_gather(small), small_w, small_m, small_v)
    small_out = []
    for r in res:
        pieces = _unpack(r, [(1,)] + rep_shapes + [conv_full_shape])
        small_out.append((pieces[0], dict(zip(REPLICATED, pieces[1:-1])), pieces[-1]))
    total_loss = small_out[0][0][0]

    width = conv_shape[-1]
    g_conv = lax.dynamic_slice_in_dim(small_out[0][2], me * width, width, axis=len(conv_shape) - 1)
    conv_rows = -(-a[CONV_W].size // COMM_COLS)
    conv_res = reduce_adamw(_pack_vec([g_conv], conv_rows)[None],
                            *[_pack_vec([a[pre + CONV_W]], conv_rows) for pre in ('', 'm_', 'v_')])
    conv_out = [_unpack(r, [conv_shape])[0] for r in conv_res]

    outs = [total_loss, gx[None]]
    for kind in range(4):
        for n in WEIGHTS:
            if n in BIG:
                outs.append(big_out[kind][n])
            elif n == CONV_W:
                outs.append(conv_out[kind])
            else:
                outs.append(small_out[kind][1][n])
    return tuple(outs)


def kernel(x, ffn1_norm, ffn1_wi, ffn1_wo, mix_norm, w_in, mla_q_norm, mla_w_uq, mla_kv_norm, mla_w_ukv, hgrn_lb_logits, hgrn_norm, ssm_conv_w, ssm_conv_b, ssm_a_log, ssm_dt_bias, ssm_d, ssm_norm, w_o_mla, w_o_hgrn, w_o_ssm, w_out, ffn2_norm, ffn2_wi, ffn2_wo, final_norm, loss_target, m_ffn1_norm, m_ffn1_wi, m_ffn1_wo, m_mix_norm, m_w_in, m_mla_q_norm, m_mla_w_uq, m_mla_kv_norm, m_mla_w_ukv, m_hgrn_lb_logits, m_hgrn_norm, m_ssm_conv_w, m_ssm_conv_b, m_ssm_a_log, m_ssm_dt_bias, m_ssm_d, m_ssm_norm, m_w_o_mla, m_w_o_hgrn, m_w_o_ssm, m_w_out, m_ffn2_norm, m_ffn2_wi, m_ffn2_wo, m_final_norm, v_ffn1_norm, v_ffn1_wi, v_ffn1_wo, v_mix_norm, v_w_in, v_mla_q_norm, v_mla_w_uq, v_mla_kv_norm, v_mla_w_ukv, v_hgrn_lb_logits, v_hgrn_norm, v_ssm_conv_w, v_ssm_conv_b, v_ssm_a_log, v_ssm_dt_bias, v_ssm_d, v_ssm_norm, v_w_o_mla, v_w_o_hgrn, v_w_o_ssm, v_w_out, v_ffn2_norm, v_ffn2_wi, v_ffn2_wo, v_final_norm):
    return _step(dict(locals()))
```

```python
import functools

import jax
import jax.numpy as jnp
from jax import lax
from jax.experimental import pallas as pl
from jax.experimental.pallas import tpu as pltpu

F32 = jnp.float32
BF16 = jnp.bfloat16
HI = lax.Precision.HIGHEST
MESH = pl.DeviceIdType.MESH

EPS = 1e-6
CHUNK = 64
N_DEV = 8

MLA_HEADS = 16
MLA_Q_RANK = 512
MLA_KV_RANK = 512
MLA_NOPE = 128
MLA_ROPE = 64
MLA_V = 128
ROPE_THETA = 10000.0
HG_HEADS = 16
HG_DK = 128
HG_WIDTH = HG_HEADS * HG_DK
SSM_HEADDIM = 64
SSM_GROUPS = 8
SSM_STATE = 128
SSM_CONV = 4

ADAM_LR = 0.001
ADAM_B1 = 0.9
ADAM_B2 = 0.999
ADAM_EPS = 1e-08
ADAM_WD = 0.01
ADAM_STEP = 10

LANES = 128
VMEM_LIMIT = 48 * 1024 * 1024
ROW_BLOCK_ELEMS = 256 * 1024
MM_TILE_M, MM_TILE_N, MM_TILE_K = 1408, 1536, 1536
MM_FULL_K = 2048
COMM_COLS = 1024
COMM_BYTES = 128 * 1024 * 1024


def _cparams(*sem):
    return pltpu.CompilerParams(dimension_semantics=sem, vmem_limit_bytes=VMEM_LIMIT)


def _tile(dim, pref):
    best = dim
    for t in range(LANES, min(dim, pref) + 1, LANES):
        if dim % t == 0:
            best = t
    return best


def _mm_call(a, b, mode, out_dtype, name):
    if mode == "nn":
        (m, k), (k2, n) = a.shape, b.shape
    elif mode == "nt":
        (m, k), (n, k2) = a.shape, b.shape
    else:
        (k, m), (k2, n) = a.shape, b.shape
    assert k == k2, (a.shape, b.shape, mode)
    tm, tn = _tile(m, MM_TILE_M), _tile(n, MM_TILE_N)
    tk = k if k <= MM_FULL_K else _tile(k, MM_TILE_K)
    nk = k // tk
    if mode == "nn":
        a_spec = pl.BlockSpec((tm, tk), lambda i, j, kk: (i, kk))
        b_spec = pl.BlockSpec((tk, tn), lambda i, j, kk: (kk, j))
        dims = (((1,), (0,)), ((), ()))
    elif mode == "nt":
        a_spec = pl.BlockSpec((tm, tk), lambda i, j, kk: (i, kk))
        b_spec = pl.BlockSpec((tn, tk), lambda i, j, kk: (j, kk))
        dims = (((1,), (1,)), ((), ()))
    else:
        a_spec = pl.BlockSpec((tk, tm), lambda i, j, kk: (kk, i))
        b_spec = pl.BlockSpec((tk, tn), lambda i, j, kk: (kk, j))
        dims = (((0,), (0,)), ((), ()))

    in_place = out_dtype == F32

    def body(a_ref, b_ref, o_ref, *scratch):
        kk = pl.program_id(2)
        part = lax.dot_general(a_ref[...].astype(BF16), b_ref[...].astype(BF16), dims, preferred_element_type=F32)
        if nk == 1:
            o_ref[...] = part.astype(o_ref.dtype)
            return
        acc_ref = o_ref if in_place else scratch[0]

        @pl.when(kk == 0)
        def _():
            acc_ref[...] = part

        @pl.when(kk != 0)
        def _():
            acc_ref[...] += part

        if not in_place:
            @pl.when(kk == nk - 1)
            def _():
                o_ref[...] = acc_ref[...].astype(o_ref.dtype)

    return pl.pallas_call(
        body, grid=(m // tm, n // tn, nk), in_specs=[a_spec, b_spec],
        out_specs=pl.BlockSpec((tm, tn), lambda i, j, kk: (i, j)),
        out_shape=jax.ShapeDtypeStruct((m, n), out_dtype),
        scratch_shapes=[] if (nk == 1 or in_place) else [pltpu.VMEM((tm, tn), F32)],
        compiler_params=_cparams("parallel", "parallel", "arbitrary"), name=name)(a, b)


@jax.custom_vjp
def mm(a, w):
    return _mm_call(a.astype(BF16), w, "nn", F32, "mm_fwd")


def _mm_fwd(a, w):
    a16 = a.astype(BF16)
    return _mm_call(a16, w, "nn", F32, "mm_fwd"), (a16, w)


def _mm_bwd(res, g):
    a16, w = res
    g16 = g.astype(BF16)
    return _mm_call(g16, w, "nt", F32, "mm_da"), _mm_call(a16, g16, "tn", w.dtype, "mm_dw")


mm.defvjp(_mm_fwd, _mm_bwd)


def _row_block(t, widths):
    bt = 8
    while 2 * bt * max(widths) <= ROW_BLOCK_ELEMS:
        bt *= 2
    while t % bt:
        bt //= 2
    return bt


def make_rowwise(fn, name, n_par, group_width=None, shared=(), nondiff=()):
    def specs(args):
        t = max(a.shape[0] for a in args)
        cut = next(a for i, a in enumerate(args) if i >= n_par and i not in shared)
        gw = group_width(cut.shape[1]) if callable(group_width) else group_width
        groups = cut.shape[1] // gw if gw else 1
        ws = [a.shape[1] if i in shared else a.shape[1] // groups for i, a in enumerate(args)]
        ows = out_widths(ws)
        bt = _row_block(t, ws + ows)
        sp = []
        for i, a in enumerate(args):
            col = (lambda g: 0) if i in shared else (lambda g: g)
            if i < n_par:
                sp.append(pl.BlockSpec((1, ws[i]), lambda g, r, col=col: (0, col(g))))
            else:
                sp.append(pl.BlockSpec((bt, ws[i]), lambda g, r, col=col: (r, col(g))))
        return t, bt, groups, ows, sp

    def out_widths(ws):
        blocks = [jax.ShapeDtypeStruct((1 if i < n_par else 8, w), F32) for i, w in enumerate(ws)]
        return [o.shape[1] for o in jax.eval_shape(fn, *blocks)]

    def fwd_call(*args):
        t, bt, groups, ows, in_specs = specs(args)
        n_in = len(args)

        def body(*refs):
            outs = fn(*[r[...] for r in refs[:n_in]])
            for r, o in zip(refs[n_in:], outs):
                r[...] = o

        return pl.pallas_call(
            body, grid=(groups, t // bt), in_specs=in_specs,
            out_specs=[pl.BlockSpec((bt, w), lambda g, r: (r, g)) for w in ows],
            out_shape=[jax.ShapeDtypeStruct((t, w * groups), F32) for w in ows],
            compiler_params=_cparams("parallel", "parallel"), name=name + "_fwd")(*args)

    def bwd_call(args, gs):
        t, bt, groups, ows, in_specs = specs(args)
        n_in, n_out = len(args), len(gs)
        diff = [i for i in range(n_in) if i not in nondiff]
        g_specs = [pl.BlockSpec((bt, w), lambda g, r: (r, g)) for w in ows]
        o_specs, o_shapes = [], []
        for i in diff:
            o_specs.append(in_specs[i])
            o_shapes.append(jax.ShapeDtypeStruct(args[i].shape, F32))

        def body(*refs):
            r_idx = pl.program_id(1)
            vals = [r[...] for r in refs[:n_in]]
            cts = tuple(r[...] for r in refs[n_in:n_in + n_out])

            def f_diff(*dv):
                full = list(vals)
                for i, v in zip(diff, dv):
                    full[i] = v
                return tuple(fn(*full))

            _, vjp = jax.vjp(f_diff, *[vals[i] for i in diff])
            grads = vjp(cts)
            for i, g_val, ref in zip(diff, grads, refs[n_in + n_out:]):
                if i < n_par:
                    @pl.when(r_idx == 0)
                    def _(ref=ref, g_val=g_val):
                        ref[...] = g_val

                    @pl.when(r_idx != 0)
                    def _(ref=ref, g_val=g_val):
                        ref[...] += g_val
                else:
                    ref[...] = g_val

        outs = pl.pallas_call(
            body, grid=(groups, t // bt), in_specs=in_specs + g_specs, out_specs=o_specs, out_shape=o_shapes,
            compiler_params=_cparams("parallel", "arbitrary"), name=name + "_bwd")(*args, *gs)
        full = [jnp.zeros_like(a) for a in args]
        for i, o in zip(diff, outs):
            full[i] = o
        return tuple(full)

    @jax.custom_vjp
    def op(*args):
        return tuple(fwd_call(*args))

    def op_fwd(*args):
        return tuple(fwd_call(*args)), args

    def op_bwd(args, gs):
        return bwd_call(args, gs)

    op.defvjp(op_fwd, op_bwd)
    return op


def _silu(x):
    return x * jax.nn.sigmoid(x)


def _rmsnorm_fn(w, x):
    return (x * lax.rsqrt(jnp.mean(x * x, axis=-1, keepdims=True) + EPS) * w,)


def _swiglu_fn(g, u):
    return (_silu(g) * u,)


def _silu_fn(x):
    return (_silu(x),)


def _rope_fn(x, cos, sin):
    i = lax.broadcasted_iota(jnp.int32, (LANES, LANES), 0)
    j = lax.broadcasted_iota(jnp.int32, (LANES, LANES), 1)
    half = MLA_ROPE // 2
    first = (j % MLA_ROPE) < half
    p = jnp.where(first & (i == j + half), -1.0, 0.0) + jnp.where((~first) & (i == j - half), 1.0, 0.0)
    return (x * cos + jnp.dot(x, p.astype(F32), precision=HI) * sin,)


def _hgrn_out_fn(w, o, g):
    return (o * lax.rsqrt(jnp.mean(o * o, axis=-1, keepdims=True) + EPS) * w * _silu(g),)


def _softplus(x):
    return jnp.maximum(x, 0.0) + jnp.log(1.0 + jnp.exp(-jnp.abs(x)))


def _dt_expand_fn(bias, dt_raw):
    nh = dt_raw.shape[1]
    h = lax.broadcasted_iota(jnp.int32, (nh, nh * SSM_HEADDIM), 0)
    c = lax.broadcasted_iota(jnp.int32, (nh, nh * SSM_HEADDIM), 1)
    e = (c // SSM_HEADDIM == h).astype(F32)
    return (jnp.dot(_softplus(dt_raw + bias), e, precision=HI),)


def _ssm_norm_fn(w, y, z):
    y = y * _silu(z)
    return (y * lax.rsqrt(jnp.mean(y * y, axis=-1, keepdims=True) + EPS) * w,)


def _merge_fn(ya, yb, yc, ga, gb, gc):
    return (jax.nn.sigmoid(ga) * ya + jax.nn.sigmoid(gb) * yb + jax.nn.sigmoid(gc) * yc,)


def _loss_fn(w, x, tgt):
    y = x * lax.rsqrt(jnp.mean(x * x, axis=-1, keepdims=True) + EPS) * w
    err = y - tgt
    return (0.5 * jnp.mean(err * err, axis=-1, keepdims=True),)


rmsnorm = make_rowwise(_rmsnorm_fn, "rmsnorm", 1)
swiglu = make_rowwise(_swiglu_fn, "swiglu", 0, group_width=512)
silu_op = make_rowwise(_silu_fn, "silu", 0, group_width=512)
rope = make_rowwise(_rope_fn, "rope", 0, group_width=LANES, shared=(1, 2), nondiff=(1, 2))
hgrn_out = make_rowwise(_hgrn_out_fn, "hgrn_out", 1, group_width=HG_DK)
dt_expand = make_rowwise(_dt_expand_fn, "dt_expand", 1)
ssm_norm = make_rowwise(_ssm_norm_fn, "ssm_norm", 1, group_width=lambda w: w // SSM_GROUPS)
merge = make_rowwise(_merge_fn, "merge", 0, group_width=512)
loss_rows = make_rowwise(_loss_fn, "loss", 1, nondiff=(2,))


HALO = 8


def _conv_blocks(t, c):
    cw = _tile(c, 512)
    bt = _row_block(t, [cw])
    return cw, bt


def _conv_fwd_call(w, b, x):
    t, c = x.shape
    cw, bt = _conv_blocks(t, c)
    hb = bt // HALO

    def body(w_ref, b_ref, x_ref, prev_ref, y_ref):
        r = pl.program_id(1)
        prev = jnp.where(r == 0, 0.0, prev_ref[...])
        xx = jnp.concatenate([prev, x_ref[...]], axis=0)
        acc = jnp.zeros((bt, cw), F32) + b_ref[...]
        for k in range(SSM_CONV):
            sh = SSM_CONV - 1 - k
            xs = xx if sh == 0 else pltpu.roll(xx, sh, axis=0)
            acc = acc + w_ref[k:k + 1, :] * xs[HALO:, :]
        y_ref[...] = acc

    return pl.pallas_call(
        body, grid=(c // cw, t // bt),
        in_specs=[pl.BlockSpec((SSM_CONV, cw), lambda g, r: (0, g)), pl.BlockSpec((1, cw), lambda g, r: (0, g)),
                  pl.BlockSpec((bt, cw), lambda g, r: (r, g)),
                  pl.BlockSpec((HALO, cw), lambda g, r: (jnp.maximum(r * hb - 1, 0), g))],
        out_specs=pl.BlockSpec((bt, cw), lambda g, r: (r, g)),
        out_shape=jax.ShapeDtypeStruct((t, c), F32),
        compiler_params=_cparams("parallel", "parallel"), name="conv_fwd")(w, b, x, x)


def _conv_bwd_call(w, x, dy):
    t, c = x.shape
    cw, bt = _conv_blocks(t, c)
    hb = bt // HALO
    nr = t // bt

    def body(w_ref, x_ref, prev_ref, dy_ref, next_ref, dx_ref, dw_ref, db_ref):
        r = pl.program_id(1)
        prev = jnp.where(r == 0, 0.0, prev_ref[...])
        nxt = jnp.where(r == nr - 1, 0.0, next_ref[...])
        xx = jnp.concatenate([prev, x_ref[...]], axis=0)
        dd = jnp.concatenate([dy_ref[...], nxt], axis=0)
        dy_val = dy_ref[...]
        dx = jnp.zeros((bt, cw), F32)
        dws = []
        for k in range(SSM_CONV):
            sh = SSM_CONV - 1 - k
            xs = xx if sh == 0 else pltpu.roll(xx, sh, axis=0)
            ds = dd if sh == 0 else pltpu.roll(dd, bt + HALO - sh, axis=0)
            dx = dx + w_ref[k:k + 1, :] * ds[:bt, :]
            dws.append(jnp.sum(dy_val * xs[HALO:, :], axis=0, keepdims=True))
        dx_ref[...] = dx
        dw = jnp.concatenate(dws, axis=0)
        db = jnp.sum(dy_val, axis=0, keepdims=True)

        @pl.when(r == 0)
        def _():
            dw_ref[...] = dw
            db_ref[...] = db

        @pl.when(r != 0)
        def _():
            dw_ref[...] += dw
            db_ref[...] += db

    return pl.pallas_call(
        body, grid=(c // cw, nr),
        in_specs=[pl.BlockSpec((SSM_CONV, cw), lambda g, r: (0, g)),
                  pl.BlockSpec((bt, cw), lambda g, r: (r, g)),
                  pl.BlockSpec((HALO, cw), lambda g, r: (jnp.maximum(r * hb - 1, 0), g)),
                  pl.BlockSpec((bt, cw), lambda g, r: (r, g)),
                  pl.BlockSpec((HALO, cw), lambda g, r: (jnp.minimum((r + 1) * hb, nr * hb - 1), g))],
        out_specs=[pl.BlockSpec((bt, cw), lambda g, r: (r, g)),
                   pl.BlockSpec((SSM_CONV, cw), lambda g, r: (0, g)), pl.BlockSpec((1, cw), lambda g, r: (0, g))],
        out_shape=[jax.ShapeDtypeStruct((t, c), F32), jax.ShapeDtypeStruct((SSM_CONV, c), F32),
                   jax.ShapeDtypeStruct((1, c), F32)],
        compiler_params=_cparams("parallel", "arbitrary"), name="conv_bwd")(w, x, x, dy, dy)


@jax.custom_vjp
def conv(w, b, x):
    return _conv_fwd_call(w, b, x)


def _conv_vjp_fwd(w, b, x):
    return _conv_fwd_call(w, b, x), (w, x)


def _conv_vjp_bwd(res, dy):
    w, x = res
    dx, dw, db = _conv_bwd_call(w, x, dy)
    return dw, db, dx


conv.defvjp(_conv_vjp_fwd, _conv_vjp_bwd)


ATT_DQK = 2 * LANES
ATT_SCALE = (MLA_NOPE + MLA_ROPE) ** -0.5
NEG = -1e30


def _att_tiles(t):
    tq = min(512, max(CHUNK, t // 4))
    tk = min(2 * tq, t)
    return tq, tk, tk // tq


def _att_mask(s, i, j, tq, tk):
    row = (i * tq + lax.broadcasted_iota(jnp.int32, (tq, tk), 0)) // CHUNK
    col = (j * tk + lax.broadcasted_iota(jnp.int32, (tq, tk), 1)) // CHUNK
    return jnp.where(col <= row, s, NEG)


def _grid_marks(n0, n1, n2):
    a, b, c = pl.program_id(0), pl.program_id(1), pl.program_id(2)
    inner0 = (b == 0) & (c == 0)
    return (a == 0) & inner0, (a == n0 // 2) & inner0, (a == n0 - 1) & (b == n1 - 1) & (c == n2 - 1)


def _att_fwd_call(q, k, v, shards=()):
    t = q.shape[0]
    h = q.shape[1] // ATT_DQK
    dv = v.shape[1] // h
    tq, tk, ratio = _att_tiles(t)
    nq, nk = t // tq, t // tk
    dims_nt = (((1,), (1,)), ((), ()))
    n_c = len(shards)

    def body(q_ref, k_ref, v_ref, *rest):
        x_refs, rest = rest[:n_c], rest[n_c:]
        o_ref, lse_ref = rest[:2]
        g_refs, rest = rest[2:2 + n_c], rest[2 + n_c:]
        m_scr, l_scr, acc_scr = rest[:3]
        i, j = pl.program_id(1), pl.program_id(2)
        if n_c:
            start, middle, finish = _gather_phases(x_refs, g_refs, *rest[3:])
            first, mid, last = _grid_marks(h, nq, nk)
            pl.when(first)(start)
            pl.when(mid)(middle)

        @pl.when(j == 0)
        def _():
            m_scr[...] = jnp.full_like(m_scr, NEG)
            l_scr[...] = jnp.zeros_like(l_scr)
            acc_scr[...] = jnp.zeros_like(acc_scr)

        def step(masked):
            s = lax.dot_general(q_ref[...], k_ref[...], dims_nt, preferred_element_type=F32) * ATT_SCALE
            if masked:
                s = _att_mask(s, i, j, tq, tk)
            m_new = jnp.maximum(m_scr[...], jnp.max(s, axis=-1, keepdims=True))
            alpha = jnp.exp(m_scr[...] - m_new)
            p = jnp.exp(s - m_new)
            l_scr[...] = alpha * l_scr[...] + jnp.sum(p, axis=-1, keepdims=True)
            acc_scr[...] = alpha * acc_scr[...] + jnp.dot(p.astype(BF16), v_ref[...], preferred_element_type=F32)
            m_scr[...] = m_new

        pl.when(j < i // ratio)(functools.partial(step, False))

        @pl.when(j == i // ratio)
        def _():
            step(True)
            o_ref[...] = acc_scr[...] / l_scr[...]
            lse_ref[0] = m_scr[...] + jnp.log(l_scr[...])

        if n_c:
            pl.when(last)(finish)

    outs = pl.pallas_call(
        body, grid=(h, nq, nk),
        in_specs=[pl.BlockSpec((tq, ATT_DQK), lambda hh, i, j: (i, hh)),
                  pl.BlockSpec((tk, ATT_DQK), lambda hh, i, j: (jnp.minimum(j, i // ratio), hh)),
                  pl.BlockSpec((tk, dv), lambda hh, i, j: (jnp.minimum(j, i // ratio), hh))] + [ANY] * n_c,
        out_specs=[pl.BlockSpec((tq, dv), lambda hh, i, j: (i, hh)),
                   pl.BlockSpec((1, tq, 1), lambda hh, i, j: (hh, i, 0))] + [ANY] * n_c,
        out_shape=[jax.ShapeDtypeStruct((t, h * dv), F32), jax.ShapeDtypeStruct((h, t, 1), F32)]
        + [jax.ShapeDtypeStruct((N_DEV,) + s.shape, s.dtype) for s in shards],
        scratch_shapes=[pltpu.VMEM((tq, 1), F32), pltpu.VMEM((tq, 1), F32), pltpu.VMEM((tq, dv), F32)]
        + (_comm_scratch(n_c) if n_c else []),
        compiler_params=_cparams("arbitrary", "arbitrary", "arbitrary"), name="att_fwd")(q, k, v, *shards)
    return outs[0], outs[1], tuple(outs[2:])


def _att_dq_call(q, k, v, o, lse, do, grads=()):
    t = q.shape[0]
    h = q.shape[1] // ATT_DQK
    dv = v.shape[1] // h
    tq, tk, ratio = _att_tiles(t)
    nq, nk = t // tq, t // tk
    dims_nt = (((1,), (1,)), ((), ()))
    n_c = len(grads)

    def body(q_ref, k_ref, v_ref, o_ref, lse_ref, do_ref, *rest):
        x_refs, rest = rest[:n_c], rest[n_c:]
        dq_ref, delta_ref = rest[:2]
        p_refs, rest = rest[2:2 + n_c], rest[2 + n_c:]
        acc_scr, d_scr = rest[:2]
        i, j = pl.program_id(1), pl.program_id(2)
        if n_c:
            start, finish = _exchange_phases(x_refs, p_refs, *rest[2:])
            first, _, last = _grid_marks(h, nq, nk)
            pl.when(first)(start)

        @pl.when(j == 0)
        def _():
            acc_scr[...] = jnp.zeros_like(acc_scr)
            d_scr[...] = jnp.sum(do_ref[...] * o_ref[...], axis=-1, keepdims=True)

        def step(masked):
            s = lax.dot_general(q_ref[...], k_ref[...], dims_nt, preferred_element_type=F32) * ATT_SCALE
            if masked:
                s = _att_mask(s, i, j, tq, tk)
            p = jnp.exp(s - lse_ref[0])
            dp = lax.dot_general(do_ref[...].astype(BF16), v_ref[...], dims_nt, preferred_element_type=F32)
            ds = p * (dp - d_scr[...]) * ATT_SCALE
            acc_scr[...] += jnp.dot(ds.astype(BF16), k_ref[...], preferred_element_type=F32)

        pl.when(j < i // ratio)(functools.partial(step, False))

        @pl.when(j == i // ratio)
        def _():
            step(True)
            dq_ref[...] = acc_scr[...].astype(dq_ref.dtype)
            delta_ref[0] = d_scr[...]

        if n_c:
            pl.when(last)(finish)

    outs = pl.pallas_call(
        body, grid=(h, nq, nk),
        in_specs=[pl.BlockSpec((tq, ATT_DQK), lambda hh, i, j: (i, hh)),
                  pl.BlockSpec((tk, ATT_DQK), lambda hh, i, j: (jnp.minimum(j, i // ratio), hh)),
                  pl.BlockSpec((tk, dv), lambda hh, i, j: (jnp.minimum(j, i // ratio), hh)),
                  pl.BlockSpec((tq, dv), lambda hh, i, j: (i, hh)),
                  pl.BlockSpec((1, tq, 1), lambda hh, i, j: (hh, i, 0)),
                  pl.BlockSpec((tq, dv), lambda hh, i, j: (i, hh))] + [ANY] * n_c,
        out_specs=[pl.BlockSpec((tq, ATT_DQK), lambda hh, i, j: (i, hh)),
                   pl.BlockSpec((1, tq, 1), lambda hh, i, j: (hh, i, 0))] + [ANY] * n_c,
        out_shape=[jax.ShapeDtypeStruct(q.shape, q.dtype), jax.ShapeDtypeStruct((h, t, 1), F32)]
        + [jax.ShapeDtypeStruct(g.shape, g.dtype) for g in grads],
        scratch_shapes=[pltpu.VMEM((tq, ATT_DQK), F32), pltpu.VMEM((tq, 1), F32)]
        + (_comm_scratch(n_c) if n_c else []),
        compiler_params=_cparams("arbitrary", "arbitrary", "arbitrary"), name="att_dq")(q, k, v, o, lse, do, *grads)
    return outs[0], outs[1], tuple(outs[2:])


def _att_dkv_call(q, k, v, lse, delta, do, grads=()):
    t = q.shape[0]
    h = q.shape[1] // ATT_DQK
    dv = v.shape[1] // h
    tq, tk, ratio = _att_tiles(t)
    nq, nk = t // tq, t // tk
    dims_nt = (((1,), (1,)), ((), ()))
    dims_tn = (((0,), (0,)), ((), ()))
    n_c = len(grads)

    def body(q_ref, k_ref, v_ref, lse_ref, delta_ref, do_ref, *rest):
        x_refs, rest = rest[:n_c], rest[n_c:]
        dk_ref, dv_ref = rest[:2]
        p_refs, rest = rest[2:2 + n_c], rest[2 + n_c:]
        dk_scr, dv_scr = rest[:2]
        j, i = pl.program_id(1), pl.program_id(2)
        if n_c:
            start, finish = _exchange_phases(x_refs, p_refs, *rest[2:])
            first, _, last = _grid_marks(h, nk, nq)
            pl.when(first)(start)

        @pl.when(i == 0)
        def _():
            dk_scr[...] = jnp.zeros_like(dk_scr)
            dv_scr[...] = jnp.zeros_like(dv_scr)

        def step(masked):
            s = lax.dot_general(q_ref[...], k_ref[...], dims_nt, preferred_element_type=F32) * ATT_SCALE
            if masked:
                s = _att_mask(s, i, j, tq, tk)
            p = jnp.exp(s - lse_ref[0])
            do_b = do_ref[...].astype(BF16)
            dv_scr[...] += lax.dot_general(p.astype(BF16), do_b, dims_tn, preferred_element_type=F32)
            dp = lax.dot_general(do_b, v_ref[...], dims_nt, preferred_element_type=F32)
            ds = p * (dp - delta_ref[0]) * ATT_SCALE
            dk_scr[...] += lax.dot_general(ds.astype(BF16), q_ref[...], dims_tn, preferred_element_type=F32)

        pl.when(i // ratio > j)(functools.partial(step, False))
        pl.when(i // ratio == j)(functools.partial(step, True))

        @pl.when(i == nq - 1)
        def _():
            dk_ref[...] = dk_scr[...].astype(dk_ref.dtype)
            dv_ref[...] = dv_scr[...].astype(dv_ref.dtype)

        if n_c:
            pl.when(last)(finish)

    def qi(i, j):
        return jnp.maximum(i, j * ratio)

    outs = pl.pallas_call(
        body, grid=(h, nk, nq),
        in_specs=[pl.BlockSpec((tq, ATT_DQK), lambda hh, j, i: (qi(i, j), hh)),
                  pl.BlockSpec((tk, ATT_DQK), lambda hh, j, i: (j, hh)),
                  pl.BlockSpec((tk, dv), lambda hh, j, i: (j, hh)),
                  pl.BlockSpec((1, tq, 1), lambda hh, j, i: (hh, qi(i, j), 0)),
                  pl.BlockSpec((1, tq, 1), lambda hh, j, i: (hh, qi(i, j), 0)),
                  pl.BlockSpec((tq, dv), lambda hh, j, i: (qi(i, j), hh))] + [ANY] * n_c,
        out_specs=[pl.BlockSpec((tk, ATT_DQK), lambda hh, j, i: (j, hh)),
                   pl.BlockSpec((tk, dv), lambda hh, j, i: (j, hh))] + [ANY] * n_c,
        out_shape=[jax.ShapeDtypeStruct(k.shape, k.dtype), jax.ShapeDtypeStruct(v.shape, v.dtype)]
        + [jax.ShapeDtypeStruct(g.shape, g.dtype) for g in grads],
        scratch_shapes=[pltpu.VMEM((tk, ATT_DQK), F32), pltpu.VMEM((tk, dv), F32)]
        + (_comm_scratch(n_c) if n_c else []),
        compiler_params=_cparams("arbitrary", "arbitrary", "arbitrary"), name="att_dkv")(
            q, k, v, lse, delta, do, *grads)
    return outs[0], outs[1], tuple(outs[2:])


@jax.custom_vjp
def attention(q, k, v, shards):
    o, _, gathered = _att_fwd_call(q, k, v, tuple(s.astype(BF16) for s in shards))
    return o, gathered


def _att_vjp_fwd(q, k, v, shards):
    o, lse, gathered = _att_fwd_call(q, k, v, tuple(s.astype(BF16) for s in shards))
    return (o, gathered), (q, k, v, o, lse)


def _att_vjp_bwd(res, cts):
    q, k, v, o, lse = res
    do, g_gathered = cts
    sizes = [g.size for g in g_gathered]
    cut = 0
    while cut < len(sizes) and 2 * sum(sizes[:cut + 1]) <= sum(sizes):
        cut += 1
    dq, delta, parts_a = _att_dq_call(q, k, v, o, lse, do, tuple(g_gathered[:cut]))
    dk, dv, parts_b = _att_dkv_call(q, k, v, lse, delta, do, tuple(g_gathered[cut:]))
    return dq, dk, dv, tuple(sum_parts(p) for p in parts_a + parts_b)


attention.defvjp(_att_vjp_fwd, _att_vjp_bwd)


MID = CHUNK // 2 - 1


def _tril(n):
    r = lax.broadcasted_iota(jnp.int32, (n, n), 0)
    c = lax.broadcasted_iota(jnp.int32, (n, n), 1)
    return c <= r


def _bdot(a, b, dims):
    return lax.dot_general(a.astype(BF16), b.astype(BF16), dims, preferred_element_type=F32)


NN = (((1,), (0,)), ((), ()))
NT = (((1,), (1,)), ((), ()))
TN = (((0,), (0,)), ((), ()))


def _hgrn_chunk(state, q_in, f_in, i_in, lb):
    tril = _tril(CHUNK)
    f = lb + (1.0 - lb) * jax.nn.sigmoid(f_in)
    logf = jnp.log(f)
    b = jnp.dot(tril.astype(F32), logf, precision=HI)
    q = _silu(q_in) * HG_DK ** -0.5
    k = 1.0 - f
    b_mid = b[MID:MID + 1, :]
    att = _bdot(q * jnp.exp(b - b_mid), k * jnp.exp(b_mid - b), NT)
    att = jnp.where(tril, att, 0.0)
    o = _bdot(q * jnp.exp(b), state, NT) + _bdot(att, i_in, NN)
    b_last = b[CHUNK - 1:CHUNK, :]
    new_state = jnp.exp(b_last) * state + _bdot(i_in, k * jnp.exp(b_last - b), TN)
    return o, new_state


HG_STEP_CHUNKS = 8


def _hgrn_step_rows(t):
    n = HG_STEP_CHUNKS
    while (t // CHUNK) % n:
        n //= 2
    return n * CHUNK


def _hgrn_chunks(state, q_in, f_in, i_in, lb):
    outs = []
    for c in range(q_in.shape[0] // CHUNK):
        rows = slice(c * CHUNK, (c + 1) * CHUNK)
        o, state = _hgrn_chunk(state, q_in[rows], f_in[rows], i_in[rows], lb)
        outs.append(o)
    return (outs[0] if len(outs) == 1 else jnp.concatenate(outs, axis=0)), state


def _hgrn_fwd_call(q, f, i, lb):
    t = q.shape[0]
    rows = _hgrn_step_rows(t)
    nc = t // rows
    blk = pl.BlockSpec((rows, HG_DK), lambda h, c: (c, h))

    def body(q_ref, f_ref, i_ref, lb_ref, o_ref, s_ref, s_scr):
        @pl.when(pl.program_id(1) == 0)
        def _():
            s_scr[...] = jnp.zeros_like(s_scr)

        s_ref[0, 0] = s_scr[...]
        o, ns = _hgrn_chunks(s_scr[...], q_ref[...], f_ref[...], i_ref[...], lb_ref[...])
        o_ref[...] = o
        s_scr[...] = ns

    return pl.pallas_call(
        body, grid=(HG_HEADS, nc),
        in_specs=[blk, blk, blk, pl.BlockSpec((1, HG_DK), lambda h, c: (0, h))],
        out_specs=[blk, pl.BlockSpec((1, 1, HG_DK, HG_DK), lambda h, c: (h, c, 0, 0))],
        out_shape=[jax.ShapeDtypeStruct((t, HG_WIDTH), F32), jax.ShapeDtypeStruct((HG_HEADS, nc, HG_DK, HG_DK), F32)],
        scratch_shapes=[pltpu.VMEM((HG_DK, HG_DK), F32)],
        compiler_params=_cparams("parallel", "arbitrary"), name="hgrn_fwd")(q, f, i, lb)


def _hgrn_bwd_call(q, f, i, lb, states, do):
    t = q.shape[0]
    rows = _hgrn_step_rows(t)
    nc = t // rows
    blk = pl.BlockSpec((rows, HG_DK), lambda h, c: (nc - 1 - c, h))
    row = pl.BlockSpec((1, HG_DK), lambda h, c: (0, h))

    def body(q_ref, f_ref, i_ref, lb_ref, s_ref, do_ref, dq_ref, df_ref, di_ref, dlb_ref, ds_scr):
        c = pl.program_id(1)

        @pl.when(c == 0)
        def _():
            ds_scr[...] = jnp.zeros_like(ds_scr)

        _, vjp = jax.vjp(_hgrn_chunks, s_ref[0, 0], q_ref[...], f_ref[...], i_ref[...], lb_ref[...])
        ds, dq, df, di, dlb = vjp((do_ref[...], ds_scr[...]))
        ds_scr[...] = ds
        dq_ref[...] = dq
        df_ref[...] = df
        di_ref[...] = di

        @pl.when(c == 0)
        def _():
            dlb_ref[...] = dlb

        @pl.when(c != 0)
        def _():
            dlb_ref[...] += dlb

    return pl.pallas_call(
        body, grid=(HG_HEADS, nc),
        in_specs=[blk, blk, blk, row, pl.BlockSpec((1, 1, HG_DK, HG_DK), lambda h, c: (h, nc - 1 - c, 0, 0)), blk],
        out_specs=[blk, blk, blk, row],
        out_shape=[jax.ShapeDtypeStruct((t, HG_WIDTH), F32)] * 3 + [jax.ShapeDtypeStruct((1, HG_WIDTH), F32)],
        scratch_shapes=[pltpu.VMEM((HG_DK, HG_DK), F32)],
        compiler_params=_cparams("parallel", "arbitrary"), name="hgrn_bwd")(q, f, i, lb, states, do)


@jax.custom_vjp
def hgrn_scan(q, f, i, lb):
    return _hgrn_fwd_call(q, f, i, lb)[0]


def _hgrn_vjp_fwd(q, f, i, lb):
    o, states = _hgrn_fwd_call(q, f, i, lb)
    return o, (q, f, i, lb, states)


def _hgrn_vjp_bwd(res, do):
    return tuple(_hgrn_bwd_call(*res, do))


hgrn_scan.defvjp(_hgrn_vjp_fwd, _hgrn_vjp_bwd)


def _ssd_chunk(state, xs, bm, cm, dtx, alog, dskip):
    assert CHUNK == SSM_HEADDIM and 2 * SSM_HEADDIM == LANES
    gw = xs.shape[1]
    trilf = _tril(CHUNK).astype(F32)
    da = dtx * (-jnp.exp(alog))
    a = jnp.dot(trilf, da, precision=HI)
    xdt = xs * dtx
    cb2 = _bdot(cm, jnp.concatenate([bm, bm], axis=0), NT)
    row = lax.broadcasted_iota(jnp.int32, (CHUNK, LANES), 0)
    src = lax.broadcasted_iota(jnp.int32, (CHUNK, LANES), 1) % CHUNK
    first_head = lax.broadcasted_iota(jnp.int32, (CHUNK, LANES), 1) < CHUNK
    ys = []
    for p in range(gw // LANES):
        lanes = slice(p * LANES, (p + 1) * LANES)
        a_src = jnp.sum(jnp.where(row <= src, da[:, lanes], 0.0), axis=0, keepdims=True)
        decay_ls = jnp.exp(jnp.where(src <= row, a[:, lanes] - a_src, NEG))
        x_pair = xdt[:, lanes]
        rhs = jnp.concatenate([jnp.where(first_head, x_pair, 0.0), jnp.where(first_head, 0.0, x_pair)], axis=0)
        ys.append(_bdot(cb2 * decay_ls, rhs, NN))
    y_diag = ys[0] if len(ys) == 1 else jnp.concatenate(ys, axis=1)
    y_off = jnp.exp(a) * _bdot(cm, state, NN)
    y = y_diag + y_off + xs * dskip
    a_last = a[CHUNK - 1:CHUNK, :]
    new_state = jnp.exp(a_last) * state + _bdot(bm, jnp.exp(a_last - a) * xdt, TN)
    return y, new_state


SSD_STEP_CHUNKS = 4


def _ssd_step_rows(t):
    n = SSD_STEP_CHUNKS
    while (t // CHUNK) % n:
        n //= 2
    return n * CHUNK


def _ssd_chunks(state, xs, bm, cm, dtx, alog, dskip):
    outs = []
    for c in range(xs.shape[0] // CHUNK):
        rows = slice(c * CHUNK, (c + 1) * CHUNK)
        y, state = _ssd_chunk(state, xs[rows], bm[rows], cm[rows], dtx[rows], alog, dskip)
        outs.append(y)
    return (outs[0] if len(outs) == 1 else jnp.concatenate(outs, axis=0)), state


def _ssd_specs(inner, rows, nc, rev):
    gw = inner // SSM_GROUPS
    nb = inner // SSM_STATE
    ci = (lambda c: nc - 1 - c) if rev else (lambda c: c)
    xs = pl.BlockSpec((rows, gw), lambda g, c: (ci(c), g))
    bm = pl.BlockSpec((rows, SSM_STATE), lambda g, c: (ci(c), nb + g))
    cm = pl.BlockSpec((rows, SSM_STATE), lambda g, c: (ci(c), nb + SSM_GROUPS + g))
    row = pl.BlockSpec((1, gw), lambda g, c: (0, g))
    st = pl.BlockSpec((1, 1, SSM_STATE, gw), lambda g, c: (g, ci(c), 0, 0))
    return gw, xs, bm, cm, row, st


def _ssd_fwd_call(xbc, dtx, alog, dskip):
    t, inner = dtx.shape
    rows = _ssd_step_rows(t)
    nc = t // rows
    gw, xs_s, bm_s, cm_s, row, st = _ssd_specs(inner, rows, nc, False)

    def body(xs_ref, bm_ref, cm_ref, dt_ref, a_ref, d_ref, y_ref, s_ref, s_scr):
        @pl.when(pl.program_id(1) == 0)
        def _():
            s_scr[...] = jnp.zeros_like(s_scr)

        s_ref[0, 0] = s_scr[...]
        y, ns = _ssd_chunks(s_scr[...], xs_ref[...], bm_ref[...], cm_ref[...], dt_ref[...], a_ref[...], d_ref[...])
        y_ref[...] = y
        s_scr[...] = ns

    return pl.pallas_call(
        body, grid=(SSM_GROUPS, nc), in_specs=[xs_s, bm_s, cm_s, xs_s, row, row],
        out_specs=[xs_s, st],
        out_shape=[jax.ShapeDtypeStruct((t, inner), F32), jax.ShapeDtypeStruct((SSM_GROUPS, nc, SSM_STATE, gw), F32)],
        scratch_shapes=[pltpu.VMEM((SSM_STATE, gw), F32)],
        compiler_params=_cparams("parallel", "arbitrary"), name="ssd_fwd")(xbc, xbc, xbc, dtx, alog, dskip)


def _ssd_bwd_call(xbc, dtx, alog, dskip, states, dy):
    t, inner = dtx.shape
    rows = _ssd_step_rows(t)
    nc = t // rows
    gw, xs_s, bm_s, cm_s, row, st = _ssd_specs(inner, rows, nc, True)
    gn = pl.BlockSpec((rows, SSM_STATE), lambda g, c: (nc - 1 - c, g))
    gn_shape = jax.ShapeDtypeStruct((t, SSM_GROUPS * SSM_STATE), F32)

    def body(xs_ref, bm_ref, cm_ref, dt_ref, a_ref, d_ref, s_ref, dy_ref,
             dxs_ref, dbm_ref, dcm_ref, ddt_ref, da_ref, dd_ref, ds_scr):
        c = pl.program_id(1)

        @pl.when(c == 0)
        def _():
            ds_scr[...] = jnp.zeros_like(ds_scr)

        _, vjp = jax.vjp(_ssd_chunks, s_ref[0, 0], xs_ref[...], bm_ref[...], cm_ref[...], dt_ref[...],
                         a_ref[...], d_ref[...])
        ds, dxs, dbm, dcm, ddt, da, dd = vjp((dy_ref[...], ds_scr[...]))
        ds_scr[...] = ds
        dxs_ref[...] = dxs
        dbm_ref[...] = dbm
        dcm_ref[...] = dcm
        ddt_ref[...] = ddt

        @pl.when(c == 0)
        def _():
            da_ref[...] = da
            dd_ref[...] = dd

        @pl.when(c != 0)
        def _():
            da_ref[...] += da
            dd_ref[...] += dd

    big = jax.ShapeDtypeStruct((t, inner), F32)
    small = jax.ShapeDtypeStruct((1, inner), F32)
    return pl.pallas_call(
        body, grid=(SSM_GROUPS, nc), in_specs=[xs_s, bm_s, cm_s, xs_s, row, row, st, xs_s],
        out_specs=[xs_s, gn, gn, xs_s, row, row],
        out_shape=[big, gn_shape, gn_shape, big, small, small],
        scratch_shapes=[pltpu.VMEM((SSM_STATE, gw), F32)],
        compiler_params=_cparams("parallel", "arbitrary"), name="ssd_bwd")(xbc, xbc, xbc, dtx, alog, dskip, states, dy)


@jax.custom_vjp
def ssd_scan(xbc, dtx, alog, dskip):
    return _ssd_fwd_call(xbc, dtx, alog, dskip)[0]


def _ssd_vjp_fwd(xbc, dtx, alog, dskip):
    y, states = _ssd_fwd_call(xbc, dtx, alog, dskip)
    return y, (xbc, dtx, alog, dskip, states)


def _ssd_vjp_bwd(res, dy):
    dxs, dbm, dcm, ddt, da, dd = _ssd_bwd_call(*res, dy)
    return jnp.concatenate([dxs, dbm, dcm], axis=1), ddt, da, dd


ssd_scan.defvjp(_ssd_vjp_fwd, _ssd_vjp_bwd)


ANY = pl.BlockSpec(memory_space=pl.ANY)


def _my_pos():
    return lax.axis_index("x"), lax.axis_index("y"), lax.axis_index("c")


def _comm_scratch(n):
    return [pltpu.SemaphoreType.DMA((n, N_DEV - 1)), pltpu.SemaphoreType.DMA((n, N_DEV - 1)),
            pltpu.SemaphoreType.DMA((n,))]


def _gather_phases(x_refs, out_refs, send_sems, recv_sems, local_sems):
    x, y, c = _my_pos()
    me, sibling = (x, y, c), (x, y, 1 - c)
    chips = [(1 - x, y), (x, 1 - y), (1 - x, 1 - y)]

    def slot(t, px, py, pc):
        return out_refs[t].at[4 * px + 2 * py + pc]

    def copy(t, k, block, to, own=False):
        return pltpu.make_async_remote_copy(
            src_ref=x_refs[t] if own else slot(t, *block), dst_ref=slot(t, *block),
            send_sem=send_sems.at[t, k], recv_sem=recv_sems.at[t, k], device_id=to, device_id_type=MESH)

    def mine(t):
        return pltpu.make_async_copy(x_refs[t], slot(t, *me), local_sems.at[t])

    def first(t):
        return [copy(t, 0, me, sibling, own=True)] + [copy(t, 1 + j, me, (*chip, c), own=True)
                                                      for j, chip in enumerate(chips)]

    def passed(t):
        return [copy(t, 4 + j, (*chip, c), sibling) for j, chip in enumerate(chips)]

    def start():
        for t in range(len(x_refs)):
            mine(t).start()
            for cp in first(t):
                cp.start()

    def middle():
        for t in range(len(x_refs)):
            for j, chip in enumerate(chips):
                copy(t, 1 + j, (*chip, c), me).wait_recv()
                copy(t, 4 + j, (*chip, c), sibling).start()

    def finish():
        for t in range(len(x_refs)):
            copy(t, 0, sibling, me).wait_recv()
            for j, chip in enumerate(chips):
                copy(t, 4 + j, (*chip, 1 - c), me).wait_recv()
            for cp in first(t) + passed(t):
                cp.wait_send()
            mine(t).wait()

    return start, middle, finish


def _exchange_phases(x_refs, out_refs, send_sems, recv_sems, local_sems):
    x, y, c = _my_pos()
    me = 4 * x + 2 * y + c

    def local(t):
        return pltpu.make_async_copy(x_refs[t].at[me], out_refs[t].at[me], local_sems.at[t])

    def copies(t):
        out = []
        for k in range(1, N_DEV):
            px = 1 - x if k & 4 else x
            py = 1 - y if k & 2 else y
            pc = 1 - c if k & 1 else c
            out.append(pltpu.make_async_remote_copy(
                src_ref=x_refs[t].at[4 * px + 2 * py + pc], dst_ref=out_refs[t].at[me],
                send_sem=send_sems.at[t, k - 1], recv_sem=recv_sems.at[t, k - 1],
                device_id=(px, py, pc), device_id_type=MESH))
        return out

    def start():
        for t in range(len(x_refs)):
            local(t).start()
            for cp in copies(t):
                cp.start()

    def finish():
        for t in range(len(x_refs)):
            for cp in copies(t):
                cp.wait_recv()
            for cp in copies(t):
                cp.wait_send()
            local(t).wait()

    return start, finish


def all_gather(x_shard):
    def body(x_ref, out_ref, send_sems, recv_sems, local_sems):
        start, middle, finish = _gather_phases([x_ref], [out_ref], send_sems, recv_sems, local_sems)
        start()
        middle()
        finish()

    return pl.pallas_call(
        body, out_shape=jax.ShapeDtypeStruct((N_DEV,) + x_shard.shape, x_shard.dtype), in_specs=[ANY],
        out_specs=ANY, scratch_shapes=_comm_scratch(1), name="all_gather")(x_shard)


def exchange(x):
    def body(x_ref, out_ref, send_sems, recv_sems, local_sems):
        start, finish = _exchange_phases([x_ref], [out_ref], send_sems, recv_sems, local_sems)
        start()
        finish()

    return pl.pallas_call(
        body, out_shape=jax.ShapeDtypeStruct(x.shape, x.dtype), in_specs=[ANY], out_specs=ANY,
        scratch_shapes=_comm_scratch(1), name="exchange")(x)


def sum_parts(parts):
    p, r, c_ = parts.shape
    br = _row_block(r, [c_]) if r % 16 == 0 else r

    def body(p_ref, o_ref):
        acc = p_ref[0].astype(F32)
        for i in range(1, p):
            acc = acc + p_ref[i].astype(F32)
        o_ref[...] = acc

    return pl.pallas_call(
        body, grid=(r // br,), in_specs=[pl.BlockSpec((p, br, c_), lambda i: (0, i, 0))],
        out_specs=pl.BlockSpec((br, c_), lambda i: (i, 0)), out_shape=jax.ShapeDtypeStruct((r, c_), F32),
        compiler_params=_cparams("parallel"), name="sum_parts")(parts)


@jax.custom_vjp
def gather_op(shard):
    return all_gather(shard.astype(BF16))


def _gather_op_fwd(shard):
    return all_gather(shard.astype(BF16)), None


def _gather_op_bwd(_, g):
    return (sum_parts(_in_chunks(exchange, g, 1)),)


gather_op.defvjp(_gather_op_fwd, _gather_op_bwd)


def _in_chunks(fn, arr, axis):
    rows = arr.shape[axis]
    pieces = 1
    while (arr.size * arr.dtype.itemsize) // pieces > COMM_BYTES and rows % (2 * pieces) == 0:
        pieces *= 2
    step = rows // pieces
    outs = [fn(lax.slice_in_dim(arr, s, s + step, axis=axis)) for s in range(0, rows, step)]
    return outs[0] if len(outs) == 1 else jnp.concatenate(outs, axis=1)


def reduce_adamw(parts, w, m, v):
    p, r, c_ = parts.shape
    br = _row_block(r, [c_]) if r % 16 == 0 else r
    c1 = 1.0 - ADAM_B1 ** ADAM_STEP
    c2 = 1.0 - ADAM_B2 ** ADAM_STEP

    def body(p_ref, w_ref, m_ref, v_ref, g_ref, d_ref, m2_ref, v2_ref):
        g = p_ref[0].astype(F32)
        for i in range(1, p):
            g = g + p_ref[i].astype(F32)
        m2 = ADAM_B1 * m_ref[...] + (1.0 - ADAM_B1) * g
        v2 = ADAM_B2 * v_ref[...] + (1.0 - ADAM_B2) * (g * g)
        g_ref[...] = g
        m2_ref[...] = m2
        v2_ref[...] = v2
        d_ref[...] = -ADAM_LR * ((m2 / c1) / (jnp.sqrt(v2 / c2) + ADAM_EPS) + ADAM_WD * w_ref[...])

    blk = pl.BlockSpec((br, c_), lambda i: (i, 0))
    return pl.pallas_call(
        body, grid=(r // br,), in_specs=[pl.BlockSpec((p, br, c_), lambda i: (0, i, 0)), blk, blk, blk],
        out_specs=[blk] * 4, out_shape=[jax.ShapeDtypeStruct((r, c_), F32)] * 4,
        compiler_params=_cparams("parallel"), name="reduce_adamw")(parts, w, m, v)


WEIGHTS = ['ffn1_norm', 'ffn1_wi', 'ffn1_wo', 'mix_norm', 'w_in', 'mla_q_norm', 'mla_w_uq', 'mla_kv_norm',
           'mla_w_ukv', 'hgrn_lb_logits', 'hgrn_norm', 'ssm_conv_w', 'ssm_conv_b', 'ssm_a_log', 'ssm_dt_bias',
           'ssm_d', 'ssm_norm', 'w_o_mla', 'w_o_hgrn', 'w_o_ssm', 'w_out', 'ffn2_norm', 'ffn2_wi', 'ffn2_wo',
           'final_norm']
COL_SHARDED = ('ffn1_wi', 'w_in', 'mla_w_uq', 'mla_w_ukv', 'ffn2_wi')
ROW_SHARDED = ('ffn1_wo', 'w_o_mla', 'w_o_hgrn', 'w_o_ssm', 'w_out', 'ffn2_wo')
BIG = tuple(n for n in WEIGHTS if n in COL_SHARDED + ROW_SHARDED)
PRE = ('ffn1_wi', 'ffn1_wo', 'w_in', 'mla_w_uq', 'mla_w_ukv')
POST = ('w_o_mla', 'w_o_hgrn', 'w_o_ssm', 'w_out', 'ffn2_wi', 'ffn2_wo')
CONV_W = 'ssm_conv_w'
REPLICATED = tuple(n for n in WEIGHTS if n not in BIG and n != CONV_W)


def _segment_plan(n, sizes):
    plan, off = [], 0
    for s in sizes:
        a, b = off, off + s
        plan.append([(j, max(a, j * n) - j * n, min(b, (j + 1) * n) - j * n)
                     for j in range(a // n, (b - 1) // n + 1)])
        off = b
    return plan


@functools.partial(jax.custom_vjp, nondiff_argnums=(1,))
def col_segments(blocks, sizes):
    outs = []
    for pieces in _segment_plan(blocks.shape[-1], sizes):
        cut = [blocks[j][:, lo:hi] for j, lo, hi in pieces]
        outs.append(cut[0] if len(cut) == 1 else jnp.concatenate(cut, axis=1))
    return tuple(outs)


def _col_segments_fwd(blocks, sizes):
    return col_segments(blocks, sizes), blocks.shape[-1]


def _col_segments_bwd(sizes, n, gs):
    per_block = [[] for _ in range(N_DEV)]
    for g, pieces in zip(gs, _segment_plan(n, sizes)):
        off = 0
        for j, lo, hi in pieces:
            per_block[j].append(g[:, off:off + hi - lo])
            off += hi - lo
    return (jnp.stack([p[0] if len(p) == 1 else jnp.concatenate(p, axis=1) for p in per_block]),)


col_segments.defvjp(_col_segments_fwd, _col_segments_bwd)


def _rope_tables(t):
    half = MLA_ROPE // 2
    inv = 1.0 / (ROPE_THETA ** (jnp.arange(0, MLA_ROPE, 2, dtype=F32) / MLA_ROPE))
    ang = jnp.arange(t, dtype=F32)[:, None] * inv[None, :]
    reps = LANES // half
    return jnp.tile(jnp.cos(ang), (1, reps)), jnp.tile(jnp.sin(ang), (1, reps))


def _ffn(x, norm, wi, wo):
    dff = wo.shape[0]
    h = rmsnorm(norm[None], x)[0]
    wg, wu = col_segments(wi, (dff, dff))
    return x + 0.5 * mm(swiglu(mm(h, wg), mm(h, wu))[0], wo)


def _per_head(w, widths, pad_to):
    k = w.shape[0]
    w3 = w.reshape(k, -1, sum(widths))
    outs, off = [], 0
    for wd in widths:
        part = w3[:, :, off:off + wd]
        if wd < pad_to:
            part = jnp.pad(part, ((0, 0), (0, 0), (0, pad_to - wd)))
        outs.append(part.reshape(k, -1))
        off += wd
    return outs


def _layer(x, p, lb, cos, sin, carried):
    t, d = x.shape
    inner = 2 * d
    conv_dim = inner + 2 * SSM_GROUPS * SSM_STATE
    n_ssm_heads = inner // SSM_HEADDIM
    x = _ffn(x, p['ffn1_norm'], p['ffn1_wi'], p['ffn1_wo'])

    h = rmsnorm(p['mix_norm'][None], x)[0]
    sizes = (MLA_Q_RANK, MLA_KV_RANK, MLA_ROPE, HG_WIDTH, HG_WIDTH, HG_WIDTH, HG_WIDTH,
             inner, conv_dim, n_ssm_heads, d, d, d)
    (w_q, w_kv, w_kpe, w_hq, w_hf, w_hi, w_hg, w_z, w_xbc, w_dt, w_ga, w_gb, w_gc) = col_segments(p['w_in'], sizes)

    qn = rmsnorm(p['mla_q_norm'][None], mm(h, w_q))[0]
    kvn = rmsnorm(p['mla_kv_norm'][None], mm(h, w_kv))[0]
    w_uq, = col_segments(p['mla_w_uq'], (N_DEV * p['mla_w_uq'].shape[-1],))
    w_ukv, = col_segments(p['mla_w_ukv'], (N_DEV * p['mla_w_ukv'].shape[-1],))
    wq_nope, wq_pe = _per_head(w_uq, (MLA_NOPE, MLA_ROPE), LANES)
    wk_nope, wv = _per_head(w_ukv, (MLA_NOPE, MLA_V), LANES)
    q_nope = mm(qn, wq_nope)
    q_pe = rope(mm(qn, wq_pe), cos, sin)[0]
    k_nope = mm(kvn, wk_nope)
    v = mm(kvn, wv)
    k_rot = rope(mm(h, jnp.pad(w_kpe, ((0, 0), (0, LANES - MLA_ROPE)))), cos, sin)[0]
    q = jnp.concatenate([q_nope.reshape(t, MLA_HEADS, LANES), q_pe.reshape(t, MLA_HEADS, LANES)], axis=2)
    k = jnp.concatenate([k_nope.reshape(t, MLA_HEADS, LANES),
                         jnp.broadcast_to(k_rot[:, None, :], (t, MLA_HEADS, LANES))], axis=2)
    names = tuple(carried)
    o, blocks = attention(q.reshape(t, -1).astype(BF16), k.reshape(t, -1).astype(BF16), v.astype(BF16),
                          tuple(carried[n] for n in names))
    arrived = dict(zip(names, blocks))
    p = dict(p)
    for n in POST:
        p[n] = _usable(n, arrived[(n, 0)])
    y_a = mm(o, p['w_o_mla'])

    o = hgrn_scan(mm(h, w_hq), mm(h, w_hf), mm(h, w_hi), lb[None])
    o = hgrn_out(p['hgrn_norm'][None], o, mm(h, w_hg))[0]
    y_b = mm(o, p['w_o_hgrn'])

    xbc = silu_op(conv(p['ssm_conv_w'], p['ssm_conv_b'][None], mm(h, w_xbc)))[0]
    dtx = dt_expand(p['ssm_dt_bias'][None], mm(h, w_dt))[0]
    y = ssd_scan(xbc, dtx, jnp.repeat(p['ssm_a_log'], SSM_HEADDIM)[None], jnp.repeat(p['ssm_d'], SSM_HEADDIM)[None])
    y = ssm_norm(p['ssm_norm'][None], y, mm(h, w_z))[0]
    y_c = mm(y, p['w_o_ssm'])

    merged = merge(y_a, y_b, y_c, mm(h, w_ga), mm(h, w_gb), mm(h, w_gc))[0]
    x = x + mm(merged, p['w_out'])
    return _ffn(x, p['ffn2_norm'], p['ffn2_wi'], p['ffn2_wo']), arrived


def _usable(name, blocks):
    return blocks.reshape(-1, blocks.shape[-1]) if name in ROW_SHARDED else blocks


def _local_loss(shards, params, x, target):
    depth = params['ffn1_norm'].shape[0]
    cos, sin = _rope_tables(x.shape[0])
    prob = jax.nn.softmax(params['hgrn_lb_logits'], axis=0)
    lower = jnp.cumsum(prob, axis=0) - prob[0:1]
    pre = {n: gather_op(shards[n][0]) for n in PRE}
    for layer in range(depth):
        p = {n: params[n][layer] for n in REPLICATED + (CONV_W,) if n != 'final_norm'}
        for n in PRE:
            p[n] = _usable(n, pre[n])
        carried = {(n, 0): shards[n][layer] for n in POST}
        if layer + 1 < depth:
            carried.update({(n, 1): shards[n][layer + 1] for n in PRE})
        x, arrived = _layer(x, p, lower[layer], cos, sin, carried)
        pre = {n: arrived.get((n, 1)) for n in PRE}
    return jnp.sum(loss_rows(params['final_norm'][None], x, target)[0])


def _pack_vec(arrays, rows):
    flat = jnp.concatenate([a.reshape(-1) for a in arrays])
    return jnp.pad(flat, (0, rows * COMM_COLS - flat.shape[0])).reshape(rows, COMM_COLS)


def _unpack(flat, shapes, lead=()):
    flat = flat.reshape(lead + (-1,))
    outs, off = [], 0
    for s in shapes:
        n = 1
        for dim in s:
            n *= dim
        outs.append(flat[..., off:off + n].reshape(lead + tuple(s)))
        off += n
    return outs


def _round_up(n, m):
    return -(-n // m) * m


def _step(a):
    x = a['x'][0]
    target = a['loss_target'][0]
    me = 4 * lax.axis_index("x") + 2 * lax.axis_index("y") + lax.axis_index("c")

    conv_shape = a[CONV_W].shape
    conv_full_shape = conv_shape[:-1] + (conv_shape[-1] * N_DEV,)
    rep_shapes = [a[n].shape for n in REPLICATED]
    n_small = 1 + sum(a[n].size for n in REPLICATED) + a[CONV_W].size * N_DEV
    small_rows = _round_up(-(-n_small // COMM_COLS), 8)

    params = {}
    conv_blocks = _unpack(all_gather(_pack_vec([a[CONV_W]], small_rows)), [conv_shape], lead=(N_DEV,))[0]
    params[CONV_W] = jnp.moveaxis(conv_blocks, 0, -2).reshape(conv_full_shape)
    for n in REPLICATED:
        params[n] = a[n]
    depth = a['ffn1_norm'].shape[0]
    shards = {n: [a[n][layer] for layer in range(depth)] for n in BIG}

    loss, (gs, gp, gx) = jax.value_and_grad(_local_loss, argnums=(0, 1, 2))(shards, params, x, target)

    big_out = [{}, {}, {}, {}]
    for n in BIG:
        width = a[n].shape[-1]
        grad = jnp.stack(gs[n]).reshape(1, -1, width)
        res = reduce_adamw(grad, *[a[pre + n].reshape(-1, width) for pre in ('', 'm_', 'v_')])
        for kind in range(4):
            big_out[kind][n] = res[kind].reshape(a[n].shape)

    small = _pack_vec([loss.reshape(1)] + [gp[n] for n in REPLICATED] + [gp[CONV_W]], small_rows)
    zero1, one1 = jnp.zeros((1,), F32), jnp.ones((1,), F32)
    zero_c, one_c = jnp.zeros(conv_full_shape, F32), jnp.ones(conv_full_shape, F32)
    small_w = _pack_vec([zero1] + [a[n] for n in REPLICATED] + [zero_c], small_rows)
    small_m = _pack_vec([zero1] + [a['m_' + n] for n in REPLICATED] + [zero_c], small_rows)
    small_v = _pack_vec([one1] + [a['v_' + n] for n in REPLICATED] + [one_c], small_rows)
    res = reduce_adamw(all_gather(small), small_w, small_m, small_v)
    small_out = []
    for r in res:
        pieces = _unpack(r, [(1,)] + rep_shapes + [conv_full_shape])
        small_out.append((pieces[0], dict(zip(REPLICATED, pieces[1:-1])), pieces[-1]))
    total_loss = small_out[0][0][0]

    width = conv_shape[-1]
    g_conv = lax.dynamic_slice_in_dim(small_out[0][2], me * width, width, axis=len(conv_shape) - 1)
    conv_rows = -(-a[CONV_W].size // COMM_COLS)
    conv_res = reduce_adamw(_pack_vec([g_conv], conv_rows)[None],
                            *[_pack_vec([a[pre + CONV_W]], conv_rows) for pre in ('', 'm_', 'v_')])
    conv_out = [_unpack(r, [conv_shape])[0] for r in conv_res]

    outs = [total_loss, gx[None]]
    for kind in range(4):
        for n in WEIGHTS:
            if n in BIG:
                outs.append(big_out[kind][n])
            elif n == CONV_W:
                outs.append(conv_out[kind])
            else:
                outs.append(small_out[kind][1][n])
    return tuple(outs)


def kernel(x, ffn1_norm, ffn1_wi, ffn1_wo, mix_norm, w_in, mla_q_norm, mla_w_uq, mla_kv_norm, mla_w_ukv, hgrn_lb_logits, hgrn_norm, ssm_conv_w, ssm_conv_b, ssm_a_log, ssm_dt_bias, ssm_d, ssm_norm, w_o_mla, w_o_hgrn, w_o_ssm, w_out, ffn2_norm, ffn2_wi, ffn2_wo, final_norm, loss_target, m_ffn1_norm, m_ffn1_wi, m_ffn1_wo, m_mix_norm, m_w_in, m_mla_q_norm, m_mla_w_uq, m_mla_kv_norm, m_mla_w_ukv, m_hgrn_lb_logits, m_hgrn_norm, m_ssm_conv_w, m_ssm_conv_b, m_ssm_a_log, m_ssm_dt_bias, m_ssm_d, m_ssm_norm, m_w_o_mla, m_w_o_hgrn, m_w_o_ssm, m_w_out, m_ffn2_norm, m_ffn2_wi, m_ffn2_wo, m_final_norm, v_ffn1_norm, v_ffn1_wi, v_ffn1_wo, v_mix_norm, v_w_in, v_mla_q_norm, v_mla_w_uq, v_mla_kv_norm, v_mla_w_ukv, v_hgrn_lb_logits, v_hgrn_norm, v_ssm_conv_w, v_ssm_conv_b, v_ssm_a_log, v_ssm_dt_bias, v_ssm_d, v_ssm_norm, v_w_o_mla, v_w_o_hgrn, v_w_o_ssm, v_w_out, v_ffn2_norm, v_ffn2_wi, v_ffn2_wo, v_final_norm):
    return _step(dict(locals()))
```

```python
import functools

import jax
import jax.numpy as jnp
from jax import lax
from jax.experimental import pallas as pl
from jax.experimental.pallas import tpu as pltpu

F32 = jnp.float32
BF16 = jnp.bfloat16
HI = lax.Precision.HIGHEST
MESH = pl.DeviceIdType.MESH

EPS = 1e-6
CHUNK = 64
N_DEV = 8

MLA_HEADS = 16
MLA_Q_RANK = 512
MLA_KV_RANK = 512
MLA_NOPE = 128
MLA_ROPE = 64
MLA_V = 128
ROPE_THETA = 10000.0
HG_HEADS = 16
HG_DK = 128
HG_WIDTH = HG_HEADS * HG_DK
SSM_HEADDIM = 64
SSM_GROUPS = 8
SSM_STATE = 128
SSM_CONV = 4

ADAM_LR = 0.001
ADAM_B1 = 0.9
ADAM_B2 = 0.999
ADAM_EPS = 1e-08
ADAM_WD = 0.01
ADAM_STEP = 10

LANES = 128
VMEM_LIMIT = 48 * 1024 * 1024
ROW_BLOCK_ELEMS = 256 * 1024
MM_TILE_M, MM_TILE_N, MM_TILE_K = 1408, 1536, 1536
MM_FULL_K = 2048
COMM_COLS = 1024
COMM_BYTES = 128 * 1024 * 1024


def _cparams(*sem):
    return pltpu.CompilerParams(dimension_semantics=sem, vmem_limit_bytes=VMEM_LIMIT)


def _tile(dim, pref):
    best = dim
    for t in range(LANES, min(dim, pref) + 1, LANES):
        if dim % t == 0:
            best = t
    return best


def _mm_call(a, b, mode, out_dtype, name):
    if mode == "nn":
        (m, k), (k2, n) = a.shape, b.shape
    elif mode == "nt":
        (m, k), (n, k2) = a.shape, b.shape
    else:
        (k, m), (k2, n) = a.shape, b.shape
    assert k == k2, (a.shape, b.shape, mode)
    tm, tn = _tile(m, MM_TILE_M), _tile(n, MM_TILE_N)
    tk = k if k <= MM_FULL_K else _tile(k, MM_TILE_K)
    nk = k // tk
    if mode == "nn":
        a_spec = pl.BlockSpec((tm, tk), lambda i, j, kk: (i, kk))
        b_spec = pl.BlockSpec((tk, tn), lambda i, j, kk: (kk, j))
        dims = (((1,), (0,)), ((), ()))
    elif mode == "nt":
        a_spec = pl.BlockSpec((tm, tk), lambda i, j, kk: (i, kk))
        b_spec = pl.BlockSpec((tn, tk), lambda i, j, kk: (j, kk))
        dims = (((1,), (1,)), ((), ()))
    else:
        a_spec = pl.BlockSpec((tk, tm), lambda i, j, kk: (kk, i))
        b_spec = pl.BlockSpec((tk, tn), lambda i, j, kk: (kk, j))
        dims = (((0,), (0,)), ((), ()))

    in_place = out_dtype == F32

    def body(a_ref, b_ref, o_ref, *scratch):
        kk = pl.program_id(2)

        def product():
            return lax.dot_general(a_ref[...].astype(BF16), b_ref[...].astype(BF16), dims,
                                   preferred_element_type=F32)

        if nk == 1:
            o_ref[...] = product().astype(o_ref.dtype)
            return
        acc_ref = o_ref if in_place else scratch[0]

        @pl.when(kk == 0)
        def _():
            acc_ref[...] = jnp.zeros_like(acc_ref)

        acc_ref[...] += product()

        if not in_place:
            @pl.when(kk == nk - 1)
            def _():
                o_ref[...] = acc_ref[...].astype(o_ref.dtype)

    return pl.pallas_call(
        body, grid=(m // tm, n // tn, nk), in_specs=[a_spec, b_spec],
        out_specs=pl.BlockSpec((tm, tn), lambda i, j, kk: (i, j)),
        out_shape=jax.ShapeDtypeStruct((m, n), out_dtype),
        scratch_shapes=[] if (nk == 1 or in_place) else [pltpu.VMEM((tm, tn), F32)],
        compiler_params=_cparams("parallel", "parallel", "arbitrary"), name=name)(a, b)


@jax.custom_vjp
def mm(a, w):
    return _mm_call(a.astype(BF16), w, "nn", F32, "mm_fwd")


def _mm_fwd(a, w):
    a16 = a.astype(BF16)
    return _mm_call(a16, w, "nn", F32, "mm_fwd"), (a16, w)


def _mm_bwd(res, g):
    a16, w = res
    g16 = g.astype(BF16)
    return _mm_call(g16, w, "nt", F32, "mm_da"), _mm_call(a16, g16, "tn", w.dtype, "mm_dw")


mm.defvjp(_mm_fwd, _mm_bwd)


def _row_block(t, widths):
    bt = 8
    while 2 * bt * max(widths) <= ROW_BLOCK_ELEMS:
        bt *= 2
    while t % bt:
        bt //= 2
    return bt


def make_rowwise(fn, name, n_par, group_width=None, shared=(), nondiff=()):
    def specs(args):
        t = max(a.shape[0] for a in args)
        cut = next(a for i, a in enumerate(args) if i >= n_par and i not in shared)
        gw = group_width(cut.shape[1]) if callable(group_width) else group_width
        groups = cut.shape[1] // gw if gw else 1
        ws = [a.shape[1] if i in shared else a.shape[1] // groups for i, a in enumerate(args)]
        ows = out_widths(ws)
        bt = _row_block(t, ws + ows)
        sp = []
        for i, a in enumerate(args):
            col = (lambda g: 0) if i in shared else (lambda g: g)
            if i < n_par:
                sp.append(pl.BlockSpec((1, ws[i]), lambda g, r, col=col: (0, col(g))))
            else:
                sp.append(pl.BlockSpec((bt, ws[i]), lambda g, r, col=col: (r, col(g))))
        return t, bt, groups, ows, sp

    def out_widths(ws):
        blocks = [jax.ShapeDtypeStruct((1 if i < n_par else 8, w), F32) for i, w in enumerate(ws)]
        return [o.shape[1] for o in jax.eval_shape(fn, *blocks)]

    def fwd_call(*args):
        t, bt, groups, ows, in_specs = specs(args)
        n_in = len(args)

        def body(*refs):
            outs = fn(*[r[...] for r in refs[:n_in]])
            for r, o in zip(refs[n_in:], outs):
                r[...] = o

        return pl.pallas_call(
            body, grid=(groups, t // bt), in_specs=in_specs,
            out_specs=[pl.BlockSpec((bt, w), lambda g, r: (r, g)) for w in ows],
            out_shape=[jax.ShapeDtypeStruct((t, w * groups), F32) for w in ows],
            compiler_params=_cparams("parallel", "parallel"), name=name + "_fwd")(*args)

    def bwd_call(args, gs):
        t, bt, groups, ows, in_specs = specs(args)
        n_in, n_out = len(args), len(gs)
        diff = [i for i in range(n_in) if i not in nondiff]
        g_specs = [pl.BlockSpec((bt, w), lambda g, r: (r, g)) for w in ows]
        o_specs, o_shapes = [], []
        for i in diff:
            o_specs.append(in_specs[i])
            o_shapes.append(jax.ShapeDtypeStruct(args[i].shape, F32))

        def body(*refs):
            r_idx = pl.program_id(1)
            vals = [r[...] for r in refs[:n_in]]
            cts = tuple(r[...] for r in refs[n_in:n_in + n_out])

            def f_diff(*dv):
                full = list(vals)
                for i, v in zip(diff, dv):
                    full[i] = v
                return tuple(fn(*full))

            _, vjp = jax.vjp(f_diff, *[vals[i] for i in diff])
            grads = vjp(cts)
            for i, g_val, ref in zip(diff, grads, refs[n_in + n_out:]):
                if i < n_par:
                    @pl.when(r_idx == 0)
                    def _(ref=ref, g_val=g_val):
                        ref[...] = g_val

                    @pl.when(r_idx != 0)
                    def _(ref=ref, g_val=g_val):
                        ref[...] += g_val
                else:
                    ref[...] = g_val

        outs = pl.pallas_call(
            body, grid=(groups, t // bt), in_specs=in_specs + g_specs, out_specs=o_specs, out_shape=o_shapes,
            compiler_params=_cparams("parallel", "arbitrary"), name=name + "_bwd")(*args, *gs)
        full = [jnp.zeros_like(a) for a in args]
        for i, o in zip(diff, outs):
            full[i] = o
        return tuple(full)

    @jax.custom_vjp
    def op(*args):
        return tuple(fwd_call(*args))

    def op_fwd(*args):
        return tuple(fwd_call(*args)), args

    def op_bwd(args, gs):
        return bwd_call(args, gs)

    op.defvjp(op_fwd, op_bwd)
    return op


def _silu(x):
    return x * jax.nn.sigmoid(x)


def _rmsnorm_fn(w, x):
    return (x * lax.rsqrt(jnp.mean(x * x, axis=-1, keepdims=True) + EPS) * w,)


def _swiglu_fn(g, u):
    return (_silu(g) * u,)


def _silu_fn(x):
    return (_silu(x),)


def _rope_fn(x, cos, sin):
    i = lax.broadcasted_iota(jnp.int32, (LANES, LANES), 0)
    j = lax.broadcasted_iota(jnp.int32, (LANES, LANES), 1)
    half = MLA_ROPE // 2
    first = (j % MLA_ROPE) < half
    p = jnp.where(first & (i == j + half), -1.0, 0.0) + jnp.where((~first) & (i == j - half), 1.0, 0.0)
    return (x * cos + jnp.dot(x, p.astype(F32), precision=HI) * sin,)


def _hgrn_out_fn(w, o, g):
    return (o * lax.rsqrt(jnp.mean(o * o, axis=-1, keepdims=True) + EPS) * w * _silu(g),)


def _softplus(x):
    return jnp.maximum(x, 0.0) + jnp.log(1.0 + jnp.exp(-jnp.abs(x)))


def _dt_expand_fn(bias, dt_raw):
    nh = dt_raw.shape[1]
    h = lax.broadcasted_iota(jnp.int32, (nh, nh * SSM_HEADDIM), 0)
    c = lax.broadcasted_iota(jnp.int32, (nh, nh * SSM_HEADDIM), 1)
    e = (c // SSM_HEADDIM == h).astype(F32)
    return (jnp.dot(_softplus(dt_raw + bias), e, precision=HI),)


def _ssm_norm_fn(w, y, z):
    y = y * _silu(z)
    return (y * lax.rsqrt(jnp.mean(y * y, axis=-1, keepdims=True) + EPS) * w,)


def _merge_fn(ya, yb, yc, ga, gb, gc):
    return (jax.nn.sigmoid(ga) * ya + jax.nn.sigmoid(gb) * yb + jax.nn.sigmoid(gc) * yc,)


def _loss_fn(w, x, tgt):
    y = x * lax.rsqrt(jnp.mean(x * x, axis=-1, keepdims=True) + EPS) * w
    err = y - tgt
    return (0.5 * jnp.mean(err * err, axis=-1, keepdims=True),)


rmsnorm = make_rowwise(_rmsnorm_fn, "rmsnorm", 1)
swiglu = make_rowwise(_swiglu_fn, "swiglu", 0, group_width=512)
silu_op = make_rowwise(_silu_fn, "silu", 0, group_width=512)
rope = make_rowwise(_rope_fn, "rope", 0, group_width=LANES, shared=(1, 2), nondiff=(1, 2))
hgrn_out = make_rowwise(_hgrn_out_fn, "hgrn_out", 1, group_width=HG_DK)
dt_expand = make_rowwise(_dt_expand_fn, "dt_expand", 1)
ssm_norm = make_rowwise(_ssm_norm_fn, "ssm_norm", 1, group_width=lambda w: w // SSM_GROUPS)
merge = make_rowwise(_merge_fn, "merge", 0, group_width=512)
loss_rows = make_rowwise(_loss_fn, "loss", 1, nondiff=(2,))


HALO = 8


def _conv_blocks(t, c):
    cw = _tile(c, 512)
    bt = _row_block(t, [cw])
    return cw, bt


def _conv_fwd_call(w, b, x):
    t, c = x.shape
    cw, bt = _conv_blocks(t, c)
    hb = bt // HALO

    def body(w_ref, b_ref, x_ref, prev_ref, y_ref):
        r = pl.program_id(1)
        prev = jnp.where(r == 0, 0.0, prev_ref[...])
        xx = jnp.concatenate([prev, x_ref[...]], axis=0)
        acc = jnp.zeros((bt, cw), F32) + b_ref[...]
        for k in range(SSM_CONV):
            sh = SSM_CONV - 1 - k
            xs = xx if sh == 0 else pltpu.roll(xx, sh, axis=0)
            acc = acc + w_ref[k:k + 1, :] * xs[HALO:, :]
        y_ref[...] = acc

    return pl.pallas_call(
        body, grid=(c // cw, t // bt),
        in_specs=[pl.BlockSpec((SSM_CONV, cw), lambda g, r: (0, g)), pl.BlockSpec((1, cw), lambda g, r: (0, g)),
                  pl.BlockSpec((bt, cw), lambda g, r: (r, g)),
                  pl.BlockSpec((HALO, cw), lambda g, r: (jnp.maximum(r * hb - 1, 0), g))],
        out_specs=pl.BlockSpec((bt, cw), lambda g, r: (r, g)),
        out_shape=jax.ShapeDtypeStruct((t, c), F32),
        compiler_params=_cparams("parallel", "parallel"), name="conv_fwd")(w, b, x, x)


def _conv_bwd_call(w, x, dy):
    t, c = x.shape
    cw, bt = _conv_blocks(t, c)
    hb = bt // HALO
    nr = t // bt

    def body(w_ref, x_ref, prev_ref, dy_ref, next_ref, dx_ref, dw_ref, db_ref):
        r = pl.program_id(1)
        prev = jnp.where(r == 0, 0.0, prev_ref[...])
        nxt = jnp.where(r == nr - 1, 0.0, next_ref[...])
        xx = jnp.concatenate([prev, x_ref[...]], axis=0)
        dd = jnp.concatenate([dy_ref[...], nxt], axis=0)
        dy_val = dy_ref[...]
        dx = jnp.zeros((bt, cw), F32)
        dws = []
        for k in range(SSM_CONV):
            sh = SSM_CONV - 1 - k
            xs = xx if sh == 0 else pltpu.roll(xx, sh, axis=0)
            ds = dd if sh == 0 else pltpu.roll(dd, bt + HALO - sh, axis=0)
            dx = dx + w_ref[k:k + 1, :] * ds[:bt, :]
            dws.append(jnp.sum(dy_val * xs[HALO:, :], axis=0, keepdims=True))
        dx_ref[...] = dx
        dw = jnp.concatenate(dws, axis=0)
        db = jnp.sum(dy_val, axis=0, keepdims=True)

        @pl.when(r == 0)
        def _():
            dw_ref[...] = dw
            db_ref[...] = db

        @pl.when(r != 0)
        def _():
            dw_ref[...] += dw
            db_ref[...] += db

    return pl.pallas_call(
        body, grid=(c // cw, nr),
        in_specs=[pl.BlockSpec((SSM_CONV, cw), lambda g, r: (0, g)),
                  pl.BlockSpec((bt, cw), lambda g, r: (r, g)),
                  pl.BlockSpec((HALO, cw), lambda g, r: (jnp.maximum(r * hb - 1, 0), g)),
                  pl.BlockSpec((bt, cw), lambda g, r: (r, g)),
                  pl.BlockSpec((HALO, cw), lambda g, r: (jnp.minimum((r + 1) * hb, nr * hb - 1), g))],
        out_specs=[pl.BlockSpec((bt, cw), lambda g, r: (r, g)),
                   pl.BlockSpec((SSM_CONV, cw), lambda g, r: (0, g)), pl.BlockSpec((1, cw), lambda g, r: (0, g))],
        out_shape=[jax.ShapeDtypeStruct((t, c), F32), jax.ShapeDtypeStruct((SSM_CONV, c), F32),
                   jax.ShapeDtypeStruct((1, c), F32)],
        compiler_params=_cparams("parallel", "arbitrary"), name="conv_bwd")(w, x, x, dy, dy)


@jax.custom_vjp
def conv(w, b, x):
    return _conv_fwd_call(w, b, x)


def _conv_vjp_fwd(w, b, x):
    return _conv_fwd_call(w, b, x), (w, x)


def _conv_vjp_bwd(res, dy):
    w, x = res
    dx, dw, db = _conv_bwd_call(w, x, dy)
    return dw, db, dx


conv.defvjp(_conv_vjp_fwd, _conv_vjp_bwd)


ATT_DQK = 2 * LANES
ATT_SCALE = (MLA_NOPE + MLA_ROPE) ** -0.5
NEG = -1e30


def _att_tiles(t):
    tq = min(512, max(CHUNK, t // 4))
    tk = min(2 * tq, t)
    return tq, tk, tk // tq


def _att_mask(s, i, j, tq, tk):
    row = (i * tq + lax.broadcasted_iota(jnp.int32, (tq, tk), 0)) // CHUNK
    col = (j * tk + lax.broadcasted_iota(jnp.int32, (tq, tk), 1)) // CHUNK
    return jnp.where(col <= row, s, NEG)


def _grid_marks(n0, n1, n2):
    a, b, c = pl.program_id(0), pl.program_id(1), pl.program_id(2)
    inner0 = (b == 0) & (c == 0)
    return (a == 0) & inner0, (a == n0 // 2) & inner0, (a == n0 - 1) & (b == n1 - 1) & (c == n2 - 1)


def _att_fwd_call(q, k, v, shards=()):
    t = q.shape[0]
    h = q.shape[1] // ATT_DQK
    dv = v.shape[1] // h
    tq, tk, ratio = _att_tiles(t)
    nq, nk = t // tq, t // tk
    dims_nt = (((1,), (1,)), ((), ()))
    n_c = len(shards)

    def body(q_ref, k_ref, v_ref, *rest):
        x_refs, rest = rest[:n_c], rest[n_c:]
        o_ref, lse_ref = rest[:2]
        g_refs, rest = rest[2:2 + n_c], rest[2 + n_c:]
        m_scr, l_scr, acc_scr = rest[:3]
        i, j = pl.program_id(1), pl.program_id(2)
        if n_c:
            start, middle, finish = _gather_phases(x_refs, g_refs, *rest[3:])
            first, mid, last = _grid_marks(h, nq, nk)
            pl.when(first)(start)
            pl.when(mid)(middle)

        @pl.when(j == 0)
        def _():
            m_scr[...] = jnp.full_like(m_scr, NEG)
            l_scr[...] = jnp.zeros_like(l_scr)
            acc_scr[...] = jnp.zeros_like(acc_scr)

        def step(masked):
            s = lax.dot_general(q_ref[...], k_ref[...], dims_nt, preferred_element_type=F32) * ATT_SCALE
            if masked:
                s = _att_mask(s, i, j, tq, tk)
            m_new = jnp.maximum(m_scr[...], jnp.max(s, axis=-1, keepdims=True))
            alpha = jnp.exp(m_scr[...] - m_new)
            p = jnp.exp(s - m_new)
            l_scr[...] = alpha * l_scr[...] + jnp.sum(p, axis=-1, keepdims=True)
            acc_scr[...] = alpha * acc_scr[...] + jnp.dot(p.astype(BF16), v_ref[...], preferred_element_type=F32)
            m_scr[...] = m_new

        pl.when(j < i // ratio)(functools.partial(step, False))

        @pl.when(j == i // ratio)
        def _():
            step(True)
            o_ref[...] = acc_scr[...] / l_scr[...]
            lse_ref[0] = m_scr[...] + jnp.log(l_scr[...])

        if n_c:
            pl.when(last)(finish)

    outs = pl.pallas_call(
        body, grid=(h, nq, nk),
        in_specs=[pl.BlockSpec((tq, ATT_DQK), lambda hh, i, j: (i, hh)),
                  pl.BlockSpec((tk, ATT_DQK), lambda hh, i, j: (jnp.minimum(j, i // ratio), hh)),
                  pl.BlockSpec((tk, dv), lambda hh, i, j: (jnp.minimum(j, i // ratio), hh))] + [ANY] * n_c,
        out_specs=[pl.BlockSpec((tq, dv), lambda hh, i, j: (i, hh)),
                   pl.BlockSpec((1, tq, 1), lambda hh, i, j: (hh, i, 0))] + [ANY] * n_c,
        out_shape=[jax.ShapeDtypeStruct((t, h * dv), F32), jax.ShapeDtypeStruct((h, t, 1), F32)]
        + [jax.ShapeDtypeStruct((N_DEV,) + s.shape, s.dtype) for s in shards],
        scratch_shapes=[pltpu.VMEM((tq, 1), F32), pltpu.VMEM((tq, 1), F32), pltpu.VMEM((tq, dv), F32)]
        + (_comm_scratch(n_c) if n_c else []),
        compiler_params=_cparams("arbitrary", "arbitrary", "arbitrary"), name="att_fwd")(q, k, v, *shards)
    return outs[0], outs[1], tuple(outs[2:])


def _att_dq_call(q, k, v, o, lse, do, grads=()):
    t = q.shape[0]
    h = q.shape[1] // ATT_DQK
    dv = v.shape[1] // h
    tq, tk, ratio = _att_tiles(t)
    nq, nk = t // tq, t // tk
    dims_nt = (((1,), (1,)), ((), ()))
    n_c = len(grads)

    def body(q_ref, k_ref, v_ref, o_ref, lse_ref, do_ref, *rest):
        x_refs, rest = rest[:n_c], rest[n_c:]
        dq_ref, delta_ref = rest[:2]
        p_refs, rest = rest[2:2 + n_c], rest[2 + n_c:]
        acc_scr, d_scr = rest[:2]
        i, j = pl.program_id(1), pl.program_id(2)
        if n_c:
            start, finish = _exchange_phases(x_refs, p_refs, *rest[2:])
            first, _, last = _grid_marks(h, nq, nk)
            pl.when(first)(start)

        @pl.when(j == 0)
        def _():
            acc_scr[...] = jnp.zeros_like(acc_scr)
            d_scr[...] = jnp.sum(do_ref[...] * o_ref[...], axis=-1, keepdims=True)

        def step(masked):
            s = lax.dot_general(q_ref[...], k_ref[...], dims_nt, preferred_element_type=F32) * ATT_SCALE
            if masked:
                s = _att_mask(s, i, j, tq, tk)
            p = jnp.exp(s - lse_ref[0])
            dp = lax.dot_general(do_ref[...].astype(BF16), v_ref[...], dims_nt, preferred_element_type=F32)
            ds = p * (dp - d_scr[...]) * ATT_SCALE
            acc_scr[...] += jnp.dot(ds.astype(BF16), k_ref[...], preferred_element_type=F32)

        pl.when(j < i // ratio)(functools.partial(step, False))

        @pl.when(j == i // ratio)
        def _():
            step(True)
            dq_ref[...] = acc_scr[...].astype(dq_ref.dtype)
            delta_ref[0] = d_scr[...]

        if n_c:
            pl.when(last)(finish)

    outs = pl.pallas_call(
        body, grid=(h, nq, nk),
        in_specs=[pl.BlockSpec((tq, ATT_DQK), lambda hh, i, j: (i, hh)),
                  pl.BlockSpec((tk, ATT_DQK), lambda hh, i, j: (jnp.minimum(j, i // ratio), hh)),
                  pl.BlockSpec((tk, dv), lambda hh, i, j: (jnp.minimum(j, i // ratio), hh)),
                  pl.BlockSpec((tq, dv), lambda hh, i, j: (i, hh)),
                  pl.BlockSpec((1, tq, 1), lambda hh, i, j: (hh, i, 0)),
                  pl.BlockSpec((tq, dv), lambda hh, i, j: (i, hh))] + [ANY] * n_c,
        out_specs=[pl.BlockSpec((tq, ATT_DQK), lambda hh, i, j: (i, hh)),
                   pl.BlockSpec((1, tq, 1), lambda hh, i, j: (hh, i, 0))] + [ANY] * n_c,
        out_shape=[jax.ShapeDtypeStruct(q.shape, q.dtype), jax.ShapeDtypeStruct((h, t, 1), F32)]
        + [jax.ShapeDtypeStruct(g.shape, g.dtype) for g in grads],
        scratch_shapes=[pltpu.VMEM((tq, ATT_DQK), F32), pltpu.VMEM((tq, 1), F32)]
        + (_comm_scratch(n_c) if n_c else []),
        compiler_params=_cparams("arbitrary", "arbitrary", "arbitrary"), name="att_dq")(q, k, v, o, lse, do, *grads)
    return outs[0], outs[1], tuple(outs[2:])


def _att_dkv_call(q, k, v, lse, delta, do, grads=()):
    t = q.shape[0]
    h = q.shape[1] // ATT_DQK
    dv = v.shape[1] // h
    tq, tk, ratio = _att_tiles(t)
    nq, nk = t // tq, t // tk
    dims_nt = (((1,), (1,)), ((), ()))
    dims_tn = (((0,), (0,)), ((), ()))
    n_c = len(grads)

    def body(q_ref, k_ref, v_ref, lse_ref, delta_ref, do_ref, *rest):
        x_refs, rest = rest[:n_c], rest[n_c:]
        dk_ref, dv_ref = rest[:2]
        p_refs, rest = rest[2:2 + n_c], rest[2 + n_c:]
        dk_scr, dv_scr = rest[:2]
        j, i = pl.program_id(1), pl.program_id(2)
        if n_c:
            start, finish = _exchange_phases(x_refs, p_refs, *rest[2:])
            first, _, last = _grid_marks(h, nk, nq)
            pl.when(first)(start)

        @pl.when(i == 0)
        def _():
            dk_scr[...] = jnp.zeros_like(dk_scr)
            dv_scr[...] = jnp.zeros_like(dv_scr)

        def step(masked):
            s = lax.dot_general(q_ref[...], k_ref[...], dims_nt, preferred_element_type=F32) * ATT_SCALE
            if masked:
                s = _att_mask(s, i, j, tq, tk)
            p = jnp.exp(s - lse_ref[0])
            do_b = do_ref[...].astype(BF16)
            dv_scr[...] += lax.dot_general(p.astype(BF16), do_b, dims_tn, preferred_element_type=F32)
            dp = lax.dot_general(do_b, v_ref[...], dims_nt, preferred_element_type=F32)
            ds = p * (dp - delta_ref[0]) * ATT_SCALE
            dk_scr[...] += lax.dot_general(ds.astype(BF16), q_ref[...], dims_tn, preferred_element_type=F32)

        pl.when(i // ratio > j)(functools.partial(step, False))
        pl.when(i // ratio == j)(functools.partial(step, True))

        @pl.when(i == nq - 1)
        def _():
            dk_ref[...] = dk_scr[...].astype(dk_ref.dtype)
            dv_ref[...] = dv_scr[...].astype(dv_ref.dtype)

        if n_c:
            pl.when(last)(finish)

    def qi(i, j):
        return jnp.maximum(i, j * ratio)

    outs = pl.pallas_call(
        body, grid=(h, nk, nq),
        in_specs=[pl.BlockSpec((tq, ATT_DQK), lambda hh, j, i: (qi(i, j), hh)),
                  pl.BlockSpec((tk, ATT_DQK), lambda hh, j, i: (j, hh)),
                  pl.BlockSpec((tk, dv), lambda hh, j, i: (j, hh)),
                  pl.BlockSpec((1, tq, 1), lambda hh, j, i: (hh, qi(i, j), 0)),
                  pl.BlockSpec((1, tq, 1), lambda hh, j, i: (hh, qi(i, j), 0)),
                  pl.BlockSpec((tq, dv), lambda hh, j, i: (qi(i, j), hh))] + [ANY] * n_c,
        out_specs=[pl.BlockSpec((tk, ATT_DQK), lambda hh, j, i: (j, hh)),
                   pl.BlockSpec((tk, dv), lambda hh, j, i: (j, hh))] + [ANY] * n_c,
        out_shape=[jax.ShapeDtypeStruct(k.shape, k.dtype), jax.ShapeDtypeStruct(v.shape, v.dtype)]
        + [jax.ShapeDtypeStruct(g.shape, g.dtype) for g in grads],
        scratch_shapes=[pltpu.VMEM((tk, ATT_DQK), F32), pltpu.VMEM((tk, dv), F32)]
        + (_comm_scratch(n_c) if n_c else []),
        compiler_params=_cparams("arbitrary", "arbitrary", "arbitrary"), name="att_dkv")(
            q, k, v, lse, delta, do, *grads)
    return outs[0], outs[1], tuple(outs[2:])


@jax.custom_vjp
def attention(q, k, v, shards):
    o, _, gathered = _att_fwd_call(q, k, v, tuple(s.astype(BF16) for s in shards))
    return o, gathered


def _att_vjp_fwd(q, k, v, shards):
    o, lse, gathered = _att_fwd_call(q, k, v, tuple(s.astype(BF16) for s in shards))
    return (o, gathered), (q, k, v, o, lse)


def _att_vjp_bwd(res, cts):
    q, k, v, o, lse = res
    do, g_gathered = cts
    sizes = [g.size for g in g_gathered]
    cut = 0
    while cut < len(sizes) and 2 * sum(sizes[:cut + 1]) <= sum(sizes):
        cut += 1
    dq, delta, parts_a = _att_dq_call(q, k, v, o, lse, do, tuple(g_gathered[:cut]))
    dk, dv, parts_b = _att_dkv_call(q, k, v, lse, delta, do, tuple(g_gathered[cut:]))
    return dq, dk, dv, tuple(sum_parts(p) for p in parts_a + parts_b)


attention.defvjp(_att_vjp_fwd, _att_vjp_bwd)


MID = CHUNK // 2 - 1


def _tril(n):
    r = lax.broadcasted_iota(jnp.int32, (n, n), 0)
    c = lax.broadcasted_iota(jnp.int32, (n, n), 1)
    return c <= r


def _bdot(a, b, dims):
    return lax.dot_general(a.astype(BF16), b.astype(BF16), dims, preferred_element_type=F32)


NN = (((1,), (0,)), ((), ()))
NT = (((1,), (1,)), ((), ()))
TN = (((0,), (0,)), ((), ()))


def _hgrn_chunk(state, q_in, f_in, i_in, lb):
    tril = _tril(CHUNK)
    f = lb + (1.0 - lb) * jax.nn.sigmoid(f_in)
    logf = jnp.log(f)
    b = jnp.dot(tril.astype(F32), logf, precision=HI)
    q = _silu(q_in) * HG_DK ** -0.5
    k = 1.0 - f
    b_mid = b[MID:MID + 1, :]
    att = _bdot(q * jnp.exp(b - b_mid), k * jnp.exp(b_mid - b), NT)
    att = jnp.where(tril, att, 0.0)
    o = _bdot(q * jnp.exp(b), state, NT) + _bdot(att, i_in, NN)
    b_last = b[CHUNK - 1:CHUNK, :]
    new_state = jnp.exp(b_last) * state + _bdot(i_in, k * jnp.exp(b_last - b), TN)
    return o, new_state


HG_STEP_CHUNKS = 8


def _hgrn_step_rows(t):
    n = HG_STEP_CHUNKS
    while (t // CHUNK) % n:
        n //= 2
    return n * CHUNK


def _hgrn_chunks(state, q_in, f_in, i_in, lb):
    outs = []
    for c in range(q_in.shape[0] // CHUNK):
        rows = slice(c * CHUNK, (c + 1) * CHUNK)
        o, state = _hgrn_chunk(state, q_in[rows], f_in[rows], i_in[rows], lb)
        outs.append(o)
    return (outs[0] if len(outs) == 1 else jnp.concatenate(outs, axis=0)), state


def _hgrn_fwd_call(q, f, i, lb):
    t = q.shape[0]
    rows = _hgrn_step_rows(t)
    nc = t // rows
    blk = pl.BlockSpec((rows, HG_DK), lambda h, c: (c, h))

    def body(q_ref, f_ref, i_ref, lb_ref, o_ref, s_ref, s_scr):
        @pl.when(pl.program_id(1) == 0)
        def _():
            s_scr[...] = jnp.zeros_like(s_scr)

        s_ref[0, 0] = s_scr[...]
        o, ns = _hgrn_chunks(s_scr[...], q_ref[...], f_ref[...], i_ref[...], lb_ref[...])
        o_ref[...] = o
        s_scr[...] = ns

    return pl.pallas_call(
        body, grid=(HG_HEADS, nc),
        in_specs=[blk, blk, blk, pl.BlockSpec((1, HG_DK), lambda h, c: (0, h))],
        out_specs=[blk, pl.BlockSpec((1, 1, HG_DK, HG_DK), lambda h, c: (h, c, 0, 0))],
        out_shape=[jax.ShapeDtypeStruct((t, HG_WIDTH), F32), jax.ShapeDtypeStruct((HG_HEADS, nc, HG_DK, HG_DK), F32)],
        scratch_shapes=[pltpu.VMEM((HG_DK, HG_DK), F32)],
        compiler_params=_cparams("parallel", "arbitrary"), name="hgrn_fwd")(q, f, i, lb)


def _hgrn_bwd_call(q, f, i, lb, states, do):
    t = q.shape[0]
    rows = _hgrn_step_rows(t)
    nc = t // rows
    blk = pl.BlockSpec((rows, HG_DK), lambda h, c: (nc - 1 - c, h))
    row = pl.BlockSpec((1, HG_DK), lambda h, c: (0, h))

    def body(q_ref, f_ref, i_ref, lb_ref, s_ref, do_ref, dq_ref, df_ref, di_ref, dlb_ref, ds_scr):
        c = pl.program_id(1)

        @pl.when(c == 0)
        def _():
            ds_scr[...] = jnp.zeros_like(ds_scr)

        _, vjp = jax.vjp(_hgrn_chunks, s_ref[0, 0], q_ref[...], f_ref[...], i_ref[...], lb_ref[...])
        ds, dq, df, di, dlb = vjp((do_ref[...], ds_scr[...]))
        ds_scr[...] = ds
        dq_ref[...] = dq
        df_ref[...] = df
        di_ref[...] = di

        @pl.when(c == 0)
        def _():
            dlb_ref[...] = dlb

        @pl.when(c != 0)
        def _():
            dlb_ref[...] += dlb

    return pl.pallas_call(
        body, grid=(HG_HEADS, nc),
        in_specs=[blk, blk, blk, row, pl.BlockSpec((1, 1, HG_DK, HG_DK), lambda h, c: (h, nc - 1 - c, 0, 0)), blk],
        out_specs=[blk, blk, blk, row],
        out_shape=[jax.ShapeDtypeStruct((t, HG_WIDTH), F32)] * 3 + [jax.ShapeDtypeStruct((1, HG_WIDTH), F32)],
        scratch_shapes=[pltpu.VMEM((HG_DK, HG_DK), F32)],
        compiler_params=_cparams("parallel", "arbitrary"), name="hgrn_bwd")(q, f, i, lb, states, do)


@jax.custom_vjp
def hgrn_scan(q, f, i, lb):
    return _hgrn_fwd_call(q, f, i, lb)[0]


def _hgrn_vjp_fwd(q, f, i, lb):
    o, states = _hgrn_fwd_call(q, f, i, lb)
    return o, (q, f, i, lb, states)


def _hgrn_vjp_bwd(res, do):
    return tuple(_hgrn_bwd_call(*res, do))


hgrn_scan.defvjp(_hgrn_vjp_fwd, _hgrn_vjp_bwd)


def _ssd_chunk(state, xs, bm, cm, dtx, alog, dskip):
    assert CHUNK == SSM_HEADDIM and 2 * SSM_HEADDIM == LANES
    gw = xs.shape[1]
    trilf = _tril(CHUNK).astype(F32)
    da = dtx * (-jnp.exp(alog))
    a = jnp.dot(trilf, da, precision=HI)
    xdt = xs * dtx
    cb2 = _bdot(cm, jnp.concatenate([bm, bm], axis=0), NT)
    row = lax.broadcasted_iota(jnp.int32, (CHUNK, LANES), 0)
    src = lax.broadcasted_iota(jnp.int32, (CHUNK, LANES), 1) % CHUNK
    first_head = lax.broadcasted_iota(jnp.int32, (CHUNK, LANES), 1) < CHUNK
    ys = []
    for p in range(gw // LANES):
        lanes = slice(p * LANES, (p + 1) * LANES)
        a_src = jnp.sum(jnp.where(row <= src, da[:, lanes], 0.0), axis=0, keepdims=True)
        decay_ls = jnp.exp(jnp.where(src <= row, a[:, lanes] - a_src, NEG))
        x_pair = xdt[:, lanes]
        rhs = jnp.concatenate([jnp.where(first_head, x_pair, 0.0), jnp.where(first_head, 0.0, x_pair)], axis=0)
        ys.append(_bdot(cb2 * decay_ls, rhs, NN))
    y_diag = ys[0] if len(ys) == 1 else jnp.concatenate(ys, axis=1)
    y_off = jnp.exp(a) * _bdot(cm, state, NN)
    y = y_diag + y_off + xs * dskip
    a_last = a[CHUNK - 1:CHUNK, :]
    new_state = jnp.exp(a_last) * state + _bdot(bm, jnp.exp(a_last - a) * xdt, TN)
    return y, new_state


SSD_STEP_CHUNKS = 4


def _ssd_step_rows(t):
    n = SSD_STEP_CHUNKS
    while (t // CHUNK) % n:
        n //= 2
    return n * CHUNK


def _ssd_chunks(state, xs, bm, cm, dtx, alog, dskip):
    outs = []
    for c in range(xs.shape[0] // CHUNK):
        rows = slice(c * CHUNK, (c + 1) * CHUNK)
        y, state = _ssd_chunk(state, xs[rows], bm[rows], cm[rows], dtx[rows], alog, dskip)
        outs.append(y)
    return (outs[0] if len(outs) == 1 else jnp.concatenate(outs, axis=0)), state


def _ssd_specs(inner, rows, nc, rev):
    gw = inner // SSM_GROUPS
    nb = inner // SSM_STATE
    ci = (lambda c: nc - 1 - c) if rev else (lambda c: c)
    xs = pl.BlockSpec((rows, gw), lambda g, c: (ci(c), g))
    bm = pl.BlockSpec((rows, SSM_STATE), lambda g, c: (ci(c), nb + g))
    cm = pl.BlockSpec((rows, SSM_STATE), lambda g, c: (ci(c), nb + SSM_GROUPS + g))
    row = pl.BlockSpec((1, gw), lambda g, c: (0, g))
    st = pl.BlockSpec((1, 1, SSM_STATE, gw), lambda g, c: (g, ci(c), 0, 0))
    return gw, xs, bm, cm, row, st


def _ssd_fwd_call(xbc, dtx, alog, dskip):
    t, inner = dtx.shape
    rows = _ssd_step_rows(t)
    nc = t // rows
    gw, xs_s, bm_s, cm_s, row, st = _ssd_specs(inner, rows, nc, False)

    def body(xs_ref, bm_ref, cm_ref, dt_ref, a_ref, d_ref, y_ref, s_ref, s_scr):
        @pl.when(pl.program_id(1) == 0)
        def _():
            s_scr[...] = jnp.zeros_like(s_scr)

        s_ref[0, 0] = s_scr[...]
        y, ns = _ssd_chunks(s_scr[...], xs_ref[...], bm_ref[...], cm_ref[...], dt_ref[...], a_ref[...], d_ref[...])
        y_ref[...] = y
        s_scr[...] = ns

    return pl.pallas_call(
        body, grid=(SSM_GROUPS, nc), in_specs=[xs_s, bm_s, cm_s, xs_s, row, row],
        out_specs=[xs_s, st],
        out_shape=[jax.ShapeDtypeStruct((t, inner), F32), jax.ShapeDtypeStruct((SSM_GROUPS, nc, SSM_STATE, gw), F32)],
        scratch_shapes=[pltpu.VMEM((SSM_STATE, gw), F32)],
        compiler_params=_cparams("parallel", "arbitrary"), name="ssd_fwd")(xbc, xbc, xbc, dtx, alog, dskip)


def _ssd_bwd_call(xbc, dtx, alog, dskip, states, dy):
    t, inner = dtx.shape
    rows = _ssd_step_rows(t)
    nc = t // rows
    gw, xs_s, bm_s, cm_s, row, st = _ssd_specs(inner, rows, nc, True)
    gn = pl.BlockSpec((rows, SSM_STATE), lambda g, c: (nc - 1 - c, g))
    gn_shape = jax.ShapeDtypeStruct((t, SSM_GROUPS * SSM_STATE), F32)

    def body(xs_ref, bm_ref, cm_ref, dt_ref, a_ref, d_ref, s_ref, dy_ref,
             dxs_ref, dbm_ref, dcm_ref, ddt_ref, da_ref, dd_ref, ds_scr):
        c = pl.program_id(1)

        @pl.when(c == 0)
        def _():
            ds_scr[...] = jnp.zeros_like(ds_scr)

        _, vjp = jax.vjp(_ssd_chunks, s_ref[0, 0], xs_ref[...], bm_ref[...], cm_ref[...], dt_ref[...],
                         a_ref[...], d_ref[...])
        ds, dxs, dbm, dcm, ddt, da, dd = vjp((dy_ref[...], ds_scr[...]))
        ds_scr[...] = ds
        dxs_ref[...] = dxs
        dbm_ref[...] = dbm
        dcm_ref[...] = dcm
        ddt_ref[...] = ddt

        @pl.when(c == 0)
        def _():
            da_ref[...] = da
            dd_ref[...] = dd

        @pl.when(c != 0)
        def _():
            da_ref[...] += da
            dd_ref[...] += dd

    big = jax.ShapeDtypeStruct((t, inner), F32)
    small = jax.ShapeDtypeStruct((1, inner), F32)
    return pl.pallas_call(
        body, grid=(SSM_GROUPS, nc), in_specs=[xs_s, bm_s, cm_s, xs_s, row, row, st, xs_s],
        out_specs=[xs_s, gn, gn, xs_s, row, row],
        out_shape=[big, gn_shape, gn_shape, big, small, small],
        scratch_shapes=[pltpu.VMEM((SSM_STATE, gw), F32)],
        compiler_params=_cparams("parallel", "arbitrary"), name="ssd_bwd")(xbc, xbc, xbc, dtx, alog, dskip, states, dy)


@jax.custom_vjp
def ssd_scan(xbc, dtx, alog, dskip):
    return _ssd_fwd_call(xbc, dtx, alog, dskip)[0]


def _ssd_vjp_fwd(xbc, dtx, alog, dskip):
    y, states = _ssd_fwd_call(xbc, dtx, alog, dskip)
    return y, (xbc, dtx, alog, dskip, states)


def _ssd_vjp_bwd(res, dy):
    dxs, dbm, dcm, ddt, da, dd = _ssd_bwd_call(*res, dy)
    return jnp.concatenate([dxs, dbm, dcm], axis=1), ddt, da, dd


ssd_scan.defvjp(_ssd_vjp_fwd, _ssd_vjp_bwd)


ANY = pl.BlockSpec(memory_space=pl.ANY)


def _my_pos():
    return lax.axis_index("x"), lax.axis_index("y"), lax.axis_index("c")


def _comm_scratch(n):
    return [pltpu.SemaphoreType.DMA((n, N_DEV - 1)), pltpu.SemaphoreType.DMA((n, N_DEV - 1)),
            pltpu.SemaphoreType.DMA((n,))]


def _gather_phases(x_refs, out_refs, send_sems, recv_sems, local_sems):
    x, y, c = _my_pos()
    me, sibling = (x, y, c), (x, y, 1 - c)
    chips = [(1 - x, y), (x, 1 - y), (1 - x, 1 - y)]

    def slot(t, px, py, pc):
        return out_refs[t].at[4 * px + 2 * py + pc]

    def copy(t, k, block, to, own=False):
        return pltpu.make_async_remote_copy(
            src_ref=x_refs[t] if own else slot(t, *block), dst_ref=slot(t, *block),
            send_sem=send_sems.at[t, k], recv_sem=recv_sems.at[t, k], device_id=to, device_id_type=MESH)

    def mine(t):
        return pltpu.make_async_copy(x_refs[t], slot(t, *me), local_sems.at[t])

    def first(t):
        return [copy(t, 0, me, sibling, own=True)] + [copy(t, 1 + j, me, (*chip, c), own=True)
                                                      for j, chip in enumerate(chips)]

    def passed(t):
        return [copy(t, 4 + j, (*chip, c), sibling) for j, chip in enumerate(chips)]

    def start():
        for t in range(len(x_refs)):
            mine(t).start()
            for cp in first(t):
                cp.start()

    def middle():
        for t in range(len(x_refs)):
            for j, chip in enumerate(chips):
                copy(t, 1 + j, (*chip, c), me).wait_recv()
                copy(t, 4 + j, (*chip, c), sibling).start()

    def finish():
        for t in range(len(x_refs)):
            copy(t, 0, sibling, me).wait_recv()
            for j, chip in enumerate(chips):
                copy(t, 4 + j, (*chip, 1 - c), me).wait_recv()
            for cp in first(t) + passed(t):
                cp.wait_send()
            mine(t).wait()

    return start, middle, finish


def _exchange_phases(x_refs, out_refs, send_sems, recv_sems, local_sems):
    x, y, c = _my_pos()
    me = 4 * x + 2 * y + c

    def local(t):
        return pltpu.make_async_copy(x_refs[t].at[me], out_refs[t].at[me], local_sems.at[t])

    def copies(t):
        out = []
        for k in range(1, N_DEV):
            px = 1 - x if k & 4 else x
            py = 1 - y if k & 2 else y
            pc = 1 - c if k & 1 else c
            out.append(pltpu.make_async_remote_copy(
                src_ref=x_refs[t].at[4 * px + 2 * py + pc], dst_ref=out_refs[t].at[me],
                send_sem=send_sems.at[t, k - 1], recv_sem=recv_sems.at[t, k - 1],
                device_id=(px, py, pc), device_id_type=MESH))
        return out

    def start():
        for t in range(len(x_refs)):
            local(t).start()
            for cp in copies(t):
                cp.start()

    def finish():
        for t in range(len(x_refs)):
            for cp in copies(t):
                cp.wait_recv()
            for cp in copies(t):
                cp.wait_send()
            local(t).wait()

    return start, finish


def all_gather(x_shard):
    def body(x_ref, out_ref, send_sems, recv_sems, local_sems):
        start, middle, finish = _gather_phases([x_ref], [out_ref], send_sems, recv_sems, local_sems)
        start()
        middle()
        finish()

    return pl.pallas_call(
        body, out_shape=jax.ShapeDtypeStruct((N_DEV,) + x_shard.shape, x_shard.dtype), in_specs=[ANY],
        out_specs=ANY, scratch_shapes=_comm_scratch(1), name="all_gather")(x_shard)


def exchange(x):
    def body(x_ref, out_ref, send_sems, recv_sems, local_sems):
        start, finish = _exchange_phases([x_ref], [out_ref], send_sems, recv_sems, local_sems)
        start()
        finish()

    return pl.pallas_call(
        body, out_shape=jax.ShapeDtypeStruct(x.shape, x.dtype), in_specs=[ANY], out_specs=ANY,
        scratch_shapes=_comm_scratch(1), name="exchange")(x)


def sum_parts(parts):
    p, r, c_ = parts.shape
    br = _row_block(r, [c_]) if r % 16 == 0 else r

    def body(p_ref, o_ref):
        acc = p_ref[0].astype(F32)
        for i in range(1, p):
            acc = acc + p_ref[i].astype(F32)
        o_ref[...] = acc

    return pl.pallas_call(
        body, grid=(r // br,), in_specs=[pl.BlockSpec((p, br, c_), lambda i: (0, i, 0))],
        out_specs=pl.BlockSpec((br, c_), lambda i: (i, 0)), out_shape=jax.ShapeDtypeStruct((r, c_), F32),
        compiler_params=_cparams("parallel"), name="sum_parts")(parts)


@jax.custom_vjp
def gather_op(shard):
    return all_gather(shard.astype(BF16))


def _gather_op_fwd(shard):
    return all_gather(shard.astype(BF16)), None


def _gather_op_bwd(_, g):
    return (sum_parts(_in_chunks(exchange, g, 1)),)


gather_op.defvjp(_gather_op_fwd, _gather_op_bwd)


def _in_chunks(fn, arr, axis):
    rows = arr.shape[axis]
    pieces = 1
    while (arr.size * arr.dtype.itemsize) // pieces > COMM_BYTES and rows % (2 * pieces) == 0:
        pieces *= 2
    step = rows // pieces
    outs = [fn(lax.slice_in_dim(arr, s, s + step, axis=axis)) for s in range(0, rows, step)]
    return outs[0] if len(outs) == 1 else jnp.concatenate(outs, axis=1)


def reduce_adamw(parts, w, m, v):
    p, r, c_ = parts.shape
    br = _row_block(r, [c_]) if r % 16 == 0 else r
    c1 = 1.0 - ADAM_B1 ** ADAM_STEP
    c2 = 1.0 - ADAM_B2 ** ADAM_STEP

    def body(p_ref, w_ref, m_ref, v_ref, g_ref, d_ref, m2_ref, v2_ref):
        g = p_ref[0].astype(F32)
        for i in range(1, p):
            g = g + p_ref[i].astype(F32)
        m2 = ADAM_B1 * m_ref[...] + (1.0 - ADAM_B1) * g
        v2 = ADAM_B2 * v_ref[...] + (1.0 - ADAM_B2) * (g * g)
        g_ref[...] = g
        m2_ref[...] = m2
        v2_ref[...] = v2
        d_ref[...] = -ADAM_LR * ((m2 / c1) / (jnp.sqrt(v2 / c2) + ADAM_EPS) + ADAM_WD * w_ref[...])

    blk = pl.BlockSpec((br, c_), lambda i: (i, 0))
    return pl.pallas_call(
        body, grid=(r // br,), in_specs=[pl.BlockSpec((p, br, c_), lambda i: (0, i, 0)), blk, blk, blk],
        out_specs=[blk] * 4, out_shape=[jax.ShapeDtypeStruct((r, c_), F32)] * 4,
        compiler_params=_cparams("parallel"), name="reduce_adamw")(parts, w, m, v)


WEIGHTS = ['ffn1_norm', 'ffn1_wi', 'ffn1_wo', 'mix_norm', 'w_in', 'mla_q_norm', 'mla_w_uq', 'mla_kv_norm',
           'mla_w_ukv', 'hgrn_lb_logits', 'hgrn_norm', 'ssm_conv_w', 'ssm_conv_b', 'ssm_a_log', 'ssm_dt_bias',
           'ssm_d', 'ssm_norm', 'w_o_mla', 'w_o_hgrn', 'w_o_ssm', 'w_out', 'ffn2_norm', 'ffn2_wi', 'ffn2_wo',
           'final_norm']
COL_SHARDED = ('ffn1_wi', 'w_in', 'mla_w_uq', 'mla_w_ukv', 'ffn2_wi')
ROW_SHARDED = ('ffn1_wo', 'w_o_mla', 'w_o_hgrn', 'w_o_ssm', 'w_out', 'ffn2_wo')
BIG = tuple(n for n in WEIGHTS if n in COL_SHARDED + ROW_SHARDED)
PRE = ('ffn1_wi', 'ffn1_wo', 'w_in', 'mla_w_uq', 'mla_w_ukv')
POST = ('w_o_mla', 'w_o_hgrn', 'w_o_ssm', 'w_out', 'ffn2_wi', 'ffn2_wo')
CONV_W = 'ssm_conv_w'
REPLICATED = tuple(n for n in WEIGHTS if n not in BIG and n != CONV_W)


def _segment_plan(n, sizes):
    plan, off = [], 0
    for s in sizes:
        a, b = off, off + s
        plan.append([(j, max(a, j * n) - j * n, min(b, (j + 1) * n) - j * n)
                     for j in range(a // n, (b - 1) // n + 1)])
        off = b
    return plan


@functools.partial(jax.custom_vjp, nondiff_argnums=(1,))
def col_segments(blocks, sizes):
    outs = []
    for pieces in _segment_plan(blocks.shape[-1], sizes):
        cut = [blocks[j][:, lo:hi] for j, lo, hi in pieces]
        outs.append(cut[0] if len(cut) == 1 else jnp.concatenate(cut, axis=1))
    return tuple(outs)


def _col_segments_fwd(blocks, sizes):
    return col_segments(blocks, sizes), blocks.shape[-1]


def _col_segments_bwd(sizes, n, gs):
    per_block = [[] for _ in range(N_DEV)]
    for g, pieces in zip(gs, _segment_plan(n, sizes)):
        off = 0
        for j, lo, hi in pieces:
            per_block[j].append(g[:, off:off + hi - lo])
            off += hi - lo
    return (jnp.stack([p[0] if len(p) == 1 else jnp.concatenate(p, axis=1) for p in per_block]),)


col_segments.defvjp(_col_segments_fwd, _col_segments_bwd)


def _rope_tables(t):
    half = MLA_ROPE // 2
    inv = 1.0 / (ROPE_THETA ** (jnp.arange(0, MLA_ROPE, 2, dtype=F32) / MLA_ROPE))
    ang = jnp.arange(t, dtype=F32)[:, None] * inv[None, :]
    reps = LANES // half
    return jnp.tile(jnp.cos(ang), (1, reps)), jnp.tile(jnp.sin(ang), (1, reps))


def _ffn(x, norm, wi, wo):
    dff = wo.shape[0]
    h = rmsnorm(norm[None], x)[0]
    wg, wu = col_segments(wi, (dff, dff))
    return x + 0.5 * mm(swiglu(mm(h, wg), mm(h, wu))[0], wo)


def _per_head(w, widths, pad_to):
    k = w.shape[0]
    w3 = w.reshape(k, -1, sum(widths))
    outs, off = [], 0
    for wd in widths:
        part = w3[:, :, off:off + wd]
        if wd < pad_to:
            part = jnp.pad(part, ((0, 0), (0, 0), (0, pad_to - wd)))
        outs.append(part.reshape(k, -1))
        off += wd
    return outs


def _layer(x, p, lb, cos, sin, carried):
    t, d = x.shape
    inner = 2 * d
    conv_dim = inner + 2 * SSM_GROUPS * SSM_STATE
    n_ssm_heads = inner // SSM_HEADDIM
    x = _ffn(x, p['ffn1_norm'], p['ffn1_wi'], p['ffn1_wo'])

    h = rmsnorm(p['mix_norm'][None], x)[0]
    sizes = (MLA_Q_RANK, MLA_KV_RANK, MLA_ROPE, HG_WIDTH, HG_WIDTH, HG_WIDTH, HG_WIDTH,
             inner, conv_dim, n_ssm_heads, d, d, d)
    (w_q, w_kv, w_kpe, w_hq, w_hf, w_hi, w_hg, w_z, w_xbc, w_dt, w_ga, w_gb, w_gc) = col_segments(p['w_in'], sizes)

    qn = rmsnorm(p['mla_q_norm'][None], mm(h, w_q))[0]
    kvn = rmsnorm(p['mla_kv_norm'][None], mm(h, w_kv))[0]
    w_uq, = col_segments(p['mla_w_uq'], (N_DEV * p['mla_w_uq'].shape[-1],))
    w_ukv, = col_segments(p['mla_w_ukv'], (N_DEV * p['mla_w_ukv'].shape[-1],))
    wq_nope, wq_pe = _per_head(w_uq, (MLA_NOPE, MLA_ROPE), LANES)
    wk_nope, wv = _per_head(w_ukv, (MLA_NOPE, MLA_V), LANES)
    q_nope = mm(qn, wq_nope)
    q_pe = rope(mm(qn, wq_pe), cos, sin)[0]
    k_nope = mm(kvn, wk_nope)
    v = mm(kvn, wv)
    k_rot = rope(mm(h, jnp.pad(w_kpe, ((0, 0), (0, LANES - MLA_ROPE)))), cos, sin)[0]
    q = jnp.concatenate([q_nope.reshape(t, MLA_HEADS, LANES), q_pe.reshape(t, MLA_HEADS, LANES)], axis=2)
    k = jnp.concatenate([k_nope.reshape(t, MLA_HEADS, LANES),
                         jnp.broadcast_to(k_rot[:, None, :], (t, MLA_HEADS, LANES))], axis=2)
    names = tuple(carried)
    o, blocks = attention(q.reshape(t, -1).astype(BF16), k.reshape(t, -1).astype(BF16), v.astype(BF16),
                          tuple(carried[n] for n in names))
    arrived = dict(zip(names, blocks))
    p = dict(p)
    for n in POST:
        p[n] = _usable(n, arrived[(n, 0)])
    y_a = mm(o, p['w_o_mla'])

    o = hgrn_scan(mm(h, w_hq), mm(h, w_hf), mm(h, w_hi), lb[None])
    o = hgrn_out(p['hgrn_norm'][None], o, mm(h, w_hg))[0]
    y_b = mm(o, p['w_o_hgrn'])

    xbc = silu_op(conv(p['ssm_conv_w'], p['ssm_conv_b'][None], mm(h, w_xbc)))[0]
    dtx = dt_expand(p['ssm_dt_bias'][None], mm(h, w_dt))[0]
    y = ssd_scan(xbc, dtx, jnp.repeat(p['ssm_a_log'], SSM_HEADDIM)[None], jnp.repeat(p['ssm_d'], SSM_HEADDIM)[None])
    y = ssm_norm(p['ssm_norm'][None], y, mm(h, w_z))[0]
    y_c = mm(y, p['w_o_ssm'])

    merged = merge(y_a, y_b, y_c, mm(h, w_ga), mm(h, w_gb), mm(h, w_gc))[0]
    x = x + mm(merged, p['w_out'])
    return _ffn(x, p['ffn2_norm'], p['ffn2_wi'], p['ffn2_wo']), arrived


def _usable(name, blocks):
    return blocks.reshape(-1, blocks.shape[-1]) if name in ROW_SHARDED else blocks


def _local_loss(shards, params, x, target):
    depth = params['ffn1_norm'].shape[0]
    cos, sin = _rope_tables(x.shape[0])
    prob = jax.nn.softmax(params['hgrn_lb_logits'], axis=0)
    lower = jnp.cumsum(prob, axis=0) - prob[0:1]
    pre = {n: gather_op(shards[n][0]) for n in PRE}
    for layer in range(depth):
        p = {n: params[n][layer] for n in REPLICATED + (CONV_W,) if n != 'final_norm'}
        for n in PRE:
            p[n] = _usable(n, pre[n])
        carried = {(n, 0): shards[n][layer] for n in POST}
        if layer + 1 < depth:
            carried.update({(n, 1): shards[n][layer + 1] for n in PRE})
        x, arrived = _layer(x, p, lower[layer], cos, sin, carried)
        pre = {n: arrived.get((n, 1)) for n in PRE}
    return jnp.sum(loss_rows(params['final_norm'][None], x, target)[0])


def _pack_vec(arrays, rows):
    flat = jnp.concatenate([a.reshape(-1) for a in arrays])
    return jnp.pad(flat, (0, rows * COMM_COLS - flat.shape[0])).reshape(rows, COMM_COLS)


def _unpack(flat, shapes, lead=()):
    flat = flat.reshape(lead + (-1,))
    outs, off = [], 0
    for s in shapes:
        n = 1
        for dim in s:
            n *= dim
        outs.append(flat[..., off:off + n].reshape(lead + tuple(s)))
        off += n
    return outs


def _round_up(n, m):
    return -(-n // m) * m


def _step(a):
    x = a['x'][0]
    target = a['loss_target'][0]
    me = 4 * lax.axis_index("x") + 2 * lax.axis_index("y") + lax.axis_index("c")

    conv_shape = a[CONV_W].shape
    conv_full_shape = conv_shape[:-1] + (conv_shape[-1] * N_DEV,)
    rep_shapes = [a[n].shape for n in REPLICATED]
    n_small = 1 + sum(a[n].size for n in REPLICATED) + a[CONV_W].size * N_DEV
    small_rows = _round_up(-(-n_small // COMM_COLS), 8)

    params = {}
    conv_blocks = _unpack(all_gather(_pack_vec([a[CONV_W]], small_rows)), [conv_shape], lead=(N_DEV,))[0]
    params[CONV_W] = jnp.moveaxis(conv_blocks, 0, -2).reshape(conv_full_shape)
    for n in REPLICATED:
        params[n] = a[n]
    depth = a['ffn1_norm'].shape[0]
    shards = {n: [a[n][layer] for layer in range(depth)] for n in BIG}

    loss, (gs, gp, gx) = jax.value_and_grad(_local_loss, argnums=(0, 1, 2))(shards, params, x, target)

    big_out = [{}, {}, {}, {}]
    for n in BIG:
        width = a[n].shape[-1]
        grad = jnp.stack(gs[n]).reshape(1, -1, width)
        res = reduce_adamw(grad, *[a[pre + n].reshape(-1, width) for pre in ('', 'm_', 'v_')])
        for kind in range(4):
            big_out[kind][n] = res[kind].reshape(a[n].shape)

    small = _pack_vec([loss.reshape(1)] + [gp[n] for n in REPLICATED] + [gp[CONV_W]], small_rows)
    zero1, one1 = jnp.zeros((1,), F32), jnp.ones((1,), F32)
    zero_c, one_c = jnp.zeros(conv_full_shape, F32), jnp.ones(conv_full_shape, F32)
    small_w = _pack_vec([zero1] + [a[n] for n in REPLICATED] + [zero_c], small_rows)
    small_m = _pack_vec([zero1] + [a['m_' + n] for n in REPLICATED] + [zero_c], small_rows)
    small_v = _pack_vec([one1] + [a['v_' + n] for n in REPLICATED] + [one_c], small_rows)
    res = reduce_adamw(all_gather(small), small_w, small_m, small_v)
    small_out = []
    for r in res:
        pieces = _unpack(r, [(1,)] + rep_shapes + [conv_full_shape])
        small_out.append((pieces[0], dict(zip(REPLICATED, pieces[1:-1])), pieces[-1]))
    total_loss = small_out[0][0][0]

    width = conv_shape[-1]
    g_conv = lax.dynamic_slice_in_dim(small_out[0][2], me * width, width, axis=len(conv_shape) - 1)
    conv_rows = -(-a[CONV_W].size // COMM_COLS)
    conv_res = reduce_adamw(_pack_vec([g_conv], conv_rows)[None],
                            *[_pack_vec([a[pre + CONV_W]], conv_rows) for pre in ('', 'm_', 'v_')])
    conv_out = [_unpack(r, [conv_shape])[0] for r in conv_res]

    outs = [total_loss, gx[None]]
    for kind in range(4):
        for n in WEIGHTS:
            if n in BIG:
                outs.append(big_out[kind][n])
            elif n == CONV_W:
                outs.append(conv_out[kind])
            else:
                outs.append(small_out[kind][1][n])
    return tuple(outs)


def kernel(x, ffn1_norm, ffn1_wi, ffn1_wo, mix_norm, w_in, mla_q_norm, mla_w_uq, mla_kv_norm, mla_w_ukv, hgrn_lb_logits, hgrn_norm, ssm_conv_w, ssm_conv_b, ssm_a_log, ssm_dt_bias, ssm_d, ssm_norm, w_o_mla, w_o_hgrn, w_o_ssm, w_out, ffn2_norm, ffn2_wi, ffn2_wo, final_norm, loss_target, m_ffn1_norm, m_ffn1_wi, m_ffn1_wo, m_mix_norm, m_w_in, m_mla_q_norm, m_mla_w_uq, m_mla_kv_norm, m_mla_w_ukv, m_hgrn_lb_logits, m_hgrn_norm, m_ssm_conv_w, m_ssm_conv_b, m_ssm_a_log, m_ssm_dt_bias, m_ssm_d, m_ssm_norm, m_w_o_mla, m_w_o_hgrn, m_w_o_ssm, m_w_out, m_ffn2_norm, m_ffn2_wi, m_ffn2_wo, m_final_norm, v_ffn1_norm, v_ffn1_wi, v_ffn1_wo, v_mix_norm, v_w_in, v_mla_q_norm, v_mla_w_uq, v_mla_kv_norm, v_mla_w_ukv, v_hgrn_lb_logits, v_hgrn_norm, v_ssm_conv_w, v_ssm_conv_b, v_ssm_a_log, v_ssm_dt_bias, v_ssm_d, v_ssm_norm, v_w_o_mla, v_w_o_hgrn, v_w_o_ssm, v_w_out, v_ffn2_norm, v_ffn2_wi, v_ffn2_wo, v_final_norm):
    return _step(dict(locals()))
```
